```python
import numpy as np
import jax
import jax.numpy as jnp
from jax import lax

D_MODEL = 2048
BATCH = 32
SEQ = 256
DEPTH = 4
DEC_BATCH = 8
DEC_SEQ = 4096
PAST_LEN = 256

GRID_W = 64
M_HEADS = 8
M_HEAD_DIM = D_MODEL // (2 * M_HEADS)
M_WIDTH = M_HEADS * M_HEAD_DIM
CHUNK = 64
CV_WIDTH = D_MODEL // 4
CONV_TAPS = 31
NA_HEADS = 8
NA_HEAD_DIM = D_MODEL // (4 * NA_HEADS)
NA_WIDTH = NA_HEADS * NA_HEAD_DIM
WIN_ROWS_MAX = 8
WIN_COLS = 16
QCOL_BLK = 16
KCOL_BLK = 32
N_CBLK = GRID_W // QCOL_BLK
CTX_QBLK = 128
MIX_WIDTH = M_WIDTH + CV_WIDTH + NA_WIDTH
N_GATES = 2 * 2 * M_HEADS
IN_WIDTH = 4 * M_WIDTH + N_GATES + 2 * CV_WIDTH + 3 * NA_WIDTH
D_FF = 4 * D_MODEL
ROPE_BASE = 10000.0
EPS = 1e-6
NEG_INF = -1e30

kernel_name = "hybrid_mlstm_conv_natten_diffusion_step"


def _in_split_points():
    sizes = [M_WIDTH] * 4 + [N_GATES] + [CV_WIDTH] * 2 + [NA_WIDTH] * 3
    return [int(s) for s in np.cumsum(sizes)[:-1]]


def _na_col_tables():
    qcol = np.arange(GRID_W).reshape(N_CBLK, QCOL_BLK)
    kstart = np.clip(np.arange(N_CBLK) * QCOL_BLK - WIN_COLS // 2, 0, GRID_W - KCOL_BLK)
    kcol = kstart[:, None] + np.arange(KCOL_BLK)[None, :]
    cs = np.clip(qcol - WIN_COLS // 2, 0, GRID_W - WIN_COLS)
    valid = (kcol[:, None, :] >= cs[..., None]) & (kcol[:, None, :] < cs[..., None] + WIN_COLS)
    dc_idx = np.clip(kcol[:, None, :] - qcol[:, :, None] + WIN_COLS - 1, 0, 2 * WIN_COLS - 2)
    return kcol, dc_idx, valid


def rms_norm(x, g):
    xf = x.astype(jnp.float32)
    y = xf * lax.rsqrt(jnp.mean(xf * xf, axis=-1, keepdims=True) + EPS)
    return (y * g.astype(jnp.float32)).astype(x.dtype)


def modulation(cond, w_ada_l, b_ada_l):
    return jax.nn.silu(cond) @ w_ada_l + b_ada_l


def rope_2d(x):
    T, dh = x.shape[1], x.shape[-1]
    half = dh // 2
    nf = half // 2
    t = jnp.arange(T)
    inv = ROPE_BASE ** (-jnp.arange(nf, dtype=jnp.float32) / nf)
    xf = x.astype(jnp.float32)

    def rot(xp, pos):
        ang = pos.astype(jnp.float32)[:, None] * inv[None, :]
        cos = jnp.cos(ang)[None, :, None, :]
        sin = jnp.sin(ang)[None, :, None, :]
        x1, x2 = xp[..., :nf], xp[..., nf:]
        return jnp.concatenate([x1 * cos - x2 * sin, x1 * sin + x2 * cos], axis=-1)

    out = jnp.concatenate([rot(xf[..., :half], t // GRID_W), rot(xf[..., half:], t % GRID_W)], axis=-1)
    return out.astype(x.dtype)


def mlstm_scan(q, k, v, ig, lf, C0, n0, m0):
    B, T, H, _ = q.shape
    nc = T // CHUNK
    f32 = jnp.float32

    def chunks(a):
        return a.astype(f32).reshape((B, nc, CHUNK) + a.shape[2:]).swapaxes(0, 1)

    causal = jnp.tril(jnp.ones((CHUNK, CHUNK), dtype=bool))[None, :, :, None]

    def step(carry, xs):
        C, n, m = carry
        qc, kc, vc, ic, fc = xs
        b = jnp.cumsum(fc, axis=1)
        dmat = jnp.where(causal, b[:, :, None, :] - b[:, None, :, :] + ic[:, None, :, :], -jnp.inf)
        inter = b + m[:, None, :]
        mj = jnp.maximum(inter, dmat.max(axis=2))
        s = jnp.einsum('bjhd,bshd->bjsh', qc, kc) * jnp.exp(dmat - mj[:, :, None, :])
        w_int = jnp.exp(inter - mj)
        num = jnp.einsum('bjsh,bshe->bjhe', s, vc) + w_int[..., None] * jnp.einsum('bjhd,bhde->bjhe', qc, C)
        den = s.sum(axis=2) + w_int * jnp.einsum('bjhd,bhd->bjh', qc, n)
        h = num / jnp.maximum(jnp.abs(den), jnp.exp(-mj))[..., None]
        b_end = b[:, -1]
        g = b_end[:, None, :] - b + ic
        m_new = jnp.maximum(b_end + m, g.max(axis=1))
        wk = jnp.exp(g - m_new[:, None, :])
        decay = jnp.exp(b_end + m - m_new)
        C_new = decay[..., None, None] * C + jnp.einsum('bsh,bshd,bshe->bhde', wk, kc, vc)
        n_new = decay[..., None] * n + jnp.einsum('bsh,bshd->bhd', wk, kc)
        return (C_new, n_new, m_new), h

    init = (C0.astype(f32), n0.astype(f32), m0.astype(f32))
    (C, n, m), h = lax.scan(step, init, tuple(chunks(a) for a in (q, k, v, ig, lf)))
    h = h.swapaxes(0, 1).reshape(B, T, H, v.shape[-1])
    return h, C, n, m


def mlstm_bidir(q, k, v, ig, lf, state0):
    C0, n0, m0 = state0
    hf, Cf, nf, mf = mlstm_scan(q, k, v, ig[:, :, 0], lf[:, :, 0], C0[:, 0], n0[:, 0], m0[:, 0])
    flip = lambda a: jnp.flip(a, axis=1)
    hb, Cb, nb, mb = mlstm_scan(flip(q), flip(k), flip(v), flip(ig[:, :, 1]), flip(lf[:, :, 1]),
                                C0[:, 1], n0[:, 1], m0[:, 1])
    h = hf + flip(hb)
    dt = q.dtype
    st = (jnp.stack([Cf, Cb], axis=1).astype(dt), jnp.stack([nf, nb], axis=1).astype(dt),
          jnp.stack([mf, mb], axis=1).astype(dt))
    return h, st


def conv_module(a, g, w_dw, b_dw, ln_g, ln_b, w_pw):
    u = a * jax.nn.sigmoid(g)
    y = lax.conv_general_dilated(u, w_dw[:, None, :], window_strides=(1,),
                                 padding=[(CONV_TAPS // 2, CONV_TAPS // 2)],
                                 dimension_numbers=('NWC', 'WIO', 'NWC'),
                                 feature_group_count=CV_WIDTH) + b_dw
    yf = y.astype(jnp.float32)
    mu = jnp.mean(yf, axis=-1, keepdims=True)
    var = jnp.mean(jnp.square(yf - mu), axis=-1, keepdims=True)
    yf = (yf - mu) * lax.rsqrt(var + EPS) * ln_g.astype(jnp.float32) + ln_b.astype(jnp.float32)
    yf = yf * jax.nn.sigmoid(yf)
    return yf.astype(a.dtype) @ w_pw


def ctx_attention(q, k, v):
    B, S, H, dh = q.shape
    qb = q.reshape(B, S // CTX_QBLK, CTX_QBLK, H, dh).swapaxes(0, 1)

    def blk(qi):
        s = jnp.einsum('bqhd,bkhd->bhqk', qi, k).astype(jnp.float32) * (dh ** -0.5)
        p = jax.nn.softmax(s, axis=-1).astype(v.dtype)
        return jnp.einsum('bhqk,bkhd->bqhd', p, v)

    o = lax.map(blk, qb)
    return o.swapaxes(0, 1).reshape(B, S, H * dh)


def na_latent(q, k, v, k_ctx, v_ctx, rpb):
    B, N, H, dh = q.shape
    rows = N // GRID_W
    wr = min(WIN_ROWS_MAX, rows)
    nloc = wr * KCOL_BLK
    kcol, dc_idx, valid = _na_col_tables()
    kcol_j = jnp.asarray(kcol)
    mask = jnp.asarray(np.broadcast_to(valid[:, :, None, :], (N_CBLK, QCOL_BLK, wr, KCOL_BLK))
                       .reshape(N_CBLK, QCOL_BLK, nloc)[:, None])
    rpb_c = rpb[:, :, jnp.asarray(dc_idx)]
    kg = k.reshape(B, rows, GRID_W, H, dh)
    vg = v.reshape(B, rows, GRID_W, H, dh)
    qg = q.reshape(B, rows, N_CBLK, QCOL_BLK, H, dh).swapaxes(0, 1)
    scale = dh ** -0.5

    def row_step(args):
        q_r, r = args
        rs = jnp.clip(r - wr // 2, 0, rows - wr)
        k_rows = lax.dynamic_slice_in_dim(kg, rs, wr, axis=1)
        v_rows = lax.dynamic_slice_in_dim(vg, rs, wr, axis=1)
        k_blk = k_rows[:, :, kcol_j].transpose(0, 2, 1, 3, 4, 5).reshape(B, N_CBLK, nloc, H, dh)
        v_blk = v_rows[:, :, kcol_j].transpose(0, 2, 1, 3, 4, 5).reshape(B, N_CBLK, nloc, H, dh)
        dr = rs + jnp.arange(wr) - r + (WIN_ROWS_MAX - 1)
        bias = rpb_c[:, dr].transpose(2, 0, 3, 1, 4).reshape(N_CBLK, H, QCOL_BLK, nloc)
        s_loc = jnp.einsum('bcqhd,bckhd->bchqk', q_r, k_blk).astype(jnp.float32) * scale + bias.astype(jnp.float32)
        s_loc = jnp.where(mask, s_loc, NEG_INF)
        s_ctx = jnp.einsum('bcqhd,bkhd->bchqk', q_r, k_ctx).astype(jnp.float32) * scale
        p = jax.nn.softmax(jnp.concatenate([s_loc, s_ctx], axis=-1), axis=-1).astype(v.dtype)
        return (jnp.einsum('bchqk,bckhd->bcqhd', p[..., :nloc], v_blk)
                + jnp.einsum('bchqk,bkhd->bcqhd', p[..., nloc:], v_ctx))

    out = lax.map(row_step, (qg, jnp.arange(rows)))
    return out.swapaxes(0, 1).reshape(B, N, H * dh)


def mixer(h, w_in_l, b_gate_l, g_mlstm_l, w_dw_l, b_dw_l, ln_g_l, ln_b_l, w_pw_l, rpb_l, w_out_l, state0, ctx_kv):
    B, T, _ = h.shape
    latent = ctx_kv is not None
    qm, km, vm, om, gates, ca, cg, qn, kn, vn = jnp.split(h @ w_in_l, _in_split_points(), axis=-1)
    qm = qm.reshape(B, T, M_HEADS, M_HEAD_DIM)
    km = km.reshape(B, T, M_HEADS, M_HEAD_DIM) * (M_HEAD_DIM ** -0.5)
    vm = vm.reshape(B, T, M_HEADS, M_HEAD_DIM)
    if latent:
        qm, km = rope_2d(qm), rope_2d(km)
    gt = gates.reshape(B, T, 2, 2, M_HEADS).astype(jnp.float32) + b_gate_l.astype(jnp.float32)
    ig = gt[:, :, :, 0]
    lf = jax.nn.log_sigmoid(gt[:, :, :, 1])
    hm, st = mlstm_bidir(qm, km, vm, ig, lf, state0)
    hm = hm * lax.rsqrt(jnp.mean(hm * hm, axis=-1, keepdims=True) + EPS) \
        * g_mlstm_l.reshape(M_HEADS, M_HEAD_DIM).astype(jnp.float32)
    y_m = (hm.reshape(B, T, M_WIDTH) * jax.nn.sigmoid(om.astype(jnp.float32))).astype(h.dtype)
    y_c = conv_module(ca, cg, w_dw_l, b_dw_l, ln_g_l, ln_b_l, w_pw_l)
    qn = qn.reshape(B, T, NA_HEADS, NA_HEAD_DIM)
    kn = kn.reshape(B, T, NA_HEADS, NA_HEAD_DIM)
    vn = vn.reshape(B, T, NA_HEADS, NA_HEAD_DIM)
    if latent:
        y_n = na_latent(qn, kn, vn, ctx_kv[0], ctx_kv[1], rpb_l)
    else:
        y_n = ctx_attention(qn, kn, vn)
    y = jnp.concatenate([y_m, y_c, y_n], axis=-1) @ w_out_l
    return y, kn, vn, st


def trunk_layer(x, mod, mix_w, norm_ffn_w, state0, ctx_kv):
    g_pre_mix, g_post_mix, g_pre_mlp, g_post_mlp, w_ff1, w_ff2 = norm_ffn_w
    sh_a, sc_a, ga_a, sh_m, sc_m, ga_m = jnp.split(mod[:, None, :], 6, axis=-1)
    h = rms_norm(x, g_pre_mix) * (1 + sc_a) + sh_a
    y, kn, vn, st = mixer(h, *mix_w, state0, ctx_kv)
    x = x + ga_a * rms_norm(y, g_post_mix)
    h = rms_norm(x, g_pre_mlp) * (1 + sc_m) + sh_m
    y = jnp.square(jax.nn.relu(h @ w_ff1)) @ w_ff2
    x = x + ga_m * rms_norm(y, g_post_mlp)
    return x, kn, vn, st


def setup_inputs(seed: int = 0) -> dict:
    key = jax.random.key(seed)
    ks = jax.random.split(key, 32)
    f32 = jnp.float32
    nrm = lambda k, shape, s: jax.random.normal(k, shape, f32) * s
    gain = lambda k, shape: 1.0 + 0.02 * jax.random.normal(k, shape, f32)
    b_gate = jnp.concatenate([nrm(ks[0], (DEPTH, 2, 1, M_HEADS), 0.1),
                              jax.random.uniform(ks[1], (DEPTH, 2, 1, M_HEADS), f32, 3.0, 6.0)], axis=2)
    return {
        "x_prompt": nrm(ks[2], (BATCH, SEQ, D_MODEL), 1.0),
        "x_sample": nrm(ks[3], (DEC_BATCH, DEC_SEQ, D_MODEL), 1.0),
        "c": nrm(ks[4], (DEC_BATCH, D_MODEL), 1.0),
        "cache_k": nrm(ks[5], (DEC_BATCH, DEPTH, PAST_LEN, NA_HEADS, NA_HEAD_DIM), 1.0),
        "cache_v": nrm(ks[6], (DEC_BATCH, DEPTH, PAST_LEN, NA_HEADS, NA_HEAD_DIM), 1.0),
        "state_C": nrm(ks[7], (DEC_BATCH, DEPTH, 2, M_HEADS, M_HEAD_DIM, M_HEAD_DIM), 0.5),
        "state_n": nrm(ks[8], (DEC_BATCH, DEPTH, 2, M_HEADS, M_HEAD_DIM), 0.5),
        "state_m": 1.0 + nrm(ks[9], (DEC_BATCH, DEPTH, 2, M_HEADS), 0.5),
        "c_ctx": nrm(ks[10], (D_MODEL,), 1.0),
        "w_ada": nrm(ks[11], (DEPTH, D_MODEL, 6 * D_MODEL), 0.5 * D_MODEL ** -0.5),
        "b_ada": nrm(ks[12], (DEPTH, 6 * D_MODEL), 0.02),
        "g_pre_mix": gain(ks[13], (DEPTH, D_MODEL)),
        "g_post_mix": gain(ks[14], (DEPTH, D_MODEL)),
        "g_pre_mlp": gain(ks[15], (DEPTH, D_MODEL)),
        "g_post_mlp": gain(ks[16], (DEPTH, D_MODEL)),
        "w_in": nrm(ks[17], (DEPTH, D_MODEL, IN_WIDTH), D_MODEL ** -0.5),
        "b_gate": b_gate,
        "g_mlstm": gain(ks[18], (DEPTH, M_WIDTH)),
        "w_dw": nrm(ks[19], (DEPTH, CONV_TAPS, CV_WIDTH), CONV_TAPS ** -0.5),
        "b_dw": nrm(ks[20], (DEPTH, CV_WIDTH), 0.02),
        "ln_g": gain(ks[21], (DEPTH, CV_WIDTH)),
        "ln_b": nrm(ks[22], (DEPTH, CV_WIDTH), 0.02),
        "w_pw": nrm(ks[23], (DEPTH, CV_WIDTH, CV_WIDTH), CV_WIDTH ** -0.5),
        "rpb": nrm(ks[24], (DEPTH, NA_HEADS, 2 * WIN_ROWS_MAX - 1, 2 * WIN_COLS - 1), 0.1),
        "w_out": nrm(ks[25], (DEPTH, MIX_WIDTH, D_MODEL), MIX_WIDTH ** -0.5),
        "w_ff1": nrm(ks[26], (DEPTH, D_MODEL, D_FF), D_MODEL ** -0.5),
        "w_ff2": nrm(ks[27], (DEPTH, D_FF, D_MODEL), D_FF ** -0.5),
    }


def reference(x_prompt, x_sample, c, cache_k, cache_v, state_C, state_n, state_m, c_ctx, w_ada, b_ada,
              g_pre_mix, g_post_mix, g_pre_mlp, g_post_mlp, w_in, b_gate, g_mlstm, w_dw, b_dw, ln_g, ln_b,
              w_pw, rpb, w_out, w_ff1, w_ff2):
    f32 = jnp.float32
    nb = x_prompt.shape[0]
    zero_state = (jnp.zeros((nb, 2, M_HEADS, M_HEAD_DIM, M_HEAD_DIM), f32),
                  jnp.zeros((nb, 2, M_HEADS, M_HEAD_DIM), f32),
                  jnp.zeros((nb, 2, M_HEADS), f32))
    xc, xl = x_prompt, x_sample
    ks_, vs_, Cs_, ns_, ms_ = [], [], [], [], []
    for l in range(DEPTH):
        mix_w = (w_in[l], b_gate[l], g_mlstm[l], w_dw[l], b_dw[l], ln_g[l], ln_b[l], w_pw[l], rpb[l], w_out[l])
        norm_ffn_w = (g_pre_mix[l], g_post_mix[l], g_pre_mlp[l], g_post_mlp[l], w_ff1[l], w_ff2[l])
        mod_ctx = modulation(c_ctx[None, :], w_ada[l], b_ada[l])
        mod_lat = modulation(c, w_ada[l], b_ada[l])
        xc, kc, vc, st = trunk_layer(xc, mod_ctx, mix_w, norm_ffn_w, zero_state, None)
        ks_.append(kc)
        vs_.append(vc)
        Cs_.append(st[0])
        ns_.append(st[1])
        ms_.append(st[2])
        xl, _, _, _ = trunk_layer(xl, mod_lat, mix_w, norm_ffn_w,
                                  (state_C[:, l], state_n[:, l], state_m[:, l]),
                                  (cache_k[:, l], cache_v[:, l]))
    new_cache_k = jnp.stack(ks_, axis=1)
    new_cache_v = jnp.stack(vs_, axis=1)
    new_state_C = jnp.stack(Cs_, axis=1)
    new_state_n = jnp.stack(ns_, axis=1)
    new_state_m = jnp.stack(ms_, axis=1)
    return (xc, xl, new_cache_k, new_cache_v, new_state_C, new_state_n, new_state_m)
```

```python
import functools
from typing import NamedTuple

import numpy as np
import jax
import jax.numpy as jnp
from jax import lax
from jax.experimental import pallas as pl
from jax.experimental.pallas import tpu as pltpu

F32 = jnp.float32
BF16 = jnp.bfloat16

EPS = 1e-6
NEG_INF = -1e30
ROPE_BASE = 10000.0

LANES = 128
M_HEAD_DIM = 128
NA_HEAD_DIM = 64
GRID_W = 64
WIN_ROWS = 8
WIN_COLS = 16
CONV_TAPS = 31
CONV_HALO = 16
VMEM_LIMIT = 52 * 1024 * 1024


class Cfg(NamedTuple):
    d_model: int
    batch: int
    seq: int
    depth: int
    dec_batch: int
    dec_seq: int
    past_len: int
    m_heads: int
    cv_width: int
    na_heads: int
    d_ff: int
    tm: int
    tn_in: int
    th_ff: int
    chunk: int
    conv_tt: int

    @property
    def m_width(self):
        return self.m_heads * M_HEAD_DIM

    @property
    def na_width(self):
        return self.na_heads * NA_HEAD_DIM

    @property
    def n_gates(self):
        return 4 * self.m_heads

    @property
    def n_ctx(self):
        return self.batch * self.seq

    @property
    def n_lat(self):
        return self.dec_batch * self.dec_seq

    @property
    def n_tok(self):
        return self.n_ctx + self.n_lat

    @property
    def col_ca(self):
        return 4 * self.m_width

    @property
    def col_qn(self):
        return self.col_ca + 2 * self.cv_width

    @property
    def col_gates(self):
        return self.col_qn + 3 * self.na_width

    @property
    def in_cols(self):
        raw = self.col_gates + LANES
        return -(-raw // self.tn_in) * self.tn_in


def _cparams(sem):
    return pltpu.CompilerParams(dimension_semantics=sem, vmem_limit_bytes=VMEM_LIMIT)


def _rms(x):
    return x * lax.rsqrt(jnp.mean(x * x, axis=-1, keepdims=True) + EPS)


def _dot(a, b):
    return jnp.dot(a, b, preferred_element_type=F32)


def _dot_nt(a, b):
    return lax.dot_general(a, b, (((1,), (1,)), ((), ())), preferred_element_type=F32)


def _dot_tn(a, b):
    return lax.dot_general(a, b, (((0,), (0,)), ((), ())), preferred_element_type=F32)


def _mod_kernel(c_ref, w_ref, b_ref, o_ref):
    c = c_ref[...]
    s = (c * jax.nn.sigmoid(c)).astype(BF16)
    o_ref[...] = _dot(s, w_ref[...].astype(BF16)) + b_ref[...]


def _modulation(cond, w_ada, b_ada, tn):
    depth, d, n6 = w_ada.shape
    rows = cond.shape[0]
    return pl.pallas_call(
        _mod_kernel,
        out_shape=jax.ShapeDtypeStruct((depth, rows, n6), F32),
        grid=(depth, n6 // tn),
        in_specs=[
            pl.BlockSpec((rows, d), lambda l, j: (0, 0)),
            pl.BlockSpec((None, d, tn), lambda l, j: (l, 0, j)),
            pl.BlockSpec((None, 1, tn), lambda l, j: (l, 0, j)),
        ],
        out_specs=pl.BlockSpec((None, rows, tn), lambda l, j: (l, 0, j)),
        compiler_params=_cparams(("arbitrary", "arbitrary")),
        name="adaln_modulation",
    )(cond, w_ada, b_ada.reshape(depth, 1, n6))


def _mod_row_map(cfg, which):
    nct = cfg.n_ctx // cfg.tm
    tpb = cfg.dec_seq // cfg.tm

    def index_map(i, *_):
        row = jnp.where(i < nct, 0, 1 + (i - nct) // tpb)
        return (row, which, 0, 0)

    return index_map


def _mod_spec(cfg, which):
    return pl.BlockSpec((None, None, 1, cfg.d_model), _mod_row_map(cfg, which))


ROW_CHUNK = 128


def _inproj_kernel(x_ref, sh_ref, sc_ref, g_ref, w_ref, wgt_ref, o_ref, gt_ref, h_scr):
    j = pl.program_id(1)
    tm = x_ref.shape[0]

    @pl.when(j == 0)
    def _():
        def body(r, carry):
            rows = pl.ds(pl.multiple_of(r * ROW_CHUNK, ROW_CHUNK), ROW_CHUNK)
            h = _rms(x_ref[rows, :]) * g_ref[...] * (1.0 + sc_ref[...]) + sh_ref[...]
            h_scr[rows, :] = h.astype(BF16)
            return carry

        lax.fori_loop(0, tm // ROW_CHUNK, body, 0)
        gt_ref[...] = _dot_nt(wgt_ref[...], h_scr[...])

    o_ref[...] = _dot(h_scr[...], w_ref[...])


def _inproj(cfg, x, modl, g, w_in_p, w_gt):
    d, tm, tn = cfg.d_model, cfg.tm, cfg.tn_in
    ng = cfg.n_gates
    return pl.pallas_call(
        _inproj_kernel,
        out_shape=(jax.ShapeDtypeStruct((cfg.n_tok, cfg.in_cols), F32),
                   jax.ShapeDtypeStruct((ng, cfg.n_tok), F32)),
        grid=(cfg.n_tok // tm, cfg.in_cols // tn),
        in_specs=[
            pl.BlockSpec((tm, d), lambda i, j: (i, 0)),
            _mod_spec(cfg, 0),
            _mod_spec(cfg, 1),
            pl.BlockSpec((1, d), lambda i, j: (0, 0)),
            pl.BlockSpec((d, tn), lambda i, j: (0, j)),
            pl.BlockSpec((ng, d), lambda i, j: (0, 0)),
        ],
        out_specs=(pl.BlockSpec((tm, tn), lambda i, j: (i, j)),
                   pl.BlockSpec((ng, tm), lambda i, j: (0, i))),
        scratch_shapes=[pltpu.VMEM((tm, d), BF16)],
        compiler_params=_cparams(("arbitrary", "arbitrary")),
        name="in_projection",
    )(x, modl, modl, g.reshape(1, d), w_in_p, w_gt)


def _rope(x, cos, sin_signed, even_block):
    swapped = jnp.where(even_block, pltpu.roll(x, 96, 1), pltpu.roll(x, 32, 1))
    return x * cos + swapped * sin_signed


def _split3_rows(x):
    hi = x.astype(BF16).astype(F32)
    r1 = x - hi
    mid = r1.astype(BF16).astype(F32)
    lo = (r1 - mid).astype(BF16).astype(F32)
    row = lax.broadcasted_iota(jnp.int32, (8, x.shape[1]), 0)
    stacked = jnp.where(row == 0, hi, jnp.where(row == 1, mid, jnp.where(row == 2, lo, 0.0)))
    return stacked.astype(BF16)


def _log_sigmoid(x):
    return jnp.minimum(x, 0.0) - jnp.log1p(jnp.exp(-jnp.abs(x)))


def _mlstm_kernel(*refs, seq_len, chunk, heads, rope, init_state, emit_state):
    it = iter(refs)
    q_ref, k_ref, v_ref, om_ref, gt_ref, bias_ref, gm_ref = (next(it) for _ in range(7))
    cos_ref = sin_ref = c0_ref = n0_ref = m0_ref = None
    if rope:
        cos_ref, sin_ref = next(it), next(it)
    if init_state:
        c0_ref, n0_ref, m0_ref = next(it), next(it), next(it)
    o_ref = next(it)
    co_ref = no_ref = mo_ref = None
    if emit_state:
        co_ref, no_ref, mo_ref = next(it), next(it), next(it)
    hs, cext, msc = next(it), next(it), next(it)

    L = chunk
    dk = M_HEAD_DIM
    head = pl.program_id(1)
    nchunks = seq_len // L

    hs[...] = jnp.zeros_like(hs)
    for d in range(2):
        if init_state:
            cext[d, :, 0:dk] = c0_ref[d]
            cext[d, :, dk:2 * dk] = jnp.transpose(jnp.broadcast_to(n0_ref[d], (dk, dk)))
            msc[d] = m0_ref[d]
        else:
            cext[d] = jnp.zeros((dk, 2 * dk), F32)
            msc[d] = jnp.zeros((1, LANES), F32)

    tt = lax.broadcasted_iota(jnp.int32, (L, L), 0)
    ss = lax.broadcasted_iota(jnp.int32, (L, L), 1)
    cum_mats = ((tt <= ss).astype(BF16), (tt >= ss).astype(BF16))
    causal = (ss <= tt, ss >= tt)
    lane_blk = lax.broadcasted_iota(jnp.int32, (L, dk), 1) // 32
    even_block = (lane_blk % 2) == 0
    ones_ext = jnp.ones((L, dk), F32)
    gate_id = lax.broadcasted_iota(jnp.int32, (4 * heads, L), 0)

    def step(d, r0):
        rows = pl.ds(r0, L)
        q = q_ref[rows, :]
        k = k_ref[rows, :] * (dk ** -0.5)
        v = v_ref[rows, :]
        if rope:
            cos, sin = cos_ref[rows, :], sin_ref[rows, :]
            q = _rope(q, cos, sin, even_block)
            k = _rope(k, cos, sin, even_block)
        row_i = d * 2 * heads + head
        gates = gt_ref[:, rows] + bias_ref[...]
        gate_row = lambda r: jnp.sum(jnp.where(gate_id == r, gates, 0.0), axis=0, keepdims=True)
        xi = gate_row(row_i)
        lf = _log_sigmoid(gate_row(row_i + heads))
        cum = _dot(_split3_rows(lf), cum_mats[d])
        b_row = cum[0:1] + cum[1:2] + cum[2:3]
        tot = jnp.sum(lf, axis=1, keepdims=True)
        m_prev = msc[d][:, 0:1]

        b_col = jnp.transpose(jnp.broadcast_to(b_row, (L, L)))
        dm = jnp.where(causal[d], b_col + (xi - b_row), -jnp.inf)
        inter = b_col[:, 0:1] + m_prev
        mj = jnp.maximum(inter, jnp.max(dm, axis=1, keepdims=True))
        s = _dot_nt(q.astype(BF16), k.astype(BF16)) * jnp.exp(dm - mj)
        w_int = jnp.exp(inter - mj)
        v_ext = jnp.concatenate([v, ones_ext], axis=1).astype(BF16)
        c_prev = cext[d]
        nd = _dot(s.astype(BF16), v_ext) + w_int * _dot(q.astype(BF16), c_prev.astype(BF16))
        den = jnp.maximum(jnp.abs(nd[:, dk:2 * dk]), jnp.exp(-mj))
        hs[rows, :] += nd[:, 0:dk] / den

        g_row = tot - b_row + xi
        m_new = jnp.maximum(tot + m_prev, jnp.max(g_row, axis=1, keepdims=True))
        wk_col = jnp.transpose(jnp.broadcast_to(jnp.exp(g_row - m_new), (dk, L)))
        decay = jnp.exp(tot + m_prev - m_new)
        cext[d] = decay * c_prev + _dot_tn((k * wk_col).astype(BF16), v_ext)
        msc[d] = jnp.broadcast_to(m_new, (1, LANES))

    def chunk_body(c, carry):
        step(0, pl.multiple_of(c * L, L))
        step(1, pl.multiple_of((nchunks - 1 - c) * L, L))
        return carry

    lax.fori_loop(0, nchunks, chunk_body, 0)

    def finish(r, carry):
        rows = pl.ds(pl.multiple_of(r * ROW_CHUNK, ROW_CHUNK), ROW_CHUNK)
        y = _rms(hs[rows, :]) * gm_ref[...] * jax.nn.sigmoid(om_ref[rows, :])
        o_ref[rows, :] = y.astype(o_ref.dtype)
        return carry

    lax.fori_loop(0, seq_len // ROW_CHUNK, finish, 0)

    if emit_state:
        for d in range(2):
            co_ref[d] = cext[d, :, 0:dk]
            no_ref[d] = jnp.transpose(cext[d, :, dk:2 * dk])[0:1, :]
            mo_ref[d] = msc[d]


def _mlstm(cfg, p, gt, bias_col, g_mlstm, *, n_seq, seq_len, row0, rope_tabs=None, state0=None, emit_state=False):
    heads, dk = cfg.m_heads, M_HEAD_DIM
    sb = row0 // seq_len
    assert sb * seq_len == row0 and seq_len % cfg.chunk == 0

    def col(block0):
        return pl.BlockSpec((seq_len, dk), lambda b, h: (sb + b, block0 + h))

    in_specs = [col(0), col(heads), col(2 * heads), col(3 * heads),
                pl.BlockSpec((cfg.n_gates, seq_len), lambda b, h: (0, sb + b)),
                pl.BlockSpec((cfg.n_gates, 1), lambda b, h: (0, 0)),
                pl.BlockSpec((None, 1, dk), lambda b, h: (h, 0, 0))]
    args = [p, p, p, p, gt, bias_col, g_mlstm.reshape(heads, 1, dk)]
    if rope_tabs is not None:
        in_specs += [pl.BlockSpec((seq_len, dk), lambda b, h: (0, 0))] * 2
        args += list(rope_tabs)
    state_spec = lambda r, c: pl.BlockSpec((None, 2, None, r, c), lambda b, h: (b, 0, h, 0, 0))
    if state0 is not None:
        c0, n0, m0 = state0
        in_specs += [state_spec(dk, dk), state_spec(1, dk), state_spec(1, LANES)]
        args += [c0, n0.reshape(n_seq, 2, heads, 1, dk),
                 jnp.broadcast_to(m0[..., None, None], (n_seq, 2, heads, 1, LANES))]
    out_shape = [jax.ShapeDtypeStruct((n_seq * seq_len, cfg.m_width), BF16)]
    out_specs = [pl.BlockSpec((seq_len, dk), lambda b, h: (b, h))]
    if emit_state:
        out_shape += [jax.ShapeDtypeStruct((n_seq, 2, heads, dk, dk), F32),
                      jax.ShapeDtypeStruct((n_seq, 2, heads, 1, dk), F32),
                      jax.ShapeDtypeStruct((n_seq, 2, heads, 1, LANES), F32)]
        out_specs += [state_spec(dk, dk), state_spec(1, dk), state_spec(1, LANES)]
    kern = functools.partial(_mlstm_kernel, seq_len=seq_len, chunk=cfg.chunk, heads=heads,
                             rope=rope_tabs is not None, init_state=state0 is not None, emit_state=emit_state)
    return pl.pallas_call(
        kern,
        out_shape=tuple(out_shape),
        grid=(n_seq, heads),
        in_specs=in_specs,
        out_specs=tuple(out_specs),
        scratch_shapes=[pltpu.VMEM((seq_len, dk), F32), pltpu.VMEM((2, dk, 2 * dk), F32),
                        pltpu.VMEM((2, 1, LANES), F32)],
        compiler_params=_cparams(("arbitrary", "arbitrary")),
        name="mlstm_latent" if rope_tabs is not None else "mlstm_context",
    )(*args)


def _rope_tables(seq_len):
    half = M_HEAD_DIM // 2
    nf = half // 2
    t = jnp.arange(seq_len)
    inv = ROPE_BASE ** (-jnp.arange(nf, dtype=F32) / nf)
    ang_r = (t // GRID_W).astype(F32)[:, None] * inv[None, :]
    ang_c = (t % GRID_W).astype(F32)[:, None] * inv[None, :]
    cos = jnp.concatenate([jnp.cos(ang_r)] * 2 + [jnp.cos(ang_c)] * 2, axis=-1)
    sin = jnp.concatenate([-jnp.sin(ang_r), jnp.sin(ang_r), -jnp.sin(ang_c), jnp.sin(ang_c)], axis=-1)
    return cos, sin


CONV_ROWS = 32


def _conv_kernel(a_ref, g_ref, ap_ref, gp_ref, an_ref, gn_ref, wdw_ref, bdw_ref, lng_ref, lnb_ref, wpw_ref,
                 o_ref, u_scr, *, n_ctx_tiles, ctx_tiles_per_seq, lat_tiles_per_seq):
    i = pl.program_id(0)
    tt = a_ref.shape[0]
    halo = CONV_HALO
    pos = jnp.where(i < n_ctx_tiles, i % ctx_tiles_per_seq, (i - n_ctx_tiles) % lat_tiles_per_seq)
    per_seq = jnp.where(i < n_ctx_tiles, ctx_tiles_per_seq, lat_tiles_per_seq)
    keep_prev = (pos > 0).astype(F32)
    keep_next = (pos < per_seq - 1).astype(F32)

    u_scr[0:halo, :] = ap_ref[...] * jax.nn.sigmoid(gp_ref[...]) * keep_prev
    u_scr[halo:halo + tt, :] = a_ref[...] * jax.nn.sigmoid(g_ref[...])
    u_scr[halo + tt:2 * halo + tt, :] = an_ref[...] * jax.nn.sigmoid(gn_ref[...]) * keep_next

    for r in range(tt // CONV_ROWS):
        base = r * CONV_ROWS
        acc = jnp.broadcast_to(bdw_ref[...], (CONV_ROWS, a_ref.shape[1]))
        for tap in range(CONV_TAPS):
            start = base + halo - CONV_TAPS // 2 + tap
            acc = acc + u_scr[start:start + CONV_ROWS, :] * wdw_ref[tap:tap + 1, :]
        mu = jnp.mean(acc, axis=-1, keepdims=True)
        cen = acc - mu
        var = jnp.mean(cen * cen, axis=-1, keepdims=True)
        y = cen * lax.rsqrt(var + EPS) * lng_ref[...] + lnb_ref[...]
        y = y * jax.nn.sigmoid(y)
        o_ref[base:base + CONV_ROWS, :] = _dot(y.astype(BF16), wpw_ref[...]).astype(o_ref.dtype)


def _conv_module(cfg, p, w_dw, b_dw, ln_g, ln_b, w_pw):
    tt, cv, halo = cfg.conv_tt, cfg.cv_width, CONV_HALO
    n_tiles = cfg.n_tok // tt
    cb = cfg.col_ca // cv
    hb = tt // halo
    last_halo = cfg.n_tok // halo - 1
    main = lambda c: pl.BlockSpec((tt, cv), lambda i: (i, c))
    prev = lambda c: pl.BlockSpec((halo, cv), lambda i: (jnp.maximum(i * hb - 1, 0), c))
    nxt = lambda c: pl.BlockSpec((halo, cv), lambda i: (jnp.minimum((i + 1) * hb, last_halo), c))
    vec = lambda: pl.BlockSpec((1, cv), lambda i: (0, 0))
    kern = functools.partial(_conv_kernel, n_ctx_tiles=cfg.n_ctx // tt, ctx_tiles_per_seq=cfg.seq // tt,
                             lat_tiles_per_seq=cfg.dec_seq // tt)
    return pl.pallas_call(
        kern,
        out_shape=jax.ShapeDtypeStruct((cfg.n_tok, cv), BF16),
        grid=(n_tiles,),
        in_specs=[main(cb), main(cb + 1), prev(cb), prev(cb + 1), nxt(cb), nxt(cb + 1),
                  pl.BlockSpec((CONV_TAPS, cv), lambda i: (0, 0)), vec(), vec(), vec(),
                  pl.BlockSpec((cv, cv), lambda i: (0, 0))],
        out_specs=pl.BlockSpec((tt, cv), lambda i: (i, 0)),
        scratch_shapes=[pltpu.VMEM((tt + 2 * halo, cv), F32)],
        compiler_params=_cparams(("arbitrary",)),
        name="conv_module",
    )(p, p, p, p, p, p, w_dw, b_dw.reshape(1, cv), ln_g.reshape(1, cv), ln_b.reshape(1, cv), w_pw)


def _softmax_pv(scores, values):
    mx = functools.reduce(jnp.maximum, [jnp.max(s, axis=1, keepdims=True) for s in scores])
    ps = [jnp.exp(s - mx) for s in scores]
    den = functools.reduce(jnp.add, [jnp.sum(p, axis=1, keepdims=True) for p in ps])
    num = functools.reduce(jnp.add, [_dot(p.astype(BF16), v) for p, v in zip(ps, values)])
    return num / den


def _ctx_attn_kernel(q_ref, k_ref, v_ref, o_ref, *, heads):
    dh = NA_HEAD_DIM
    q = (q_ref[...] * (dh ** -0.5)).astype(BF16)
    k = k_ref[...].astype(BF16)
    v = v_ref[...].astype(BF16)
    outs = []
    for h in range(heads):
        sl = slice(h * dh, (h + 1) * dh)
        outs.append(_softmax_pv([_dot_nt(q[:, sl], k[:, sl])], [v[:, sl]]))
    o_ref[...] = jnp.concatenate(outs, axis=1).astype(o_ref.dtype)


def _ctx_attention(cfg, p):
    s, nw = cfg.seq, cfg.na_width
    cb = cfg.col_qn // nw
    spec = lambda c: pl.BlockSpec((s, nw), lambda b: (b, c))
    return pl.pallas_call(
        functools.partial(_ctx_attn_kernel, heads=cfg.na_heads),
        out_shape=jax.ShapeDtypeStruct((cfg.n_ctx, nw), BF16),
        grid=(cfg.batch,),
        in_specs=[spec(cb), spec(cb + 1), spec(cb + 2)],
        out_specs=pl.BlockSpec((s, nw), lambda b: (b, 0)),
        compiler_params=_cparams(("arbitrary",)),
        name="context_attention",
    )(p, p, p)


NA_QROWS = 8


def _na_kernel(q_ref, kp_ref, kc_ref, kn_ref, vp_ref, vc_ref, vn_ref, kx_ref, vx_ref, bt_ref, o_ref,
               kw_scr, vw_scr, *, heads, grid_rows):
    dh = NA_HEAD_DIM
    blk = NA_QROWS * GRID_W
    win = WIN_ROWS * GRID_W
    m = pl.program_id(1)
    for slot, (kr, vr) in enumerate(((kp_ref, vp_ref), (kc_ref, vc_ref), (kn_ref, vn_ref))):
        kw_scr[slot * blk:(slot + 1) * blk, :] = kr[...].astype(BF16)
        vw_scr[slot * blk:(slot + 1) * blk, :] = vr[...].astype(BF16)
    kx = kx_ref[...].astype(BF16)
    vx = vx_ref[...].astype(BF16)

    def row_body(i, carry):
        r = m * NA_QROWS + i
        rs = jnp.clip(r - WIN_ROWS // 2, 0, grid_rows - WIN_ROWS)
        start = rs - r + (WIN_ROWS - 1)
        off = pl.multiple_of((rs - NA_QROWS * (m - 1)) * GRID_W, GRID_W)
        qrows = pl.ds(pl.multiple_of(i * GRID_W, GRID_W), GRID_W)
        q = (q_ref[qrows, :] * (dh ** -0.5)).astype(BF16)
        kwin = kw_scr[pl.ds(off, win), :]
        vwin = vw_scr[pl.ds(off, win), :]
        outs = []
        for h in range(heads):
            sl = slice(h * dh, (h + 1) * dh)
            s_loc = _dot_nt(q[:, sl], kwin[:, sl]) + bt_ref[start, h]
            s_ctx = _dot_nt(q[:, sl], kx[:, sl])
            outs.append(_softmax_pv([s_loc, s_ctx], [vwin[:, sl], vx[:, sl]]))
        o_ref[qrows, :] = jnp.concatenate(outs, axis=1).astype(o_ref.dtype)
        return carry

    lax.fori_loop(0, NA_QROWS, row_body, 0)


def _na_bias_table(rpb, grid_rows):
    del grid_rows
    qc = np.arange(GRID_W)[:, None]
    kc = np.arange(GRID_W)[None, :]
    cs = np.clip(qc - WIN_COLS // 2, 0, GRID_W - WIN_COLS)
    valid = (kc >= cs) & (kc < cs + WIN_COLS)
    dc = np.clip(kc - qc + WIN_COLS - 1, 0, 2 * WIN_COLS - 2)
    dr = np.arange(WIN_ROWS)[:, None] + np.arange(WIN_ROWS)[None, :]
    tab = rpb[:, dr][:, :, :, dc]
    tab = jnp.where(jnp.asarray(valid)[None, None, None], tab, NEG_INF)
    tab = tab.transpose(1, 0, 3, 2, 4)
    return tab.reshape(WIN_ROWS, rpb.shape[0], GRID_W, WIN_ROWS * GRID_W)


def _na_latent(cfg, p, k_ctx, v_ctx, bias_tab):
    nw, heads = cfg.na_width, cfg.na_heads
    blk = NA_QROWS * GRID_W
    grid_rows = cfg.dec_seq // GRID_W
    nblk = grid_rows // NA_QROWS
    cb = cfg.col_qn // nw
    rb0 = cfg.n_ctx // blk
    assert rb0 * blk == cfg.n_ctx and nblk * NA_QROWS == grid_rows

    def rows(shift, c):
        return pl.BlockSpec((blk, nw), lambda b, m: (rb0 + b * nblk + jnp.clip(m + shift, 0, nblk - 1), c))

    ctx_spec = pl.BlockSpec((None, cfg.past_len, nw), lambda b, m: (b, 0, 0))
    return pl.pallas_call(
        functools.partial(_na_kernel, heads=heads, grid_rows=grid_rows),
        out_shape=jax.ShapeDtypeStruct((cfg.n_lat, nw), BF16),
        grid=(cfg.dec_batch, nblk),
        in_specs=[rows(0, cb), rows(-1, cb + 1), rows(0, cb + 1), rows(1, cb + 1),
                  rows(-1, cb + 2), rows(0, cb + 2), rows(1, cb + 2), ctx_spec, ctx_spec,
                  pl.BlockSpec((WIN_ROWS, heads, GRID_W, WIN_ROWS * GRID_W), lambda b, m: (0, 0, 0, 0))],
        out_specs=pl.BlockSpec((blk, nw), lambda b, m: (b * nblk + m, 0)),
        scratch_shapes=[pltpu.VMEM((3 * blk, nw), BF16), pltpu.VMEM((3 * blk, nw), BF16)],
        compiler_params=_cparams(("arbitrary", "arbitrary")),
        name="neighbourhood_attention",
    )(p, p, p, p, p, p, p, k_ctx, v_ctx, bias_tab)


def _outproj_kernel(ym_ref, yc_ref, yn_ref, x_ref, w_ref, ga_ref, gpost_ref, gpre_ref, sc_ref, sh_ref,
                    xo_ref, ho_ref, y_scr, *, m_width, cv_width):
    a, b = m_width, m_width + cv_width
    y_scr[...] = (_dot(ym_ref[...], w_ref[0:a, :]) + _dot(yc_ref[...], w_ref[a:b, :])
                  + _dot(yn_ref[...], w_ref[b:, :]))

    def body(r, carry):
        rows = pl.ds(pl.multiple_of(r * ROW_CHUNK, ROW_CHUNK), ROW_CHUNK)
        x = x_ref[rows, :] + ga_ref[...] * (_rms(y_scr[rows, :]) * gpost_ref[...])
        xo_ref[rows, :] = x
        ho_ref[rows, :] = (_rms(x) * gpre_ref[...] * (1.0 + sc_ref[...]) + sh_ref[...]).astype(BF16)
        return carry

    lax.fori_loop(0, x_ref.shape[0] // ROW_CHUNK, body, 0)


def _outproj(cfg, ym, yc, yn, x, w_out, modl, g_post, g_pre):
    d, tm = cfg.d_model, cfg.tm
    rowblk = lambda w: pl.BlockSpec((tm, w), lambda i: (i, 0))
    vec = lambda: pl.BlockSpec((1, d), lambda i: (0, 0))
    return pl.pallas_call(
        functools.partial(_outproj_kernel, m_width=cfg.m_width, cv_width=cfg.cv_width),
        out_shape=(jax.ShapeDtypeStruct((cfg.n_tok, d), F32), jax.ShapeDtypeStruct((cfg.n_tok, d), BF16)),
        grid=(cfg.n_tok // tm,),
        in_specs=[rowblk(cfg.m_width), rowblk(cfg.cv_width), rowblk(cfg.na_width), rowblk(d),
                  pl.BlockSpec((d, d), lambda i: (0, 0)),
                  _mod_spec(cfg, 2), vec(), vec(), _mod_spec(cfg, 4), _mod_spec(cfg, 3)],
        out_specs=(rowblk(d), rowblk(d)),
        scratch_shapes=[pltpu.VMEM((tm, d), F32)],
        compiler_params=_cparams(("arbitrary",)),
        name="out_projection",
    )(ym, yc, yn, x, w_out, modl, g_post.reshape(1, d), g_pre.reshape(1, d), modl, modl)


def _ffn_kernel(h_ref, x_ref, w1_ref, w2_ref, ga_ref, g_ref, o_ref, acc_ref):
    kk = pl.program_id(1)
    a = jnp.maximum(_dot(h_ref[...], w1_ref[...]), 0.0)
    part = _dot((a * a).astype(BF16), w2_ref[...])

    @pl.when(kk == 0)
    def _():
        acc_ref[...] = part

    @pl.when(kk > 0)
    def _():
        acc_ref[...] += part

    @pl.when(kk == pl.num_programs(1) - 1)
    def _():
        def body(r, carry):
            rows = pl.ds(pl.multiple_of(r * ROW_CHUNK, ROW_CHUNK), ROW_CHUNK)
            o_ref[rows, :] = x_ref[rows, :] + ga_ref[...] * (_rms(acc_ref[rows, :]) * g_ref[...])
            return carry

        lax.fori_loop(0, x_ref.shape[0] // ROW_CHUNK, body, 0)


def _ffn(cfg, h, x, w1, w2, modl, g_post):
    d, tm, th = cfg.d_model, cfg.tm, cfg.th_ff
    return pl.pallas_call(
        _ffn_kernel,
        out_shape=jax.ShapeDtypeStruct((cfg.n_tok, d), F32),
        grid=(cfg.n_tok // tm, cfg.d_ff // th),
        in_specs=[pl.BlockSpec((tm, d), lambda i, k: (i, 0)),
                  pl.BlockSpec((tm, d), lambda i, k: (i, 0)),
                  pl.BlockSpec((d, th), lambda i, k: (0, k)),
                  pl.BlockSpec((th, d), lambda i, k: (k, 0)),
                  _mod_spec(cfg, 5),
                  pl.BlockSpec((1, d), lambda i, k: (0, 0))],
        out_specs=pl.BlockSpec((tm, d), lambda i, k: (i, 0)),
        scratch_shapes=[pltpu.VMEM((tm, d), F32)],
        compiler_params=_cparams(("arbitrary", "arbitrary")),
        name="relu2_mlp",
    )(h, x, w1, w2, modl, g_post.reshape(1, d))


def _prep_w_in(cfg, w_in_l):
    mw, cv, nw, ng = cfg.m_width, cfg.cv_width, cfg.na_width, cfg.n_gates
    g0 = 4 * mw
    main = jnp.concatenate([w_in_l[:, :g0], w_in_l[:, g0 + ng:]], axis=1)
    gates = w_in_l[:, g0:g0 + ng]
    pad = cfg.in_cols - (main.shape[1] + ng)
    w_p = jnp.concatenate([main, gates, jnp.zeros((cfg.d_model, pad), w_in_l.dtype)], axis=1)
    return w_p.astype(BF16), gates.T.astype(BF16)


def _forward(cfg, x_prompt, x_sample, c, cache_k, cache_v, state_C, state_n, state_m, c_ctx, w_ada, b_ada,
             g_pre_mix, g_post_mix, g_pre_mlp, g_post_mlp, w_in, b_gate, g_mlstm, w_dw, b_dw, ln_g, ln_b,
             w_pw, rpb, w_out, w_ff1, w_ff2):
    d = cfg.d_model
    x = jnp.concatenate([x_prompt.reshape(cfg.n_ctx, d), x_sample.reshape(cfg.n_lat, d)], axis=0)

    n_cond = 1 + cfg.dec_batch
    cond_rows = -(-n_cond // 8) * 8
    cond = jnp.concatenate([c_ctx[None, :], c, jnp.zeros((cond_rows - n_cond, d), F32)], axis=0)
    mod = _modulation(cond, w_ada, b_ada, tn=min(1024, 6 * d)).reshape(cfg.depth, cond_rows, 6, 1, d)

    rope_tabs = _rope_tables(cfg.dec_seq)
    nw = cfg.na_width
    ks, vs, cs, ns, ms = [], [], [], [], []
    for l in range(cfg.depth):
        w_in_p, w_gt = _prep_w_in(cfg, w_in[l])
        p, gt = _inproj(cfg, x, mod[l], g_pre_mix[l], w_in_p, w_gt)

        bias_col = b_gate[l].reshape(cfg.n_gates, 1)
        ym_c, c_new, n_new, m_new = _mlstm(cfg, p, gt, bias_col, g_mlstm[l], n_seq=cfg.batch, seq_len=cfg.seq,
                                           row0=0, emit_state=True)
        (ym_l,) = _mlstm(cfg, p, gt, bias_col, g_mlstm[l], n_seq=cfg.dec_batch, seq_len=cfg.dec_seq,
                         row0=cfg.n_ctx, rope_tabs=rope_tabs,
                         state0=(state_C[:, l], state_n[:, l], state_m[:, l]))
        yc = _conv_module(cfg, p, w_dw[l], b_dw[l], ln_g[l], ln_b[l], w_pw[l].astype(BF16))
        yn_c = _ctx_attention(cfg, p)
        yn_l = _na_latent(cfg, p, cache_k[:, l].reshape(cfg.dec_batch, cfg.past_len, nw),
                          cache_v[:, l].reshape(cfg.dec_batch, cfg.past_len, nw),
                          _na_bias_table(rpb[l], cfg.dec_seq // GRID_W))
        ym = jnp.concatenate([ym_c, ym_l], axis=0)
        yn = jnp.concatenate([yn_c, yn_l], axis=0)

        x, h2 = _outproj(cfg, ym, yc, yn, x, w_out[l].astype(BF16), mod[l], g_post_mix[l], g_pre_mlp[l])
        x = _ffn(cfg, h2, x, w_ff1[l].astype(BF16), w_ff2[l].astype(BF16), mod[l], g_post_mlp[l])

        kcol = cfg.col_qn + nw
        ks.append(p[:cfg.n_ctx, kcol:kcol + nw].reshape(cfg.batch, cfg.seq, cfg.na_heads, NA_HEAD_DIM))
        vs.append(p[:cfg.n_ctx, kcol + nw:kcol + 2 * nw].reshape(cfg.batch, cfg.seq, cfg.na_heads, NA_HEAD_DIM))
        cs.append(c_new)
        ns.append(n_new[:, :, :, 0, :])
        ms.append(m_new[:, :, :, 0, 0])

    y_prompt = x[:cfg.n_ctx].reshape(cfg.batch, cfg.seq, d)
    y_sample = x[cfg.n_ctx:].reshape(cfg.dec_batch, cfg.dec_seq, d)
    return (y_prompt, y_sample, jnp.stack(ks, axis=1), jnp.stack(vs, axis=1),
            jnp.stack(cs, axis=1), jnp.stack(ns, axis=1), jnp.stack(ms, axis=1))


def _make_cfg(x_prompt, x_sample, cache_k, w_in, w_ff1, w_pw, **tiles):
    batch, seq, d = x_prompt.shape
    dec_batch, dec_seq, _ = x_sample.shape
    return Cfg(d_model=d, batch=batch, seq=seq, depth=w_in.shape[0], dec_batch=dec_batch, dec_seq=dec_seq,
               past_len=cache_k.shape[2], m_heads=d // (2 * M_HEAD_DIM), cv_width=w_pw.shape[-1],
               na_heads=cache_k.shape[3], d_ff=w_ff1.shape[-1], **tiles)


def kernel(x_prompt, x_sample, c, cache_k, cache_v, state_C, state_n, state_m, c_ctx, w_ada, b_ada, g_pre_mix,
           g_post_mix, g_pre_mlp, g_post_mlp, w_in, b_gate, g_mlstm, w_dw, b_dw, ln_g, ln_b, w_pw, rpb, w_out,
           w_ff1, w_ff2):
    cfg = _make_cfg(x_prompt, x_sample, cache_k, w_in, w_ff1, w_pw,
                    tm=512, tn_in=768, th_ff=1024, chunk=128, conv_tt=256)
    return _forward(cfg, x_prompt, x_sample, c, cache_k, cache_v, state_C, state_n, state_m, c_ctx, w_ada, b_ada,
                    g_pre_mix, g_post_mix, g_pre_mlp, g_post_mlp, w_in, b_gate, g_mlstm, w_dw, b_dw, ln_g, ln_b,
                    w_pw, rpb, w_out, w_ff1, w_ff2)
```

```python
import functools
from typing import NamedTuple

import numpy as np
import jax
import jax.numpy as jnp
from jax import lax
from jax.experimental import pallas as pl
from jax.experimental.pallas import tpu as pltpu

F32 = jnp.float32
BF16 = jnp.bfloat16

EPS = 1e-6
NEG_INF = -1e30
ROPE_BASE = 10000.0

LANES = 128
SUBLANES = 8
M_HEAD_DIM = 128
NA_HEAD_DIM = 64
GRID_W = 64
WIN_ROWS = 8
WIN_COLS = 16
CONV_TAPS = 31
CONV_HALO = 16
VMEM_LIMIT = 52 * 1024 * 1024


class Cfg(NamedTuple):
    d_model: int
    batch: int
    seq: int
    depth: int
    dec_batch: int
    dec_seq: int
    past_len: int
    m_heads: int
    cv_width: int
    na_heads: int
    d_ff: int
    tm_in: int
    tm_out: int
    tm_ff: int
    tn_in: int
    th_ff: int
    ctx_sub: int
    lat_unroll: int
    conv_tt: int

    @property
    def m_width(self):
        return self.m_heads * M_HEAD_DIM

    @property
    def na_width(self):
        return self.na_heads * NA_HEAD_DIM

    @property
    def n_gates(self):
        return 4 * self.m_heads

    @property
    def n_ctx(self):
        return self.batch * self.seq

    @property
    def n_lat(self):
        return self.dec_batch * self.dec_seq

    @property
    def n_tok(self):
        return self.n_ctx + self.n_lat

    @property
    def col_ca(self):
        return 4 * self.m_width

    @property
    def col_qn(self):
        return self.col_ca + 2 * self.cv_width

    @property
    def col_gates(self):
        return self.col_qn + 3 * self.na_width

    @property
    def in_cols(self):
        raw = self.col_gates + LANES
        return -(-raw // self.tn_in) * self.tn_in


def _cparams(sem):
    return pltpu.CompilerParams(dimension_semantics=sem, vmem_limit_bytes=VMEM_LIMIT)


def _rms(x):
    return x * lax.rsqrt(jnp.mean(x * x, axis=-1, keepdims=True) + EPS)


def _dot(a, b):
    return jnp.dot(a, b, preferred_element_type=F32)


def _dot_nt(a, b):
    return lax.dot_general(a, b, (((1,), (1,)), ((), ())), preferred_element_type=F32)


def _dot_tn(a, b):
    return lax.dot_general(a, b, (((0,), (0,)), ((), ())), preferred_element_type=F32)


def _mod_kernel(c_ref, w_ref, b_ref, o_ref):
    c = c_ref[...]
    s = (c * jax.nn.sigmoid(c)).astype(BF16)
    o_ref[...] = _dot(s, w_ref[...].astype(BF16)) + b_ref[...]


def _modulation(cond, w_ada, b_ada, tn):
    depth, d, n6 = w_ada.shape
    rows = cond.shape[0]
    return pl.pallas_call(
        _mod_kernel,
        out_shape=jax.ShapeDtypeStruct((depth, rows, n6), F32),
        grid=(depth, n6 // tn),
        in_specs=[
            pl.BlockSpec((rows, d), lambda l, j: (0, 0)),
            pl.BlockSpec((None, d, tn), lambda l, j: (l, 0, j)),
            pl.BlockSpec((None, 1, tn), lambda l, j: (l, 0, j)),
        ],
        out_specs=pl.BlockSpec((None, rows, tn), lambda l, j: (l, 0, j)),
        compiler_params=_cparams(("arbitrary", "arbitrary")),
        name="adaln_modulation",
    )(cond, w_ada, b_ada.reshape(depth, 1, n6))


def _mod_row_map(cfg, which, tm):
    assert cfg.n_ctx % tm == 0 and cfg.dec_seq % tm == 0
    nct = cfg.n_ctx // tm
    tpb = cfg.dec_seq // tm

    def index_map(i, *_):
        row = jnp.where(i < nct, 0, 1 + (i - nct) // tpb)
        return (row, which, 0, 0)

    return index_map


def _mod_spec(cfg, which, tm):
    return pl.BlockSpec((None, None, 1, cfg.d_model), _mod_row_map(cfg, which, tm))


ROW_CHUNK = 128


def _rope(x, cos, sin_signed, even_block):
    swapped = jnp.where(even_block, pltpu.roll(x, 96, 1), pltpu.roll(x, 32, 1))
    return x * cos + swapped * sin_signed


def _inproj_kernel(x_ref, sh_ref, sc_ref, g_ref, w_ref, wgt_ref, cos_ref, sin_ref, o_ref, gt_ref, h_scr,
                   *, q_tiles, k_tiles):
    j = pl.program_id(1)
    tm, tn = o_ref.shape

    @pl.when(j == 0)
    def _():
        def body(r, carry):
            rows = pl.ds(pl.multiple_of(r * ROW_CHUNK, ROW_CHUNK), ROW_CHUNK)
            h = _rms(x_ref[rows, :]) * g_ref[...] * (1.0 + sc_ref[...]) + sh_ref[...]
            h_scr[rows, :] = h.astype(BF16)
            return carry

        lax.fori_loop(0, tm // ROW_CHUNK, body, 0)
        gt_ref[...] = _dot_nt(wgt_ref[...], h_scr[...])

    @pl.when(j >= q_tiles + k_tiles)
    def _():
        o_ref[...] = _dot(h_scr[...], w_ref[...])

    @pl.when(j < q_tiles + k_tiles)
    def _():
        o_ref[...] = _dot(h_scr[...], w_ref[...])
        scale = jnp.where(j >= q_tiles, M_HEAD_DIM ** -0.5, 1.0).astype(F32)
        lane_blk = lax.broadcasted_iota(jnp.int32, (ROW_CHUNK, M_HEAD_DIM), 1) // 32
        even_block = (lane_blk % 2) == 0
        for r in range(tm // ROW_CHUNK):
            rows = slice(r * ROW_CHUNK, (r + 1) * ROW_CHUNK)
            cos, sin = cos_ref[rows, :], sin_ref[rows, :]
            for hh in range(tn // M_HEAD_DIM):
                cols = slice(hh * M_HEAD_DIM, (hh + 1) * M_HEAD_DIM)
                o_ref[rows, cols] = _rope(o_ref[rows, cols], cos, sin, even_block) * scale


def _inproj(cfg, x, modl, g, w_in_p, w_gt, rope_tabs):
    d, tm, tn = cfg.d_model, cfg.tm_in, cfg.tn_in
    ng = cfg.n_gates
    assert cfg.m_width % tn == 0
    kern = functools.partial(_inproj_kernel, q_tiles=cfg.m_width // tn, k_tiles=cfg.m_width // tn)
    return pl.pallas_call(
        kern,
        out_shape=(jax.ShapeDtypeStruct((cfg.n_tok, cfg.in_cols), F32),
                   jax.ShapeDtypeStruct((ng, cfg.n_tok), F32)),
        grid=(cfg.n_tok // tm, cfg.in_cols // tn),
        in_specs=[
            pl.BlockSpec((tm, d), lambda i, j: (i, 0)),
            _mod_spec(cfg, 0, tm),
            _mod_spec(cfg, 1, tm),
            pl.BlockSpec((1, d), lambda i, j: (0, 0)),
            pl.BlockSpec((d, tn), lambda i, j: (0, j)),
            pl.BlockSpec((ng, d), lambda i, j: (0, 0)),
            pl.BlockSpec((tm, M_HEAD_DIM), lambda i, j: (i, 0)),
            pl.BlockSpec((tm, M_HEAD_DIM), lambda i, j: (i, 0)),
        ],
        out_specs=(pl.BlockSpec((tm, tn), lambda i, j: (i, j)),
                   pl.BlockSpec((ng, tm), lambda i, j: (0, i))),
        scratch_shapes=[pltpu.VMEM((tm, d), BF16)],
        compiler_params=_cparams(("arbitrary", "arbitrary")),
        name="in_projection",
    )(x, modl, modl, g.reshape(1, d), w_in_p, w_gt, *rope_tabs)


ROW_A, ROW_M, ROW_E, ROW_WK, ROW_MPREV, ROW_DECAY = range(6)
FINISH_ROWS = 512


def _log_sigmoid(x):
    return jnp.minimum(x, 0.0) - jnp.log1p(jnp.exp(-jnp.abs(x)))


def _cumsum_lanes(x, cum_mat):
    hi = x.astype(BF16)
    r1 = x - hi.astype(F32)
    mid = r1.astype(BF16)
    lo = (r1 - mid.astype(F32)).astype(BF16)
    return _dot(hi, cum_mat) + _dot(mid, cum_mat) + _dot(lo, cum_mat)


def _cummax_lanes(x, reverse):
    n = x.shape[1]
    lane = lax.broadcasted_iota(jnp.int32, x.shape, 1)
    k = 1
    while k < n:
        if reverse:
            shifted, ok = pltpu.roll(x, n - k, 1), lane < n - k
        else:
            shifted, ok = pltpu.roll(x, k, 1), lane >= k
        x = jnp.maximum(x, jnp.where(ok, shifted, -jnp.inf))
        k *= 2
    return x


def _col_bcast(row, n):
    return jnp.transpose(jnp.broadcast_to(row, (n, row.shape[1])))


def _mlstm_kernel(*refs, seq_len, n_sub, heads, unroll, init_state, emit_state):
    it = iter(refs)
    q_ref, k_ref, v_ref, om_ref, gt_ref, bias_ref, gm_ref = (next(it) for _ in range(7))
    c0_ref = n0_ref = m0_ref = None
    if init_state:
        c0_ref, n0_ref, m0_ref = next(it), next(it), next(it)
    o_ref = next(it)
    co_ref = no_ref = mo_ref = None
    if emit_state:
        co_ref, no_ref, mo_ref = next(it), next(it), next(it)
    hs_f, hs_b, cext, rows_scr = next(it), next(it), next(it), next(it)

    L = dk = M_HEAD_DIM
    head = pl.program_id(1)
    nc = seq_len // L

    tt = lax.broadcasted_iota(jnp.int32, (L, L), 0)
    ss = lax.broadcasted_iota(jnp.int32, (L, L), 1)
    cum_mats = ((tt <= ss).astype(BF16), (tt >= ss).astype(BF16))
    causal = (ss <= tt, ss >= tt)

    for d in range(2):
        row_i = d * 2 * heads + head
        xi = gt_ref[row_i] + bias_ref[row_i]
        lf = _log_sigmoid(gt_ref[row_i + heads] + bias_ref[row_i + heads])
        b = _cumsum_lanes(lf, cum_mats[d])
        tot = jnp.sum(lf, axis=1, keepdims=True)
        a = xi - b
        g = tot - b + xi
        gmax = jnp.max(g, axis=1, keepdims=True)
        arun = _cummax_lanes(a, reverse=(d == 1))
        for s in range(n_sub):
            m = m0_ref[s, d][:, 0:1] if init_state else jnp.zeros((1, 1), F32)
            for c in (range(nc) if d == 0 else reversed(range(nc))):
                r = s * nc + c
                m_new = jnp.maximum(tot[r:r + 1] + m, gmax[r:r + 1])
                m_row = jnp.maximum(arun[r:r + 1], m)
                rows_scr[d, r, ROW_A:ROW_A + 1, :] = a[r:r + 1]
                rows_scr[d, r, ROW_M:ROW_M + 1, :] = m_row
                rows_scr[d, r, ROW_E:ROW_E + 1, :] = -(b[r:r + 1] + m_row)
                rows_scr[d, r, ROW_WK:ROW_WK + 1, :] = jnp.exp(g[r:r + 1] - m_new)
                rows_scr[d, r, ROW_MPREV:ROW_MPREV + 1, :] = jnp.broadcast_to(m, (1, L))
                rows_scr[d, r, ROW_DECAY:ROW_DECAY + 1, :] = jnp.broadcast_to(jnp.exp(tot[r:r + 1] + m - m_new), (1, L))
                m = m_new
            if emit_state:
                mo_ref[s, d] = jnp.broadcast_to(m, (1, LANES))
            if init_state:
                cext[s, d, :, 0:dk] = c0_ref[s, d]
                cext[s, d, :, dk:2 * dk] = _col_bcast(n0_ref[s, d], dk)
            else:
                cext[s, d] = jnp.zeros((dk, 2 * dk), F32)

    ones_ext = jnp.ones((L, dk), BF16)

    def prepare(s, d, c):
        rows = pl.ds(pl.multiple_of(s * seq_len + c * L, L), L)
        tile = rows_scr[d, s * nc + c]
        q = q_ref[rows, :].astype(BF16)
        k = k_ref[rows, :]
        v_ext = jnp.concatenate([v_ref[rows, :].astype(BF16), ones_ext], axis=1)
        m_b = _col_bcast(tile[ROW_M:ROW_M + 1], dk)
        floor = jnp.exp(_col_bcast(tile[ROW_E:ROW_E + 1], dk))
        dmat = jnp.exp(jnp.where(causal[d], tile[ROW_A:ROW_A + 1] - m_b, -jnp.inf))
        sc = (_dot_nt(q, k.astype(BF16)) * dmat).astype(BF16)
        w_int = jnp.exp(tile[ROW_MPREV:ROW_MPREV + 1] - m_b)
        kt_w = (jnp.transpose(k) * tile[ROW_WK:ROW_WK + 1]).astype(BF16)
        return rows, q, v_ext, sc, w_int, floor, kt_w, tile[ROW_DECAY:ROW_DECAY + 1, 0:1]

    def apply(s, d, prepared):
        rows, q, v_ext, sc, w_int, floor, kt_w, decay = prepared
        c_prev = cext[s, d]
        nd = _dot(sc, v_ext) + jnp.concatenate([w_int, w_int], axis=1) * _dot(q, c_prev.astype(BF16))
        den = jnp.maximum(jnp.abs(nd[:, dk:2 * dk]), floor)
        (hs_f if d == 0 else hs_b)[rows, :] = nd[:, 0:dk] / den
        cext[s, d] = decay * c_prev + _dot(kt_w, v_ext)

    def chunk_body(i, carry):
        todo = []
        for u in range(unroll):
            c = i * unroll + u
            for s in range(n_sub):
                todo.append((s, 0, prepare(s, 0, c)))
                todo.append((s, 1, prepare(s, 1, nc - 1 - c)))
        for s, d, prepared in todo:
            apply(s, d, prepared)
        return carry

    lax.fori_loop(0, nc // unroll, chunk_body, 0)

    def finish(r, carry):
        rows = pl.ds(pl.multiple_of(r * FINISH_ROWS, FINISH_ROWS), FINISH_ROWS)
        y = _rms(hs_f[rows, :] + hs_b[rows, :]) * gm_ref[...] * jax.nn.sigmoid(om_ref[rows, :])
        o_ref[rows, :] = y.astype(o_ref.dtype)
        return carry

    lax.fori_loop(0, n_sub * seq_len // FINISH_ROWS, finish, 0)

    if emit_state:
        for s in range(n_sub):
            for d in range(2):
                co_ref[s, d] = cext[s, d, :, 0:dk]
                no_ref[s, d] = jnp.transpose(cext[s, d, :, dk:2 * dk])[0:1, :]


def _mlstm(cfg, p, gt, bias_rows, g_mlstm, *, n_seq, seq_len, row0, n_sub, unroll, name, state0=None,
           emit_state=False):
    heads, dk = cfg.m_heads, M_HEAD_DIM
    L = dk
    nc = seq_len // L
    blk_rows = n_sub * seq_len
    sb = row0 // blk_rows
    assert sb * blk_rows == row0 and nc * L == seq_len and n_seq % n_sub == 0 and nc % unroll == 0
    assert (n_sub * nc) % 8 == 0
    n_steps = n_seq // n_sub

    def col(block0):
        return pl.BlockSpec((blk_rows, dk), lambda b, h: (sb + b, block0 + h))

    gt4 = gt[:, row0:row0 + n_seq * seq_len].reshape(cfg.n_gates, n_steps, n_sub * nc, L)
    in_specs = [col(0), col(heads), col(2 * heads), col(3 * heads),
                pl.BlockSpec((cfg.n_gates, None, n_sub * nc, L), lambda b, h: (0, b, 0, 0)),
                pl.BlockSpec((cfg.n_gates, 1, L), lambda b, h: (0, 0, 0)),
                pl.BlockSpec((None, 1, dk), lambda b, h: (h, 0, 0))]
    args = [p, p, p, p, gt4, bias_rows, g_mlstm.reshape(heads, 1, dk)]
    state_spec = lambda r, c: pl.BlockSpec((n_sub, 2, None, r, c), lambda b, h: (b, 0, h, 0, 0))
    if state0 is not None:
        c0, n0, m0 = state0
        in_specs += [state_spec(dk, dk), state_spec(1, dk), state_spec(1, LANES)]
        args += [c0, n0.reshape(n_seq, 2, heads, 1, dk),
                 jnp.broadcast_to(m0[..., None, None], (n_seq, 2, heads, 1, LANES))]
    out_shape = [jax.ShapeDtypeStruct((n_seq * seq_len, cfg.m_width), BF16)]
    out_specs = [pl.BlockSpec((blk_rows, dk), lambda b, h: (b, h))]
    if emit_state:
        out_shape += [jax.ShapeDtypeStruct((n_seq, 2, heads, dk, dk), F32),
                      jax.ShapeDtypeStruct((n_seq, 2, heads, 1, dk), F32),
                      jax.ShapeDtypeStruct((n_seq, 2, heads, 1, LANES), F32)]
        out_specs += [state_spec(dk, dk), state_spec(1, dk), state_spec(1, LANES)]
    kern = functools.partial(_mlstm_kernel, seq_len=seq_len, n_sub=n_sub, heads=heads, unroll=unroll,
                             init_state=state0 is not None, emit_state=emit_state)
    return pl.pallas_call(
        kern,
        out_shape=tuple(out_shape),
        grid=(n_steps, heads),
        in_specs=in_specs,
        out_specs=tuple(out_specs),
        scratch_shapes=[pltpu.VMEM((blk_rows, dk), F32), pltpu.VMEM((blk_rows, dk), F32),
                        pltpu.VMEM((n_sub, 2, dk, 2 * dk), F32), pltpu.VMEM((2, n_sub * nc, 8, L), F32)],
        compiler_params=_cparams(("arbitrary", "arbitrary")),
        name=name,
    )(*args)


def _rope_tables(cfg):
    half = M_HEAD_DIM // 2
    nf = half // 2
    t = jnp.arange(cfg.dec_seq)
    inv = ROPE_BASE ** (-jnp.arange(nf, dtype=F32) / nf)
    ang_r = (t // GRID_W).astype(F32)[:, None] * inv[None, :]
    ang_c = (t % GRID_W).astype(F32)[:, None] * inv[None, :]
    cos = jnp.concatenate([jnp.cos(ang_r)] * 2 + [jnp.cos(ang_c)] * 2, axis=-1)
    sin = jnp.concatenate([-jnp.sin(ang_r), jnp.sin(ang_r), -jnp.sin(ang_c), jnp.sin(ang_c)], axis=-1)
    cos = jnp.concatenate([jnp.ones((cfg.n_ctx, M_HEAD_DIM), F32), jnp.tile(cos, (cfg.dec_batch, 1))], axis=0)
    sin = jnp.concatenate([jnp.zeros((cfg.n_ctx, M_HEAD_DIM), F32), jnp.tile(sin, (cfg.dec_batch, 1))], axis=0)
    return cos, sin


CONV_ROWS = 32


def _conv_kernel(a_ref, g_ref, ap_ref, gp_ref, an_ref, gn_ref, wdw_ref, bdw_ref, lng_ref, lnb_ref, wpw_ref,
                 o_ref, u_scr, *, n_ctx_tiles, ctx_tiles_per_seq, lat_tiles_per_seq):
    i = pl.program_id(0)
    tt = a_ref.shape[0]
    halo = CONV_HALO
    pos = jnp.where(i < n_ctx_tiles, i % ctx_tiles_per_seq, (i - n_ctx_tiles) % lat_tiles_per_seq)
    per_seq = jnp.where(i < n_ctx_tiles, ctx_tiles_per_seq, lat_tiles_per_seq)
    keep_prev = (pos > 0).astype(F32)
    keep_next = (pos < per_seq - 1).astype(F32)

    u_scr[0, 0:halo, :] = ap_ref[...] * jax.nn.sigmoid(gp_ref[...]) * keep_prev
    u_scr[0, halo:halo + tt, :] = a_ref[...] * jax.nn.sigmoid(g_ref[...])
    u_scr[0, halo + tt:2 * halo + tt, :] = an_ref[...] * jax.nn.sigmoid(gn_ref[...]) * keep_next
    n_rows = tt + 2 * halo
    window = u_scr[0]
    for r in range(1, SUBLANES):
        u_scr[r] = pltpu.roll(window, n_rows - r, 0)

    for blk in range(tt // CONV_ROWS):
        base = blk * CONV_ROWS
        acc = jnp.broadcast_to(bdw_ref[...], (CONV_ROWS, a_ref.shape[1]))
        for tap in range(CONV_TAPS):
            start = base + halo - CONV_TAPS // 2 + tap
            r, aligned = start % SUBLANES, start - start % SUBLANES
            w_tap = pltpu.repeat(wdw_ref[tap], CONV_ROWS // SUBLANES, 0)
            acc = acc + u_scr[r, aligned:aligned + CONV_ROWS, :] * w_tap
        mu = jnp.mean(acc, axis=-1, keepdims=True)
        cen = acc - mu
        var = jnp.mean(cen * cen, axis=-1, keepdims=True)
        y = cen * lax.rsqrt(var + EPS) * lng_ref[...] + lnb_ref[...]
        y = y * jax.nn.sigmoid(y)
        o_ref[base:base + CONV_ROWS, :] = _dot(y.astype(BF16), wpw_ref[...]).astype(o_ref.dtype)


def _conv_module(cfg, p, w_dw, b_dw, ln_g, ln_b, w_pw):
    tt, cv, halo = cfg.conv_tt, cfg.cv_width, CONV_HALO
    n_tiles = cfg.n_tok // tt
    cb = cfg.col_ca // cv
    hb = tt // halo
    last_halo = cfg.n_tok // halo - 1
    main = lambda c: pl.BlockSpec((tt, cv), lambda i: (i, c))
    prev = lambda c: pl.BlockSpec((halo, cv), lambda i: (jnp.maximum(i * hb - 1, 0), c))
    nxt = lambda c: pl.BlockSpec((halo, cv), lambda i: (jnp.minimum((i + 1) * hb, last_halo), c))
    vec = lambda: pl.BlockSpec((1, cv), lambda i: (0, 0))
    kern = functools.partial(_conv_kernel, n_ctx_tiles=cfg.n_ctx // tt, ctx_tiles_per_seq=cfg.seq // tt,
                             lat_tiles_per_seq=cfg.dec_seq // tt)
    return pl.pallas_call(
        kern,
        out_shape=jax.ShapeDtypeStruct((cfg.n_tok, cv), BF16),
        grid=(n_tiles,),
        in_specs=[main(cb), main(cb + 1), prev(cb), prev(cb + 1), nxt(cb), nxt(cb + 1),
                  pl.BlockSpec((CONV_TAPS, SUBLANES, cv), lambda i: (0, 0, 0)), vec(), vec(), vec(),
                  pl.BlockSpec((cv, cv), lambda i: (0, 0))],
        out_specs=pl.BlockSpec((tt, cv), lambda i: (i, 0)),
        scratch_shapes=[pltpu.VMEM((SUBLANES, tt + 2 * halo, cv), F32)],
        compiler_params=_cparams(("arbitrary",)),
        name="conv_module",
    )(p, p, p, p, p, p, jnp.broadcast_to(w_dw[:, None, :], (CONV_TAPS, SUBLANES, cv)),
      b_dw.reshape(1, cv), ln_g.reshape(1, cv), ln_b.reshape(1, cv), w_pw)


def _softmax_pv(scores, values):
    mx = functools.reduce(jnp.maximum, [jnp.max(s, axis=1, keepdims=True) for s in scores])
    ps = [jnp.exp(s - mx) for s in scores]
    den = functools.reduce(jnp.add, [jnp.sum(p, axis=1, keepdims=True) for p in ps])
    num = functools.reduce(jnp.add, [_dot(p.astype(BF16), v) for p, v in zip(ps, values)])
    return num / den


def _head_pair(q2, scores_fn, values):
    first_half = lax.broadcasted_iota(jnp.int32, q2.shape, 1) < NA_HEAD_DIM
    zero = jnp.zeros_like(q2)
    out_even = _softmax_pv(scores_fn(jnp.where(first_half, q2, zero), 0), values)
    out_odd = _softmax_pv(scores_fn(jnp.where(first_half, zero, q2), 1), values)
    return jnp.where(first_half, out_even, out_odd)


def _ctx_attn_kernel(q_ref, k_ref, v_ref, o_ref, *, heads):
    dh = NA_HEAD_DIM
    q = (q_ref[...] * (dh ** -0.5)).astype(BF16)
    k = k_ref[...].astype(BF16)
    v = v_ref[...].astype(BF16)
    outs = []
    for hp in range(heads // 2):
        sl = slice(hp * LANES, (hp + 1) * LANES)
        outs.append(_head_pair(q[:, sl], lambda qh, half: [_dot_nt(qh, k[:, sl])], [v[:, sl]]))
    o_ref[...] = jnp.concatenate(outs, axis=1).astype(o_ref.dtype)


def _ctx_attention(cfg, p):
    s, nw = cfg.seq, cfg.na_width
    cb = cfg.col_qn // nw
    spec = lambda c: pl.BlockSpec((s, nw), lambda b: (b, c))
    return pl.pallas_call(
        functools.partial(_ctx_attn_kernel, heads=cfg.na_heads),
        out_shape=jax.ShapeDtypeStruct((cfg.n_ctx, nw), BF16),
        grid=(cfg.batch,),
        in_specs=[spec(cb), spec(cb + 1), spec(cb + 2)],
        out_specs=pl.BlockSpec((s, nw), lambda b: (b, 0)),
        compiler_params=_cparams(("arbitrary",)),
        name="context_attention",
    )(p, p, p)


NA_QROWS = 8


def _na_kernel(q_ref, kp_ref, kc_ref, kn_ref, vp_ref, vc_ref, vn_ref, kx_ref, vx_ref, bt_ref, o_ref,
               kw_scr, vw_scr, *, heads, grid_rows):
    dh = NA_HEAD_DIM
    blk = NA_QROWS * GRID_W
    win = WIN_ROWS * GRID_W
    m = pl.program_id(1)
    for slot, (kr, vr) in enumerate(((kp_ref, vp_ref), (kc_ref, vc_ref), (kn_ref, vn_ref))):
        kw_scr[slot * blk:(slot + 1) * blk, :] = kr[...].astype(BF16)
        vw_scr[slot * blk:(slot + 1) * blk, :] = vr[...].astype(BF16)
    kx = kx_ref[...].astype(BF16)
    vx = vx_ref[...].astype(BF16)

    def row_body(i, carry):
        r = m * NA_QROWS + i
        rs = jnp.clip(r - WIN_ROWS // 2, 0, grid_rows - WIN_ROWS)
        start = rs - r + (WIN_ROWS - 1)
        off = pl.multiple_of((rs - NA_QROWS * (m - 1)) * GRID_W, GRID_W)
        qrows = pl.ds(pl.multiple_of(i * GRID_W, GRID_W), GRID_W)
        q = (q_ref[qrows, :] * (dh ** -0.5)).astype(BF16)
        kwin = kw_scr[pl.ds(off, win), :]
        vwin = vw_scr[pl.ds(off, win), :]
        first_half = lax.broadcasted_iota(jnp.int32, (GRID_W, LANES), 1) < dh
        pairs = [slice(hp * LANES, (hp + 1) * LANES) for hp in range(heads // 2)]
        scores = []
        for hp, sl in enumerate(pairs):
            q2 = q[:, sl]
            zero = jnp.zeros_like(q2)
            qm = jnp.concatenate([jnp.where(first_half, q2, zero), jnp.where(first_half, zero, q2)], axis=0)
            scores.append((_dot_nt(kwin[:, sl], qm) + bt_ref[start, hp],
                           _dot_nt(kx[:, sl], qm)))
        probs = []
        for s_loc, s_ctx in scores:
            mx = jnp.maximum(jnp.max(s_loc, axis=0, keepdims=True), jnp.max(s_ctx, axis=0, keepdims=True))
            p_loc = jnp.exp(s_loc - mx)
            p_ctx = jnp.exp(s_ctx - mx)
            inv = 1.0 / (jnp.sum(p_loc, axis=0, keepdims=True) + jnp.sum(p_ctx, axis=0, keepdims=True))
            probs.append(((p_loc * inv).astype(BF16), (p_ctx * inv).astype(BF16)))
        outs = []
        for (p_loc, p_ctx), sl in zip(probs, pairs):
            pv = _dot_tn(p_loc, vwin[:, sl]) + _dot_tn(p_ctx, vx[:, sl])
            outs.append(jnp.where(first_half, pv[0:GRID_W], pv[GRID_W:2 * GRID_W]))
        o_ref[qrows, :] = jnp.concatenate(outs, axis=1).astype(o_ref.dtype)
        return carry

    lax.fori_loop(0, NA_QROWS, row_body, 0)


def _na_bias_table(rpb):
    heads = rpb.shape[0]
    qc = np.arange(GRID_W)[:, None]
    kc = np.arange(GRID_W)[None, :]
    cs = np.clip(qc - WIN_COLS // 2, 0, GRID_W - WIN_COLS)
    valid = (kc >= cs) & (kc < cs + WIN_COLS)
    dc = np.clip(kc - qc + WIN_COLS - 1, 0, 2 * WIN_COLS - 2)
    dr = np.arange(WIN_ROWS)[:, None] + np.arange(WIN_ROWS)[None, :]
    tab = rpb[:, dr][:, :, :, dc]
    tab = jnp.where(jnp.asarray(valid)[None, None, None], tab, NEG_INF)
    tab = tab.reshape(heads // 2, 2, WIN_ROWS, WIN_ROWS, GRID_W, GRID_W)
    tab = tab.transpose(2, 0, 3, 5, 1, 4)
    return tab.reshape(WIN_ROWS, heads // 2, WIN_ROWS * GRID_W, 2 * GRID_W)


def _na_latent(cfg, p, k_ctx, v_ctx, bias_tab):
    nw, heads = cfg.na_width, cfg.na_heads
    blk = NA_QROWS * GRID_W
    grid_rows = cfg.dec_seq // GRID_W
    nblk = grid_rows // NA_QROWS
    cb = cfg.col_qn // nw
    rb0 = cfg.n_ctx // blk
    assert rb0 * blk == cfg.n_ctx and nblk * NA_QROWS == grid_rows

    def rows(shift, c):
        return pl.BlockSpec((blk, nw), lambda b, m: (rb0 + b * nblk + jnp.clip(m + shift, 0, nblk - 1), c))

    ctx_spec = pl.BlockSpec((None, cfg.past_len, nw), lambda b, m: (b, 0, 0))
    return pl.pallas_call(
        functools.partial(_na_kernel, heads=heads, grid_rows=grid_rows),
        out_shape=jax.ShapeDtypeStruct((cfg.n_lat, nw), BF16),
        grid=(cfg.dec_batch, nblk),
        in_specs=[rows(0, cb), rows(-1, cb + 1), rows(0, cb + 1), rows(1, cb + 1),
                  rows(-1, cb + 2), rows(0, cb + 2), rows(1, cb + 2), ctx_spec, ctx_spec,
                  pl.BlockSpec((WIN_ROWS, heads // 2, WIN_ROWS * GRID_W, 2 * GRID_W), lambda b, m: (0, 0, 0, 0))],
        out_specs=pl.BlockSpec((blk, nw), lambda b, m: (b * nblk + m, 0)),
        scratch_shapes=[pltpu.VMEM((3 * blk, nw), BF16), pltpu.VMEM((3 * blk, nw), BF16)],
        compiler_params=_cparams(("arbitrary", "arbitrary")),
        name="neighbourhood_attention",
    )(p, p, p, p, p, p, p, k_ctx, v_ctx, bias_tab)


def _outproj_kernel(ym_ref, yc_ref, yn_ref, x_ref, w_ref, ga_ref, gpost_ref, gpre_ref, sc_ref, sh_ref,
                    xo_ref, ho_ref, y_scr, *, m_width, cv_width):
    a, b = m_width, m_width + cv_width
    y_scr[...] = (_dot(ym_ref[...], w_ref[0:a, :]) + _dot(yc_ref[...], w_ref[a:b, :])
                  + _dot(yn_ref[...], w_ref[b:, :]))

    def body(r, carry):
        rows = pl.ds(pl.multiple_of(r * ROW_CHUNK, ROW_CHUNK), ROW_CHUNK)
        x = x_ref[rows, :] + ga_ref[...] * (_rms(y_scr[rows, :]) * gpost_ref[...])
        xo_ref[rows, :] = x
        ho_ref[rows, :] = (_rms(x) * gpre_ref[...] * (1.0 + sc_ref[...]) + sh_ref[...]).astype(BF16)
        return carry

    lax.fori_loop(0, x_ref.shape[0] // ROW_CHUNK, body, 0)


def _outproj(cfg, ym, yc, yn, x, w_out, modl, g_post, g_pre):
    d, tm = cfg.d_model, cfg.tm_out
    rowblk = lambda w: pl.BlockSpec((tm, w), lambda i: (i, 0))
    vec = lambda: pl.BlockSpec((1, d), lambda i: (0, 0))
    return pl.pallas_call(
        functools.partial(_outproj_kernel, m_width=cfg.m_width, cv_width=cfg.cv_width),
        out_shape=(jax.ShapeDtypeStruct((cfg.n_tok, d), F32), jax.ShapeDtypeStruct((cfg.n_tok, d), BF16)),
        grid=(cfg.n_tok // tm,),
        in_specs=[rowblk(cfg.m_width), rowblk(cfg.cv_width), rowblk(cfg.na_width), rowblk(d),
                  pl.BlockSpec((d, d), lambda i: (0, 0)),
                  _mod_spec(cfg, 2, tm), vec(), vec(), _mod_spec(cfg, 4, tm), _mod_spec(cfg, 3, tm)],
        out_specs=(rowblk(d), rowblk(d)),
        scratch_shapes=[pltpu.VMEM((tm, d), F32)],
        compiler_params=_cparams(("arbitrary",)),
        name="out_projection",
    )(ym, yc, yn, x, w_out, modl, g_post.reshape(1, d), g_pre.reshape(1, d), modl, modl)


def _ffn_kernel(h_ref, x_ref, w1_ref, w2_ref, ga_ref, g_ref, o_ref, acc_ref):
    kk = pl.program_id(1)
    a = jnp.maximum(_dot(h_ref[...], w1_ref[...]), 0.0)
    part = _dot((a * a).astype(BF16), w2_ref[...])

    @pl.when(kk == 0)
    def _():
        acc_ref[...] = part

    @pl.when(kk > 0)
    def _():
        acc_ref[...] += part

    @pl.when(kk == pl.num_programs(1) - 1)
    def _():
        def body(r, carry):
            rows = pl.ds(pl.multiple_of(r * ROW_CHUNK, ROW_CHUNK), ROW_CHUNK)
            o_ref[rows, :] = x_ref[rows, :] + ga_ref[...] * (_rms(acc_ref[rows, :]) * g_ref[...])
            return carry

        lax.fori_loop(0, x_ref.shape[0] // ROW_CHUNK, body, 0)


def _ffn(cfg, h, x, w1, w2, modl, g_post):
    d, tm, th = cfg.d_model, cfg.tm_ff, cfg.th_ff
    return pl.pallas_call(
        _ffn_kernel,
        out_shape=jax.ShapeDtypeStruct((cfg.n_tok, d), F32),
        grid=(cfg.n_tok // tm, cfg.d_ff // th),
        in_specs=[pl.BlockSpec((tm, d), lambda i, k: (i, 0)),
                  pl.BlockSpec((tm, d), lambda i, k: (i, 0)),
                  pl.BlockSpec((d, th), lambda i, k: (0, k)),
                  pl.BlockSpec((th, d), lambda i, k: (k, 0)),
                  _mod_spec(cfg, 5, tm),
                  pl.BlockSpec((1, d), lambda i, k: (0, 0))],
        out_specs=pl.BlockSpec((tm, d), lambda i, k: (i, 0)),
        scratch_shapes=[pltpu.VMEM((tm, d), F32)],
        compiler_params=_cparams(("arbitrary", "arbitrary")),
        name="relu2_mlp",
    )(h, x, w1, w2, modl, g_post.reshape(1, d))


def _prep_w_in(cfg, w_in_l):
    mw, cv, nw, ng = cfg.m_width, cfg.cv_width, cfg.na_width, cfg.n_gates
    g0 = 4 * mw
    main = jnp.concatenate([w_in_l[:, :g0], w_in_l[:, g0 + ng:]], axis=1)
    gates = w_in_l[:, g0:g0 + ng]
    pad = cfg.in_cols - (main.shape[1] + ng)
    w_p = jnp.concatenate([main, gates, jnp.zeros((cfg.d_model, pad), w_in_l.dtype)], axis=1)
    return w_p.astype(BF16), gates.T.astype(BF16)


def _forward(cfg, x_prompt, x_sample, c, cache_k, cache_v, state_C, state_n, state_m, c_ctx, w_ada, b_ada,
             g_pre_mix, g_post_mix, g_pre_mlp, g_post_mlp, w_in, b_gate, g_mlstm, w_dw, b_dw, ln_g, ln_b,
             w_pw, rpb, w_out, w_ff1, w_ff2):
    d = cfg.d_model
    x = jnp.concatenate([x_prompt.reshape(cfg.n_ctx, d), x_sample.reshape(cfg.n_lat, d)], axis=0)

    n_cond = 1 + cfg.dec_batch
    cond_rows = -(-n_cond // 8) * 8
    cond = jnp.concatenate([c_ctx[None, :], c, jnp.zeros((cond_rows - n_cond, d), F32)], axis=0)
    mod = _modulation(cond, w_ada, b_ada, tn=min(1024, 6 * d)).reshape(cfg.depth, cond_rows, 6, 1, d)

    rope_tabs = _rope_tables(cfg)
    nw = cfg.na_width
    ks, vs, cs, ns, ms = [], [], [], [], []
    for l in range(cfg.depth):
        w_in_p, w_gt = _prep_w_in(cfg, w_in[l])
        p, gt = _inproj(cfg, x, mod[l], g_pre_mix[l], w_in_p, w_gt, rope_tabs)

        bias_rows = jnp.broadcast_to(b_gate[l].reshape(cfg.n_gates, 1, 1), (cfg.n_gates, 1, M_HEAD_DIM))
        ym_c, c_new, n_new, m_new = _mlstm(cfg, p, gt, bias_rows, g_mlstm[l], n_seq=cfg.batch, seq_len=cfg.seq,
                                           row0=0, n_sub=cfg.ctx_sub, unroll=1, name="mlstm_context",
                                           emit_state=True)
        (ym_l,) = _mlstm(cfg, p, gt, bias_rows, g_mlstm[l], n_seq=cfg.dec_batch, seq_len=cfg.dec_seq,
                         row0=cfg.n_ctx, n_sub=1, unroll=cfg.lat_unroll, name="mlstm_latent",
                         state0=(state_C[:, l], state_n[:, l], state_m[:, l]))
        yc = _conv_module(cfg, p, w_dw[l], b_dw[l], ln_g[l], ln_b[l], w_pw[l].astype(BF16))
        yn_c = _ctx_attention(cfg, p)
        yn_l = _na_latent(cfg, p, cache_k[:, l].reshape(cfg.dec_batch, cfg.past_len, nw),
                          cache_v[:, l].reshape(cfg.dec_batch, cfg.past_len, nw),
                          _na_bias_table(rpb[l]))
        ym = jnp.concatenate([ym_c, ym_l], axis=0)
        yn = jnp.concatenate([yn_c, yn_l], axis=0)

        x, h2 = _outproj(cfg, ym, yc, yn, x, w_out[l].astype(BF16), mod[l], g_post_mix[l], g_pre_mlp[l])
        x = _ffn(cfg, h2, x, w_ff1[l].astype(BF16), w_ff2[l].astype(BF16), mod[l], g_post_mlp[l])

        kcol = cfg.col_qn + nw
        ks.append(p[:cfg.n_ctx, kcol:kcol + nw].reshape(cfg.batch, cfg.seq, cfg.na_heads, NA_HEAD_DIM))
        vs.append(p[:cfg.n_ctx, kcol + nw:kcol + 2 * nw].reshape(cfg.batch, cfg.seq, cfg.na_heads, NA_HEAD_DIM))
        cs.append(c_new)
        ns.append(n_new[:, :, :, 0, :])
        ms.append(m_new[:, :, :, 0, 0])

    y_prompt = x[:cfg.n_ctx].reshape(cfg.batch, cfg.seq, d)
    y_sample = x[cfg.n_ctx:].reshape(cfg.dec_batch, cfg.dec_seq, d)
    return (y_prompt, y_sample, jnp.stack(ks, axis=1), jnp.stack(vs, axis=1),
            jnp.stack(cs, axis=1), jnp.stack(ns, axis=1), jnp.stack(ms, axis=1))


def _make_cfg(x_prompt, x_sample, cache_k, w_in, w_ff1, w_pw, **tiles):
    batch, seq, d = x_prompt.shape
    dec_batch, dec_seq, _ = x_sample.shape
    return Cfg(d_model=d, batch=batch, seq=seq, depth=w_in.shape[0], dec_batch=dec_batch, dec_seq=dec_seq,
               past_len=cache_k.shape[2], m_heads=d // (2 * M_HEAD_DIM), cv_width=w_pw.shape[-1],
               na_heads=cache_k.shape[3], d_ff=w_ff1.shape[-1], **tiles)


def kernel(x_prompt, x_sample, c, cache_k, cache_v, state_C, state_n, state_m, c_ctx, w_ada, b_ada, g_pre_mix,
           g_post_mix, g_pre_mlp, g_post_mlp, w_in, b_gate, g_mlstm, w_dw, b_dw, ln_g, ln_b, w_pw, rpb, w_out,
           w_ff1, w_ff2):
    cfg = _make_cfg(x_prompt, x_sample, cache_k, w_in, w_ff1, w_pw,
                    tm_in=1024, tm_out=512, tm_ff=512, tn_in=1024, th_ff=1024, ctx_sub=4, lat_unroll=4,
                    conv_tt=256)
    return _forward(cfg, x_prompt, x_sample, c, cache_k, cache_v, state_C, state_n, state_m, c_ctx, w_ada, b_ada,
                    g_pre_mix, g_post_mix, g_pre_mlp, g_post_mlp, w_in, b_gate, g_mlstm, w_dw, b_dw, ln_g, ln_b,
                    w_pw, rpb, w_out, w_ff1, w_ff2)
```

```python
import functools
from typing import NamedTuple

import numpy as np
import jax
import jax.numpy as jnp
from jax import lax
from jax.experimental import pallas as pl
from jax.experimental.pallas import tpu as pltpu

F32 = jnp.float32
BF16 = jnp.bfloat16

EPS = 1e-6
NEG_INF = -1e30
ROPE_BASE = 10000.0

LANES = 128
SUBLANES = 8
M_HEAD_DIM = 128
NA_HEAD_DIM = 64
GRID_W = 64
WIN_ROWS = 8
WIN_COLS = 16
CONV_TAPS = 31
CONV_HALO = 16
VMEM_LIMIT = 52 * 1024 * 1024


class Cfg(NamedTuple):
    d_model: int
    batch: int
    seq: int
    depth: int
    dec_batch: int
    dec_seq: int
    past_len: int
    m_heads: int
    cv_width: int
    na_heads: int
    d_ff: int
    tm_in: int
    tm_out: int
    tm_ff: int
    tn_in: int
    th_ff: int
    ctx_sub: int
    lat_unroll: int
    conv_tt: int

    @property
    def m_width(self):
        return self.m_heads * M_HEAD_DIM

    @property
    def na_width(self):
        return self.na_heads * NA_HEAD_DIM

    @property
    def n_gates(self):
        return 4 * self.m_heads

    @property
    def n_ctx(self):
        return self.batch * self.seq

    @property
    def n_lat(self):
        return self.dec_batch * self.dec_seq

    @property
    def n_tok(self):
        return self.n_ctx + self.n_lat

    @property
    def col_ca(self):
        return 4 * self.m_width

    @property
    def col_kv(self):
        return self.col_ca + 2 * self.cv_width

    @property
    def col_qn(self):
        return self.col_kv + 2 * self.na_width

    @property
    def col_gates(self):
        return self.col_qn + self.na_width

    @property
    def in_cols(self):
        raw = self.col_gates + LANES
        return -(-raw // self.tn_in) * self.tn_in


def _cparams(sem):
    return pltpu.CompilerParams(dimension_semantics=sem, vmem_limit_bytes=VMEM_LIMIT)


def _rms(x):
    return x * lax.rsqrt(jnp.mean(x * x, axis=-1, keepdims=True) + EPS)


def _dot(a, b):
    return jnp.dot(a, b, preferred_element_type=F32)


def _dot_nt(a, b):
    return lax.dot_general(a, b, (((1,), (1,)), ((), ())), preferred_element_type=F32)


def _dot_tn(a, b):
    return lax.dot_general(a, b, (((0,), (0,)), ((), ())), preferred_element_type=F32)


def _mod_kernel(c_ref, w_ref, b_ref, o_ref):
    c = c_ref[...]
    s = (c * jax.nn.sigmoid(c)).astype(BF16)
    o_ref[...] = _dot(s, w_ref[...].astype(BF16)) + b_ref[...]


def _modulation(cond, w_ada, b_ada, tn):
    depth, d, n6 = w_ada.shape
    rows = cond.shape[0]
    return pl.pallas_call(
        _mod_kernel,
        out_shape=jax.ShapeDtypeStruct((depth, rows, n6), F32),
        grid=(depth, n6 // tn),
        in_specs=[
            pl.BlockSpec((rows, d), lambda l, j: (0, 0)),
            pl.BlockSpec((None, d, tn), lambda l, j: (l, 0, j)),
            pl.BlockSpec((None, 1, tn), lambda l, j: (l, 0, j)),
        ],
        out_specs=pl.BlockSpec((None, rows, tn), lambda l, j: (l, 0, j)),
        compiler_params=_cparams(("arbitrary", "arbitrary")),
        name="adaln_modulation",
    )(cond, w_ada, b_ada.reshape(depth, 1, n6))


def _mod_row_map(cfg, which, tm):
    assert cfg.n_ctx % tm == 0 and cfg.dec_seq % tm == 0
    nct = cfg.n_ctx // tm
    tpb = cfg.dec_seq // tm

    def index_map(i, *_):
        row = jnp.where(i < nct, 0, 1 + (i - nct) // tpb)
        return (row, which, 0, 0)

    return index_map


def _mod_spec(cfg, which, tm):
    return pl.BlockSpec((None, None, 1, cfg.d_model), _mod_row_map(cfg, which, tm))


ROW_CHUNK = 128


def _rope(x, cos, sin_signed, even_block):
    swapped = jnp.where(even_block, pltpu.roll(x, 96, 1), pltpu.roll(x, 32, 1))
    return x * cos + swapped * sin_signed


IN_ROWS = 512


def _inproj_kernel(x_ref, sh_ref, sc_ref, g_ref, w_ref, wgt_ref, cos_ref, sin_ref, o_ref, gt_ref, kv_ref,
                   h_scr, acc_scr, *, q_tiles, k_tiles, kv_tile, n_ctx_tiles):
    i = pl.program_id(0)
    j = pl.program_id(1)
    tm, tn = o_ref.shape

    def norm_rows(rows):
        h = _rms(x_ref[rows, :]) * g_ref[...] * (1.0 + sc_ref[...]) + sh_ref[...]
        h_scr[rows, :] = h.astype(BF16)

    def matmul_rope(rows):
        acc_scr[rows, :] = _dot(h_scr[rows, :], w_ref[...])
        scale = jnp.where(j >= q_tiles, M_HEAD_DIM ** -0.5, 1.0).astype(F32)
        lane_blk = lax.broadcasted_iota(jnp.int32, (ROW_CHUNK, M_HEAD_DIM), 1) // 32
        even_block = (lane_blk % 2) == 0
        for r in range(rows.start, rows.stop, ROW_CHUNK):
            chunk = slice(r, r + ROW_CHUNK)
            cos, sin = cos_ref[chunk, :], sin_ref[chunk, :]
            for hh in range(tn // M_HEAD_DIM):
                cols = slice(hh * M_HEAD_DIM, (hh + 1) * M_HEAD_DIM)
                o_ref[chunk, cols] = (_rope(acc_scr[chunk, cols], cos, sin, even_block) * scale).astype(o_ref.dtype)

    @pl.when(j == 0)
    def _():
        blk_rows = min(IN_ROWS, tm)
        for blk in range(tm // blk_rows):
            for r in range(blk * blk_rows, (blk + 1) * blk_rows, ROW_CHUNK):
                norm_rows(slice(r, r + ROW_CHUNK))
            matmul_rope(slice(blk * blk_rows, (blk + 1) * blk_rows))
        gt_ref[...] = _dot_nt(wgt_ref[...], h_scr[...])

    @pl.when((j > 0) & (j < q_tiles + k_tiles))
    def _():
        matmul_rope(slice(0, tm))

    @pl.when((j >= q_tiles + k_tiles) & (j != kv_tile))
    def _():
        o_ref[...] = _dot(h_scr[...], w_ref[...]).astype(o_ref.dtype)

    @pl.when(j == kv_tile)
    def _():
        acc_scr[...] = _dot(h_scr[...], w_ref[...])
        o_ref[...] = acc_scr[...].astype(o_ref.dtype)

        @pl.when(i < n_ctx_tiles)
        def _():
            kv_ref[...] = acc_scr[...]


def _inproj(cfg, x, modl, g, w_in_p, w_gt, rope_tabs):
    d, tm, tn = cfg.d_model, cfg.tm_in, cfg.tn_in
    ng = cfg.n_gates
    nct = cfg.n_ctx // tm
    assert cfg.m_width % tn == 0 and cfg.col_kv % tn == 0 and 2 * cfg.na_width == tn and nct * tm == cfg.n_ctx
    kern = functools.partial(_inproj_kernel, q_tiles=cfg.m_width // tn, k_tiles=cfg.m_width // tn,
                             kv_tile=cfg.col_kv // tn, n_ctx_tiles=nct)
    return pl.pallas_call(
        kern,
        out_shape=(jax.ShapeDtypeStruct((cfg.n_tok, cfg.in_cols), BF16),
                   jax.ShapeDtypeStruct((ng, cfg.n_tok), F32),
                   jax.ShapeDtypeStruct((cfg.n_ctx, tn), F32)),
        grid=(cfg.n_tok // tm, cfg.in_cols // tn),
        in_specs=[
            pl.BlockSpec((tm, d), lambda i, j: (i, 0)),
            _mod_spec(cfg, 0, tm),
            _mod_spec(cfg, 1, tm),
            pl.BlockSpec((1, d), lambda i, j: (0, 0)),
            pl.BlockSpec((d, tn), lambda i, j: (0, j)),
            pl.BlockSpec((ng, d), lambda i, j: (0, 0)),
            pl.BlockSpec((tm, M_HEAD_DIM), lambda i, j: (i, 0)),
            pl.BlockSpec((tm, M_HEAD_DIM), lambda i, j: (i, 0)),
        ],
        out_specs=(pl.BlockSpec((tm, tn), lambda i, j: (i, j)),
                   pl.BlockSpec((ng, tm), lambda i, j: (0, i)),
                   pl.BlockSpec((tm, tn), lambda i, j: (jnp.minimum(i, nct - 1), 0))),
        scratch_shapes=[pltpu.VMEM((tm, d), BF16), pltpu.VMEM((tm, tn), F32)],
        compiler_params=_cparams(("arbitrary", "arbitrary")),
        name="in_projection",
    )(x, modl, modl, g.reshape(1, d), w_in_p, w_gt, *rope_tabs)


ROW_A, ROW_M, ROW_E, ROW_WK, ROW_MPREV, ROW_DECAY = range(6)
FINISH_ROWS = 512


def _log_sigmoid(x):
    return jnp.minimum(x, 0.0) - jnp.log1p(jnp.exp(-jnp.abs(x)))


def _cumsum_lanes(x, cum_mat):
    hi = x.astype(BF16)
    r1 = x - hi.astype(F32)
    mid = r1.astype(BF16)
    lo = (r1 - mid.astype(F32)).astype(BF16)
    return _dot(hi, cum_mat) + _dot(mid, cum_mat) + _dot(lo, cum_mat)


def _cummax_lanes(x, reverse):
    n = x.shape[1]
    lane = lax.broadcasted_iota(jnp.int32, x.shape, 1)
    k = 1
    while k < n:
        if reverse:
            shifted, ok = pltpu.roll(x, n - k, 1), lane < n - k
        else:
            shifted, ok = pltpu.roll(x, k, 1), lane >= k
        x = jnp.maximum(x, jnp.where(ok, shifted, -jnp.inf))
        k *= 2
    return x


def _col_bcast(row, n):
    return jnp.transpose(jnp.broadcast_to(row, (n, row.shape[1])))


def _mlstm_kernel(*refs, seq_len, n_sub, heads, unroll, init_state, emit_state):
    it = iter(refs)
    q_ref, k_ref, v_ref, om_ref, gt_ref, bias_ref, gm_ref = (next(it) for _ in range(7))
    c0_ref = n0_ref = m0_ref = None
    if init_state:
        c0_ref, n0_ref, m0_ref = next(it), next(it), next(it)
    o_ref = next(it)
    co_ref = no_ref = mo_ref = None
    if emit_state:
        co_ref, no_ref, mo_ref = next(it), next(it), next(it)
    hs_f, hs_b, cext, rows_scr = next(it), next(it), next(it), next(it)

    L = dk = M_HEAD_DIM
    head = pl.program_id(1)
    nc = seq_len // L

    tt = lax.broadcasted_iota(jnp.int32, (L, L), 0)
    ss = lax.broadcasted_iota(jnp.int32, (L, L), 1)
    cum_mats = ((tt <= ss).astype(BF16), (tt >= ss).astype(BF16))
    causal = (ss <= tt, ss >= tt)

    for d in range(2):
        row_i = d * 2 * heads + head
        xi = gt_ref[row_i] + bias_ref[row_i]
        lf = _log_sigmoid(gt_ref[row_i + heads] + bias_ref[row_i + heads])
        b = _cumsum_lanes(lf, cum_mats[d])
        tot = jnp.sum(lf, axis=1, keepdims=True)
        a = xi - b
        g = tot - b + xi
        gmax = jnp.max(g, axis=1, keepdims=True)
        arun = _cummax_lanes(a, reverse=(d == 1))
        for s in range(n_sub):
            m = m0_ref[s, d][:, 0:1] if init_state else jnp.zeros((1, 1), F32)
            for c in (range(nc) if d == 0 else reversed(range(nc))):
                r = s * nc + c
                m_new = jnp.maximum(tot[r:r + 1] + m, gmax[r:r + 1])
                m_row = jnp.maximum(arun[r:r + 1], m)
                rows_scr[d, r, ROW_A:ROW_A + 1, :] = a[r:r + 1]
                rows_scr[d, r, ROW_M:ROW_M + 1, :] = m_row
                rows_scr[d, r, ROW_E:ROW_E + 1, :] = -(b[r:r + 1] + m_row)
                rows_scr[d, r, ROW_WK:ROW_WK + 1, :] = jnp.exp(g[r:r + 1] - m_new)
                rows_scr[d, r, ROW_MPREV:ROW_MPREV + 1, :] = jnp.broadcast_to(m, (1, L))
                rows_scr[d, r, ROW_DECAY:ROW_DECAY + 1, :] = jnp.broadcast_to(jnp.exp(tot[r:r + 1] + m - m_new), (1, L))
                m = m_new
            if emit_state:
                mo_ref[s, d] = jnp.broadcast_to(m, (1, LANES))
            if init_state:
                cext[s, d, :, 0:dk] = c0_ref[s, d]
                cext[s, d, :, dk:2 * dk] = _col_bcast(n0_ref[s, d], dk)
            else:
                cext[s, d] = jnp.zeros((dk, 2 * dk), F32)

    ones_ext = jnp.ones((L, dk), BF16)

    def prepare(s, d, c):
        rows = pl.ds(pl.multiple_of(s * seq_len + c * L, L), L)
        tile = rows_scr[d, s * nc + c]
        q = q_ref[rows, :].astype(BF16)
        k = k_ref[rows, :]
        v_ext = jnp.concatenate([v_ref[rows, :].astype(BF16), ones_ext], axis=1)
        m_b = _col_bcast(tile[ROW_M:ROW_M + 1], dk)
        floor = jnp.exp(_col_bcast(tile[ROW_E:ROW_E + 1], dk))
        dmat = jnp.exp(jnp.where(causal[d], tile[ROW_A:ROW_A + 1] - m_b, -jnp.inf))
        sc = (_dot_nt(q, k.astype(BF16)) * dmat).astype(BF16)
        w_int = jnp.exp(tile[ROW_MPREV:ROW_MPREV + 1] - m_b)
        kt_w = (jnp.transpose(k.astype(F32)) * tile[ROW_WK:ROW_WK + 1]).astype(BF16)
        return rows, q, v_ext, sc, w_int, floor, kt_w, tile[ROW_DECAY:ROW_DECAY + 1, 0:1]

    def apply(s, d, prepared):
        rows, q, v_ext, sc, w_int, floor, kt_w, decay = prepared
        c_prev = cext[s, d]
        nd = _dot(sc, v_ext) + jnp.concatenate([w_int, w_int], axis=1) * _dot(q, c_prev.astype(BF16))
        den = jnp.maximum(jnp.abs(nd[:, dk:2 * dk]), floor)
        (hs_f if d == 0 else hs_b)[rows, :] = nd[:, 0:dk] / den
        cext[s, d] = decay * c_prev + _dot(kt_w, v_ext)

    def chunk_body(i, carry):
        todo = []
        for u in range(unroll):
            c = i * unroll + u
            for s in range(n_sub):
                todo.append((s, 0, prepare(s, 0, c)))
                todo.append((s, 1, prepare(s, 1, nc - 1 - c)))
        for s, d, prepared in todo:
            apply(s, d, prepared)
        return carry

    lax.fori_loop(0, nc // unroll, chunk_body, 0)

    def finish(r, carry):
        rows = pl.ds(pl.multiple_of(r * FINISH_ROWS, FINISH_ROWS), FINISH_ROWS)
        y = _rms(hs_f[rows, :] + hs_b[rows, :]) * gm_ref[...] * jax.nn.sigmoid(om_ref[rows, :].astype(F32))
        o_ref[rows, :] = y.astype(o_ref.dtype)
        return carry

    lax.fori_loop(0, n_sub * seq_len // FINISH_ROWS, finish, 0)

    if emit_state:
        for s in range(n_sub):
            for d in range(2):
                co_ref[s, d] = cext[s, d, :, 0:dk]
                no_ref[s, d] = jnp.transpose(cext[s, d, :, dk:2 * dk])[0:1, :]


def _mlstm(cfg, p, gt, bias_rows, g_mlstm, *, n_seq, seq_len, row0, n_sub, unroll, name, state0=None,
           emit_state=False):
    heads, dk = cfg.m_heads, M_HEAD_DIM
    L = dk
    nc = seq_len // L
    blk_rows = n_sub * seq_len
    sb = row0 // blk_rows
    assert sb * blk_rows == row0 and nc * L == seq_len and n_seq % n_sub == 0 and nc % unroll == 0
    assert (n_sub * nc) % 8 == 0
    n_steps = n_seq // n_sub

    def col(block0):
        return pl.BlockSpec((blk_rows, dk), lambda b, h: (sb + b, block0 + h))

    gt4 = gt[:, row0:row0 + n_seq * seq_len].reshape(cfg.n_gates, n_steps, n_sub * nc, L)
    in_specs = [col(0), col(heads), col(2 * heads), col(3 * heads),
                pl.BlockSpec((cfg.n_gates, None, n_sub * nc, L), lambda b, h: (0, b, 0, 0)),
                pl.BlockSpec((cfg.n_gates, 1, L), lambda b, h: (0, 0, 0)),
                pl.BlockSpec((None, 1, dk), lambda b, h: (h, 0, 0))]
    args = [p, p, p, p, gt4, bias_rows, g_mlstm.reshape(heads, 1, dk)]
    state_spec = lambda r, c: pl.BlockSpec((n_sub, 2, None, r, c), lambda b, h: (b, 0, h, 0, 0))
    if state0 is not None:
        c0, n0, m0 = state0
        in_specs += [state_spec(dk, dk), state_spec(1, dk), state_spec(1, LANES)]
        args += [c0, n0.reshape(n_seq, 2, heads, 1, dk),
                 jnp.broadcast_to(m0[..., None, None], (n_seq, 2, heads, 1, LANES))]
    out_shape = [jax.ShapeDtypeStruct((n_seq * seq_len, cfg.m_width), BF16)]
    out_specs = [pl.BlockSpec((blk_rows, dk), lambda b, h: (b, h))]
    if emit_state:
        out_shape += [jax.ShapeDtypeStruct((n_seq, 2, heads, dk, dk), F32),
                      jax.ShapeDtypeStruct((n_seq, 2, heads, 1, dk), F32),
                      jax.ShapeDtypeStruct((n_seq, 2, heads, 1, LANES), F32)]
        out_specs += [state_spec(dk, dk), state_spec(1, dk), state_spec(1, LANES)]
    kern = functools.partial(_mlstm_kernel, seq_len=seq_len, n_sub=n_sub, heads=heads, unroll=unroll,
                             init_state=state0 is not None, emit_state=emit_state)
    return pl.pallas_call(
        kern,
        out_shape=tuple(out_shape),
        grid=(n_steps, heads),
        in_specs=in_specs,
        out_specs=tuple(out_specs),
        scratch_shapes=[pltpu.VMEM((blk_rows, dk), F32), pltpu.VMEM((blk_rows, dk), F32),
                        pltpu.VMEM((n_sub, 2, dk, 2 * dk), F32), pltpu.VMEM((2, n_sub * nc, 8, L), F32)],
        compiler_params=_cparams(("arbitrary", "arbitrary")),
        name=name,
    )(*args)


def _rope_tables(cfg):
    half = M_HEAD_DIM // 2
    nf = half // 2
    t = jnp.arange(cfg.dec_seq)
    inv = ROPE_BASE ** (-jnp.arange(nf, dtype=F32) / nf)
    ang_r = (t // GRID_W).astype(F32)[:, None] * inv[None, :]
    ang_c = (t % GRID_W).astype(F32)[:, None] * inv[None, :]
    cos = jnp.concatenate([jnp.cos(ang_r)] * 2 + [jnp.cos(ang_c)] * 2, axis=-1)
    sin = jnp.concatenate([-jnp.sin(ang_r), jnp.sin(ang_r), -jnp.sin(ang_c), jnp.sin(ang_c)], axis=-1)
    cos = jnp.concatenate([jnp.ones((cfg.n_ctx, M_HEAD_DIM), F32), jnp.tile(cos, (cfg.dec_batch, 1))], axis=0)
    sin = jnp.concatenate([jnp.zeros((cfg.n_ctx, M_HEAD_DIM), F32), jnp.tile(sin, (cfg.dec_batch, 1))], axis=0)
    return cos, sin


CONV_ROWS = 32


def _conv_kernel(a_ref, g_ref, ap_ref, gp_ref, an_ref, gn_ref, wdw_ref, bdw_ref, lng_ref, lnb_ref, wpw_ref,
                 o_ref, u_scr, *, n_ctx_tiles, ctx_tiles_per_seq, lat_tiles_per_seq):
    i = pl.program_id(0)
    tt = a_ref.shape[0]
    halo = CONV_HALO
    pos = jnp.where(i < n_ctx_tiles, i % ctx_tiles_per_seq, (i - n_ctx_tiles) % lat_tiles_per_seq)
    per_seq = jnp.where(i < n_ctx_tiles, ctx_tiles_per_seq, lat_tiles_per_seq)
    keep_prev = (pos > 0).astype(F32)
    keep_next = (pos < per_seq - 1).astype(F32)

    glu = lambda a, g: a[...].astype(F32) * jax.nn.sigmoid(g[...].astype(F32))
    u_scr[0, 0:halo, :] = glu(ap_ref, gp_ref) * keep_prev
    u_scr[0, halo:halo + tt, :] = glu(a_ref, g_ref)
    u_scr[0, halo + tt:2 * halo + tt, :] = glu(an_ref, gn_ref) * keep_next
    n_rows = tt + 2 * halo
    window = u_scr[0]
    for r in range(1, SUBLANES):
        u_scr[r] = pltpu.roll(window, n_rows - r, 0)

    for blk in range(tt // CONV_ROWS):
        base = blk * CONV_ROWS
        acc = jnp.broadcast_to(bdw_ref[...], (CONV_ROWS, a_ref.shape[1]))
        for tap in range(CONV_TAPS):
            start = base + halo - CONV_TAPS // 2 + tap
            r, aligned = start % SUBLANES, start - start % SUBLANES
            w_tap = jnp.concatenate([wdw_ref[tap]] * (CONV_ROWS // SUBLANES), axis=0)
            acc = acc + u_scr[r, aligned:aligned + CONV_ROWS, :] * w_tap
        mu = jnp.mean(acc, axis=-1, keepdims=True)
        cen = acc - mu
        var = jnp.mean(cen * cen, axis=-1, keepdims=True)
        y = cen * lax.rsqrt(var + EPS) * lng_ref[...] + lnb_ref[...]
        y = y * jax.nn.sigmoid(y)
        o_ref[base:base + CONV_ROWS, :] = _dot(y.astype(BF16), wpw_ref[...]).astype(o_ref.dtype)


def _conv_module(cfg, p, w_dw, b_dw, ln_g, ln_b, w_pw):
    tt, cv, halo = cfg.conv_tt, cfg.cv_width, CONV_HALO
    n_tiles = cfg.n_tok // tt
    cb = cfg.col_ca // cv
    hb = tt // halo
    last_halo = cfg.n_tok // halo - 1
    main = lambda c: pl.BlockSpec((tt, cv), lambda i: (i, c))
    prev = lambda c: pl.BlockSpec((halo, cv), lambda i: (jnp.maximum(i * hb - 1, 0), c))
    nxt = lambda c: pl.BlockSpec((halo, cv), lambda i: (jnp.minimum((i + 1) * hb, last_halo), c))
    vec = lambda: pl.BlockSpec((1, cv), lambda i: (0, 0))
    kern = functools.partial(_conv_kernel, n_ctx_tiles=cfg.n_ctx // tt, ctx_tiles_per_seq=cfg.seq // tt,
                             lat_tiles_per_seq=cfg.dec_seq // tt)
    return pl.pallas_call(
        kern,
        out_shape=jax.ShapeDtypeStruct((cfg.n_tok, cv), BF16),
        grid=(n_tiles,),
        in_specs=[main(cb), main(cb + 1), prev(cb), prev(cb + 1), nxt(cb), nxt(cb + 1),
                  pl.BlockSpec((CONV_TAPS, SUBLANES, cv), lambda i: (0, 0, 0)), vec(), vec(), vec(),
                  pl.BlockSpec((cv, cv), lambda i: (0, 0))],
        out_specs=pl.BlockSpec((tt, cv), lambda i: (i, 0)),
        scratch_shapes=[pltpu.VMEM((SUBLANES, tt + 2 * halo, cv), F32)],
        compiler_params=_cparams(("arbitrary",)),
        name="conv_module",
    )(p, p, p, p, p, p, jnp.broadcast_to(w_dw[:, None, :], (CONV_TAPS, SUBLANES, cv)),
      b_dw.reshape(1, cv), ln_g.reshape(1, cv), ln_b.reshape(1, cv), w_pw)


def _softmax_pv(scores, values):
    mx = functools.reduce(jnp.maximum, [jnp.max(s, axis=1, keepdims=True) for s in scores])
    ps = [jnp.exp(s - mx) for s in scores]
    den = functools.reduce(jnp.add, [jnp.sum(p, axis=1, keepdims=True) for p in ps])
    num = functools.reduce(jnp.add, [_dot(p.astype(BF16), v) for p, v in zip(ps, values)])
    return num / den


def _head_pair(q2, scores_fn, values):
    first_half = lax.broadcasted_iota(jnp.int32, q2.shape, 1) < NA_HEAD_DIM
    zero = jnp.zeros_like(q2)
    out_even = _softmax_pv(scores_fn(jnp.where(first_half, q2, zero), 0), values)
    out_odd = _softmax_pv(scores_fn(jnp.where(first_half, zero, q2), 1), values)
    return jnp.where(first_half, out_even, out_odd)


def _ctx_attn_kernel(q_ref, k_ref, v_ref, o_ref, *, heads):
    dh = NA_HEAD_DIM
    q = (q_ref[...] * (dh ** -0.5)).astype(BF16)
    k = k_ref[...].astype(BF16)
    v = v_ref[...].astype(BF16)
    outs = []
    for hp in range(heads // 2):
        sl = slice(hp * LANES, (hp + 1) * LANES)
        outs.append(_head_pair(q[:, sl], lambda qh, half: [_dot_nt(qh, k[:, sl])], [v[:, sl]]))
    o_ref[...] = jnp.concatenate(outs, axis=1).astype(o_ref.dtype)


def _ctx_attention(cfg, p):
    s, nw = cfg.seq, cfg.na_width
    qb, kb = cfg.col_qn // nw, cfg.col_kv // nw
    spec = lambda c: pl.BlockSpec((s, nw), lambda b: (b, c))
    return pl.pallas_call(
        functools.partial(_ctx_attn_kernel, heads=cfg.na_heads),
        out_shape=jax.ShapeDtypeStruct((cfg.n_ctx, nw), BF16),
        grid=(cfg.batch,),
        in_specs=[spec(qb), spec(kb), spec(kb + 1)],
        out_specs=pl.BlockSpec((s, nw), lambda b: (b, 0)),
        compiler_params=_cparams(("arbitrary",)),
        name="context_attention",
    )(p, p, p)


NA_QROWS = 8


def _na_kernel(q_ref, kp_ref, kc_ref, kn_ref, vp_ref, vc_ref, vn_ref, kx_ref, vx_ref, bt_ref, o_ref,
               kw_scr, vw_scr, *, heads, grid_rows):
    dh = NA_HEAD_DIM
    blk = NA_QROWS * GRID_W
    win = WIN_ROWS * GRID_W
    m = pl.program_id(1)
    for slot, (kr, vr) in enumerate(((kp_ref, vp_ref), (kc_ref, vc_ref), (kn_ref, vn_ref))):
        kw_scr[slot * blk:(slot + 1) * blk, :] = kr[...].astype(BF16)
        vw_scr[slot * blk:(slot + 1) * blk, :] = vr[...].astype(BF16)
    kx = kx_ref[...].astype(BF16)
    vx = vx_ref[...].astype(BF16)

    def row_body(i, carry):
        r = m * NA_QROWS + i
        rs = jnp.clip(r - WIN_ROWS // 2, 0, grid_rows - WIN_ROWS)
        start = rs - r + (WIN_ROWS - 1)
        off = pl.multiple_of((rs - NA_QROWS * (m - 1)) * GRID_W, GRID_W)
        qrows = pl.ds(pl.multiple_of(i * GRID_W, GRID_W), GRID_W)
        q = (q_ref[qrows, :] * (dh ** -0.5)).astype(BF16)
        kwin = kw_scr[pl.ds(off, win), :]
        vwin = vw_scr[pl.ds(off, win), :]
        first_half = lax.broadcasted_iota(jnp.int32, (GRID_W, LANES), 1) < dh
        pairs = [slice(hp * LANES, (hp + 1) * LANES) for hp in range(heads // 2)]
        scores = []
        for hp, sl in enumerate(pairs):
            q2 = q[:, sl]
            zero = jnp.zeros_like(q2)
            qm = jnp.concatenate([jnp.where(first_half, q2, zero), jnp.where(first_half, zero, q2)], axis=0)
            scores.append((_dot_nt(kwin[:, sl], qm) + bt_ref[start, hp],
                           _dot_nt(kx[:, sl], qm)))
        probs = []
        for s_loc, s_ctx in scores:
            mx = jnp.maximum(jnp.max(s_loc, axis=0, keepdims=True), jnp.max(s_ctx, axis=0, keepdims=True))
            p_loc = jnp.exp(s_loc - mx)
            p_ctx = jnp.exp(s_ctx - mx)
            inv = 1.0 / (jnp.sum(p_loc, axis=0, keepdims=True) + jnp.sum(p_ctx, axis=0, keepdims=True))
            probs.append(((p_loc * inv).astype(BF16), (p_ctx * inv).astype(BF16)))
        outs = []
        for (p_loc, p_ctx), sl in zip(probs, pairs):
            pv = _dot_tn(p_loc, vwin[:, sl]) + _dot_tn(p_ctx, vx[:, sl])
            outs.append(jnp.where(first_half, pv[0:GRID_W], pv[GRID_W:2 * GRID_W]))
        o_ref[qrows, :] = jnp.concatenate(outs, axis=1).astype(o_ref.dtype)
        return carry

    lax.fori_loop(0, NA_QROWS, row_body, 0)


def _na_bias_table(rpb):
    heads = rpb.shape[0]
    qc = np.arange(GRID_W)[:, None]
    kc = np.arange(GRID_W)[None, :]
    cs = np.clip(qc - WIN_COLS // 2, 0, GRID_W - WIN_COLS)
    valid = (kc >= cs) & (kc < cs + WIN_COLS)
    dc = np.clip(kc - qc + WIN_COLS - 1, 0, 2 * WIN_COLS - 2)
    dr = np.arange(WIN_ROWS)[:, None] + np.arange(WIN_ROWS)[None, :]
    tab = rpb[:, dr][:, :, :, dc]
    tab = jnp.where(jnp.asarray(valid)[None, None, None], tab, NEG_INF)
    tab = tab.reshape(heads // 2, 2, WIN_ROWS, WIN_ROWS, GRID_W, GRID_W)
    tab = tab.transpose(2, 0, 3, 5, 1, 4)
    return tab.reshape(WIN_ROWS, heads // 2, WIN_ROWS * GRID_W, 2 * GRID_W)


def _na_latent(cfg, p, k_ctx, v_ctx, bias_tab):
    nw, heads = cfg.na_width, cfg.na_heads
    blk = NA_QROWS * GRID_W
    grid_rows = cfg.dec_seq // GRID_W
    nblk = grid_rows // NA_QROWS
    qb, kb = cfg.col_qn // nw, cfg.col_kv // nw
    rb0 = cfg.n_ctx // blk
    assert rb0 * blk == cfg.n_ctx and nblk * NA_QROWS == grid_rows

    def rows(shift, c):
        return pl.BlockSpec((blk, nw), lambda b, m: (rb0 + b * nblk + jnp.clip(m + shift, 0, nblk - 1), c))

    ctx_spec = pl.BlockSpec((None, cfg.past_len, nw), lambda b, m: (b, 0, 0))
    return pl.pallas_call(
        functools.partial(_na_kernel, heads=heads, grid_rows=grid_rows),
        out_shape=jax.ShapeDtypeStruct((cfg.n_lat, nw), BF16),
        grid=(cfg.dec_batch, nblk),
        in_specs=[rows(0, qb), rows(-1, kb), rows(0, kb), rows(1, kb),
                  rows(-1, kb + 1), rows(0, kb + 1), rows(1, kb + 1), ctx_spec, ctx_spec,
                  pl.BlockSpec((WIN_ROWS, heads // 2, WIN_ROWS * GRID_W, 2 * GRID_W), lambda b, m: (0, 0, 0, 0))],
        out_specs=pl.BlockSpec((blk, nw), lambda b, m: (b * nblk + m, 0)),
        scratch_shapes=[pltpu.VMEM((3 * blk, nw), BF16), pltpu.VMEM((3 * blk, nw), BF16)],
        compiler_params=_cparams(("arbitrary", "arbitrary")),
        name="neighbourhood_attention",
    )(p, p, p, p, p, p, p, k_ctx, v_ctx, bias_tab)


OUT_ROWS = 256


def _outproj_kernel(ym_ref, yc_ref, yn_ref, x_ref, w_ref, ga_ref, gpost_ref, gpre_ref, sc_ref, sh_ref,
                    xo_ref, ho_ref, y_scr, *, m_width, cv_width):
    a, b = m_width, m_width + cv_width
    for blk in range(x_ref.shape[0] // OUT_ROWS):
        blk_rows = slice(blk * OUT_ROWS, (blk + 1) * OUT_ROWS)
        y_scr[blk_rows, :] = (_dot(ym_ref[blk_rows, :], w_ref[0:a, :]) + _dot(yc_ref[blk_rows, :], w_ref[a:b, :])
                              + _dot(yn_ref[blk_rows, :], w_ref[b:, :]))
        for r in range(OUT_ROWS // ROW_CHUNK):
            rows = slice(blk * OUT_ROWS + r * ROW_CHUNK, blk * OUT_ROWS + (r + 1) * ROW_CHUNK)
            x = x_ref[rows, :] + ga_ref[...] * (_rms(y_scr[rows, :]) * gpost_ref[...])
            xo_ref[rows, :] = x
            ho_ref[rows, :] = (_rms(x) * gpre_ref[...] * (1.0 + sc_ref[...]) + sh_ref[...]).astype(BF16)


def _outproj(cfg, ym, yc, yn, x, w_out, modl, g_post, g_pre):
    d, tm = cfg.d_model, cfg.tm_out
    rowblk = lambda w: pl.BlockSpec((tm, w), lambda i: (i, 0))
    vec = lambda: pl.BlockSpec((1, d), lambda i: (0, 0))
    return pl.pallas_call(
        functools.partial(_outproj_kernel, m_width=cfg.m_width, cv_width=cfg.cv_width),
        out_shape=(jax.ShapeDtypeStruct((cfg.n_tok, d), F32), jax.ShapeDtypeStruct((cfg.n_tok, d), BF16)),
        grid=(cfg.n_tok // tm,),
        in_specs=[rowblk(cfg.m_width), rowblk(cfg.cv_width), rowblk(cfg.na_width), rowblk(d),
                  pl.BlockSpec((d, d), lambda i: (0, 0)),
                  _mod_spec(cfg, 2, tm), vec(), vec(), _mod_spec(cfg, 4, tm), _mod_spec(cfg, 3, tm)],
        out_specs=(rowblk(d), rowblk(d)),
        scratch_shapes=[pltpu.VMEM((tm, d), F32)],
        compiler_params=_cparams(("arbitrary",)),
        name="out_projection",
    )(ym, yc, yn, x, w_out, modl, g_post.reshape(1, d), g_pre.reshape(1, d), modl, modl)


def _ffn_kernel(h_ref, x_ref, w1_ref, w2_ref, ga_ref, g_ref, o_ref, acc_ref):
    kk = pl.program_id(1)
    a = jnp.maximum(_dot(h_ref[...], w1_ref[...]), 0.0)
    part = _dot((a * a).astype(BF16), w2_ref[...])

    @pl.when(kk == 0)
    def _():
        acc_ref[...] = part

    @pl.when(kk > 0)
    def _():
        acc_ref[...] += part

    @pl.when(kk == pl.num_programs(1) - 1)
    def _():
        def body(r, carry):
            rows = pl.ds(pl.multiple_of(r * ROW_CHUNK, ROW_CHUNK), ROW_CHUNK)
            o_ref[rows, :] = x_ref[rows, :] + ga_ref[...] * (_rms(acc_ref[rows, :]) * g_ref[...])
            return carry

        lax.fori_loop(0, x_ref.shape[0] // ROW_CHUNK, body, 0)


def _ffn(cfg, h, x, w1, w2, modl, g_post):
    d, tm, th = cfg.d_model, cfg.tm_ff, cfg.th_ff
    return pl.pallas_call(
        _ffn_kernel,
        out_shape=jax.ShapeDtypeStruct((cfg.n_tok, d), F32),
        grid=(cfg.n_tok // tm, cfg.d_ff // th),
        in_specs=[pl.BlockSpec((tm, d), lambda i, k: (i, 0)),
                  pl.BlockSpec((tm, d), lambda i, k: (i, 0)),
                  pl.BlockSpec((d, th), lambda i, k: (0, k)),
                  pl.BlockSpec((th, d), lambda i, k: (k, 0)),
                  _mod_spec(cfg, 5, tm),
                  pl.BlockSpec((1, d), lambda i, k: (0, 0))],
        out_specs=pl.BlockSpec((tm, d), lambda i, k: (i, 0)),
        scratch_shapes=[pltpu.VMEM((tm, d), F32)],
        compiler_params=_cparams(("arbitrary", "arbitrary")),
        name="relu2_mlp",
    )(h, x, w1, w2, modl, g_post.reshape(1, d))


def _prep_w_in(cfg, w_in_l):
    mw, cv, nw, ng = cfg.m_width, cfg.cv_width, cfg.na_width, cfg.n_gates
    g0 = 4 * mw
    c0 = g0 + ng
    n0 = c0 + 2 * cv
    gates = w_in_l[:, g0:c0]
    parts = [w_in_l[:, :g0], w_in_l[:, c0:n0], w_in_l[:, n0 + nw:n0 + 3 * nw], w_in_l[:, n0:n0 + nw], gates]
    pad = cfg.in_cols - (cfg.col_gates + ng)
    w_p = jnp.concatenate(parts + [jnp.zeros((cfg.d_model, pad), w_in_l.dtype)], axis=1)
    return w_p.astype(BF16), gates.T.astype(BF16)


def _forward(cfg, x_prompt, x_sample, c, cache_k, cache_v, state_C, state_n, state_m, c_ctx, w_ada, b_ada,
             g_pre_mix, g_post_mix, g_pre_mlp, g_post_mlp, w_in, b_gate, g_mlstm, w_dw, b_dw, ln_g, ln_b,
             w_pw, rpb, w_out, w_ff1, w_ff2):
    d = cfg.d_model
    x = jnp.concatenate([x_prompt.reshape(cfg.n_ctx, d), x_sample.reshape(cfg.n_lat, d)], axis=0)

    n_cond = 1 + cfg.dec_batch
    cond_rows = -(-n_cond // 8) * 8
    cond = jnp.concatenate([c_ctx[None, :], c, jnp.zeros((cond_rows - n_cond, d), F32)], axis=0)
    mod = _modulation(cond, w_ada, b_ada, tn=min(1024, 6 * d)).reshape(cfg.depth, cond_rows, 6, 1, d)

    rope_tabs = _rope_tables(cfg)
    nw = cfg.na_width
    ks, vs, cs, ns, ms = [], [], [], [], []
    for l in range(cfg.depth):
        w_in_p, w_gt = _prep_w_in(cfg, w_in[l])
        p, gt, kv_new = _inproj(cfg, x, mod[l], g_pre_mix[l], w_in_p, w_gt, rope_tabs)

        bias_rows = jnp.broadcast_to(b_gate[l].reshape(cfg.n_gates, 1, 1), (cfg.n_gates, 1, M_HEAD_DIM))
        ym_c, c_new, n_new, m_new = _mlstm(cfg, p, gt, bias_rows, g_mlstm[l], n_seq=cfg.batch, seq_len=cfg.seq,
                                           row0=0, n_sub=cfg.ctx_sub, unroll=1, name="mlstm_context",
                                           emit_state=True)
        (ym_l,) = _mlstm(cfg, p, gt, bias_rows, g_mlstm[l], n_seq=cfg.dec_batch, seq_len=cfg.dec_seq,
                         row0=cfg.n_ctx, n_sub=1, unroll=cfg.lat_unroll, name="mlstm_latent",
                         state0=(state_C[:, l], state_n[:, l], state_m[:, l]))
        yc = _conv_module(cfg, p, w_dw[l], b_dw[l], ln_g[l], ln_b[l], w_pw[l].astype(BF16))
        yn_c = _ctx_attention(cfg, p)
        yn_l = _na_latent(cfg, p, cache_k[:, l].reshape(cfg.dec_batch, cfg.past_len, nw),
                          cache_v[:, l].reshape(cfg.dec_batch, cfg.past_len, nw),
                          _na_bias_table(rpb[l]))
        ym = jnp.concatenate([ym_c, ym_l], axis=0)
        yn = jnp.concatenate([yn_c, yn_l], axis=0)

        x, h2 = _outproj(cfg, ym, yc, yn, x, w_out[l].astype(BF16), mod[l], g_post_mix[l], g_pre_mlp[l])
        x = _ffn(cfg, h2, x, w_ff1[l].astype(BF16), w_ff2[l].astype(BF16), mod[l], g_post_mlp[l])

        ks.append(kv_new[:, :nw].reshape(cfg.batch, cfg.seq, cfg.na_heads, NA_HEAD_DIM))
        vs.append(kv_new[:, nw:].reshape(cfg.batch, cfg.seq, cfg.na_heads, NA_HEAD_DIM))
        cs.append(c_new)
        ns.append(n_new[:, :, :, 0, :])
        ms.append(m_new[:, :, :, 0, 0])

    y_prompt = x[:cfg.n_ctx].reshape(cfg.batch, cfg.seq, d)
    y_sample = x[cfg.n_ctx:].reshape(cfg.dec_batch, cfg.dec_seq, d)
    return (y_prompt, y_sample, jnp.stack(ks, axis=1), jnp.stack(vs, axis=1),
            jnp.stack(cs, axis=1), jnp.stack(ns, axis=1), jnp.stack(ms, axis=1))


def _make_cfg(x_prompt, x_sample, cache_k, w_in, w_ff1, w_pw, **tiles):
    batch, seq, d = x_prompt.shape
    dec_batch, dec_seq, _ = x_sample.shape
    return Cfg(d_model=d, batch=batch, seq=seq, depth=w_in.shape[0], dec_batch=dec_batch, dec_seq=dec_seq,
               past_len=cache_k.shape[2], m_heads=d // (2 * M_HEAD_DIM), cv_width=w_pw.shape[-1],
               na_heads=cache_k.shape[3], d_ff=w_ff1.shape[-1], **tiles)


def kernel(x_prompt, x_sample, c, cache_k, cache_v, state_C, state_n, state_m, c_ctx, w_ada, b_ada, g_pre_mix,
           g_post_mix, g_pre_mlp, g_post_mlp, w_in, b_gate, g_mlstm, w_dw, b_dw, ln_g, ln_b, w_pw, rpb, w_out,
           w_ff1, w_ff2):
    cfg = _make_cfg(x_prompt, x_sample, cache_k, w_in, w_ff1, w_pw,
                    tm_in=1024, tm_out=512, tm_ff=512, tn_in=1024, th_ff=1024, ctx_sub=4, lat_unroll=4,
                    conv_tt=256)
    return _forward(cfg, x_prompt, x_sample, c, cache_k, cache_v, state_C, state_n, state_m, c_ctx, w_ada, b_ada,
                    g_pre_mix, g_post_mix, g_pre_mlp, g_post_mlp, w_in, b_gate, g_mlstm, w_dw, b_dw, ln_g, ln_b,
                    w_pw, rpb, w_out, w_ff1, w_ff2)
```

```python
import functools
from typing import NamedTuple

import numpy as np
import jax
import jax.numpy as jnp
from jax import lax
from jax.experimental import pallas as pl
from jax.experimental.pallas import tpu as pltpu

F32 = jnp.float32
BF16 = jnp.bfloat16

EPS = 1e-6
NEG_INF = -1e30
ROPE_BASE = 10000.0

LANES = 128
SUBLANES = 8
M_HEAD_DIM = 128
NA_HEAD_DIM = 64
GRID_W = 64
WIN_ROWS = 8
WIN_COLS = 16
CONV_TAPS = 31
CONV_HALO = 16
VMEM_LIMIT = 52 * 1024 * 1024


class Cfg(NamedTuple):
    d_model: int
    batch: int
    seq: int
    depth: int
    dec_batch: int
    dec_seq: int
    past_len: int
    m_heads: int
    cv_width: int
    na_heads: int
    d_ff: int
    tm_in: int
    tm_out: int
    tm_ff: int
    tn_in: int
    th_ff: int
    ctx_sub: int
    lat_unroll: int
    conv_tt: int

    @property
    def m_width(self):
        return self.m_heads * M_HEAD_DIM

    @property
    def na_width(self):
        return self.na_heads * NA_HEAD_DIM

    @property
    def n_gates(self):
        return 4 * self.m_heads

    @property
    def n_ctx(self):
        return self.batch * self.seq

    @property
    def n_lat(self):
        return self.dec_batch * self.dec_seq

    @property
    def n_tok(self):
        return self.n_ctx + self.n_lat

    @property
    def col_ca(self):
        return 4 * self.m_width

    @property
    def col_kv(self):
        return self.col_ca + 2 * self.cv_width

    @property
    def col_qn(self):
        return self.col_kv + 2 * self.na_width

    @property
    def col_gates(self):
        return self.col_qn + self.na_width

    @property
    def in_cols(self):
        raw = self.col_gates + LANES
        return -(-raw // self.tn_in) * self.tn_in


def _cparams(sem):
    return pltpu.CompilerParams(dimension_semantics=sem, vmem_limit_bytes=VMEM_LIMIT)


def _rms(x):
    return x * lax.rsqrt(jnp.mean(x * x, axis=-1, keepdims=True) + EPS)


def _dot(a, b):
    return jnp.dot(a, b, preferred_element_type=F32)


def _dot_nt(a, b):
    return lax.dot_general(a, b, (((1,), (1,)), ((), ())), preferred_element_type=F32)


def _dot_tn(a, b):
    return lax.dot_general(a, b, (((0,), (0,)), ((), ())), preferred_element_type=F32)


def _mod_kernel(c_ref, w_ref, b_ref, o_ref):
    c = c_ref[...]
    s = (c * jax.nn.sigmoid(c)).astype(BF16)
    o_ref[...] = _dot(s, w_ref[...].astype(BF16)) + b_ref[...]


def _modulation(cond, w_ada, b_ada, tn):
    depth, d, n6 = w_ada.shape
    rows = cond.shape[0]
    return pl.pallas_call(
        _mod_kernel,
        out_shape=jax.ShapeDtypeStruct((depth, rows, n6), F32),
        grid=(depth, n6 // tn),
        in_specs=[
            pl.BlockSpec((rows, d), lambda l, j: (0, 0)),
            pl.BlockSpec((None, d, tn), lambda l, j: (l, 0, j)),
            pl.BlockSpec((None, 1, tn), lambda l, j: (l, 0, j)),
        ],
        out_specs=pl.BlockSpec((None, rows, tn), lambda l, j: (l, 0, j)),
        compiler_params=_cparams(("arbitrary", "arbitrary")),
        name="adaln_modulation",
    )(cond, w_ada, b_ada.reshape(depth, 1, n6))


def _mod_row_map(cfg, which, tm):
    assert cfg.n_ctx % tm == 0 and cfg.dec_seq % tm == 0
    nct = cfg.n_ctx // tm
    tpb = cfg.dec_seq // tm

    def index_map(i, *_):
        row = jnp.where(i < nct, 0, 1 + (i - nct) // tpb)
        return (row, which, 0, 0)

    return index_map


def _mod_spec(cfg, which, tm):
    return pl.BlockSpec((None, None, 1, cfg.d_model), _mod_row_map(cfg, which, tm))


ROW_CHUNK = 128


def _rope(x, cos, sin_signed, even_block):
    swapped = jnp.where(even_block, pltpu.roll(x, 96, 1), pltpu.roll(x, 32, 1))
    return x * cos + swapped * sin_signed


IN_ROWS = 512


def _inproj_kernel(x_ref, sh_ref, sc_ref, g_ref, w_ref, wgt_ref, cos_ref, sin_ref, o_ref, gt_ref, kv_ref,
                   h_scr, acc_scr, *, q_tiles, k_tiles, kv_tile, n_ctx_tiles):
    i = pl.program_id(0)
    j = pl.program_id(1)
    tm, tn = o_ref.shape

    def norm_rows(rows):
        h = _rms(x_ref[rows, :]) * g_ref[...] * (1.0 + sc_ref[...]) + sh_ref[...]
        h_scr[rows, :] = h.astype(BF16)

    def matmul_rope(rows):
        acc_scr[rows, :] = _dot(h_scr[rows, :], w_ref[...])
        scale = jnp.where(j >= q_tiles, M_HEAD_DIM ** -0.5, 1.0).astype(F32)
        lane_blk = lax.broadcasted_iota(jnp.int32, (ROW_CHUNK, M_HEAD_DIM), 1) // 32
        even_block = (lane_blk % 2) == 0
        for r in range(rows.start, rows.stop, ROW_CHUNK):
            chunk = slice(r, r + ROW_CHUNK)
            cos, sin = cos_ref[chunk, :], sin_ref[chunk, :]
            for hh in range(tn // M_HEAD_DIM):
                cols = slice(hh * M_HEAD_DIM, (hh + 1) * M_HEAD_DIM)
                o_ref[chunk, cols] = (_rope(acc_scr[chunk, cols], cos, sin, even_block) * scale).astype(o_ref.dtype)

    @pl.when(j == 0)
    def _():
        blk_rows = min(IN_ROWS, tm)
        for blk in range(tm // blk_rows):
            for r in range(blk * blk_rows, (blk + 1) * blk_rows, ROW_CHUNK):
                norm_rows(slice(r, r + ROW_CHUNK))
            matmul_rope(slice(blk * blk_rows, (blk + 1) * blk_rows))
        gt_ref[...] = _dot_nt(wgt_ref[...], h_scr[...])

    @pl.when((j > 0) & (j < q_tiles + k_tiles))
    def _():
        matmul_rope(slice(0, tm))

    @pl.when((j >= q_tiles + k_tiles) & (j != kv_tile))
    def _():
        o_ref[...] = _dot(h_scr[...], w_ref[...]).astype(o_ref.dtype)

    @pl.when(j == kv_tile)
    def _():
        acc_scr[...] = _dot(h_scr[...], w_ref[...])
        o_ref[...] = acc_scr[...].astype(o_ref.dtype)

        @pl.when(i < n_ctx_tiles)
        def _():
            kv_ref[...] = acc_scr[...]


def _inproj(cfg, x, modl, g, w_in_p, w_gt, rope_tabs):
    d, tm, tn = cfg.d_model, cfg.tm_in, cfg.tn_in
    ng = cfg.n_gates
    nct = cfg.n_ctx // tm
    assert cfg.m_width % tn == 0 and cfg.col_kv % tn == 0 and 2 * cfg.na_width == tn and nct * tm == cfg.n_ctx
    kern = functools.partial(_inproj_kernel, q_tiles=cfg.m_width // tn, k_tiles=cfg.m_width // tn,
                             kv_tile=cfg.col_kv // tn, n_ctx_tiles=nct)
    return pl.pallas_call(
        kern,
        out_shape=(jax.ShapeDtypeStruct((cfg.n_tok, cfg.in_cols), BF16),
                   jax.ShapeDtypeStruct((ng, cfg.n_tok), F32),
                   jax.ShapeDtypeStruct((cfg.n_ctx, tn), F32)),
        grid=(cfg.n_tok // tm, cfg.in_cols // tn),
        in_specs=[
            pl.BlockSpec((tm, d), lambda i, j: (i, 0)),
            _mod_spec(cfg, 0, tm),
            _mod_spec(cfg, 1, tm),
            pl.BlockSpec((1, d), lambda i, j: (0, 0)),
            pl.BlockSpec((d, tn), lambda i, j: (0, j)),
            pl.BlockSpec((ng, d), lambda i, j: (0, 0)),
            pl.BlockSpec((tm, M_HEAD_DIM), lambda i, j: (i, 0)),
            pl.BlockSpec((tm, M_HEAD_DIM), lambda i, j: (i, 0)),
        ],
        out_specs=(pl.BlockSpec((tm, tn), lambda i, j: (i, j)),
                   pl.BlockSpec((ng, tm), lambda i, j: (0, i)),
                   pl.BlockSpec((tm, tn), lambda i, j: (jnp.minimum(i, nct - 1), 0))),
        scratch_shapes=[pltpu.VMEM((tm, d), BF16), pltpu.VMEM((tm, tn), F32)],
        compiler_params=_cparams(("arbitrary", "arbitrary")),
        name="in_projection",
    )(x, modl, modl, g.reshape(1, d), w_in_p, w_gt, *rope_tabs)


ROW_A, ROW_M, ROW_E, ROW_WK, ROW_MPREV, ROW_DECAY = range(6)
LOG2_E = 1.4426950408889634
FINISH_ROWS = 512


def _log_sigmoid(x):
    return jnp.minimum(x, 0.0) - jnp.log1p(jnp.exp(-jnp.abs(x)))


def _cumsum_lanes(x, cum_mat):
    hi = x.astype(BF16)
    r1 = x - hi.astype(F32)
    mid = r1.astype(BF16)
    lo = (r1 - mid.astype(F32)).astype(BF16)
    return _dot(hi, cum_mat) + _dot(mid, cum_mat) + _dot(lo, cum_mat)


def _cummax_lanes(x, reverse):
    n = x.shape[1]
    lane = lax.broadcasted_iota(jnp.int32, x.shape, 1)
    k = 1
    while k < n:
        if reverse:
            shifted, ok = pltpu.roll(x, n - k, 1), lane < n - k
        else:
            shifted, ok = pltpu.roll(x, k, 1), lane >= k
        x = jnp.maximum(x, jnp.where(ok, shifted, -jnp.inf))
        k *= 2
    return x


def _col_bcast(row, n):
    return jnp.transpose(jnp.broadcast_to(row, (n, row.shape[1])))


def _mlstm_kernel(*refs, seq_len, n_sub, heads, unroll, init_state, emit_state):
    it = iter(refs)
    q_ref, k_ref, v_ref, om_ref, gt_ref, bias_ref, gm_ref = (next(it) for _ in range(7))
    c0_ref = n0_ref = m0_ref = None
    if init_state:
        c0_ref, n0_ref, m0_ref = next(it), next(it), next(it)
    o_ref = next(it)
    co_ref = no_ref = mo_ref = None
    if emit_state:
        co_ref, no_ref, mo_ref = next(it), next(it), next(it)
    hs_f, hs_b, cext, rows_scr = next(it), next(it), next(it), next(it)

    L = dk = M_HEAD_DIM
    head = pl.program_id(1)
    nc = seq_len // L

    tt = lax.broadcasted_iota(jnp.int32, (L, L), 0)
    ss = lax.broadcasted_iota(jnp.int32, (L, L), 1)
    cum_mats = ((tt <= ss).astype(BF16), (tt >= ss).astype(BF16))
    causal = (ss <= tt, ss >= tt)

    for d in range(2):
        row_i = d * 2 * heads + head
        xi = gt_ref[row_i] + bias_ref[row_i]
        lf = _log_sigmoid(gt_ref[row_i + heads] + bias_ref[row_i + heads])
        b = _cumsum_lanes(lf, cum_mats[d])
        tot = jnp.sum(lf, axis=1, keepdims=True)
        a = xi - b
        g = tot - b + xi
        gmax = jnp.max(g, axis=1, keepdims=True)
        arun = _cummax_lanes(a, reverse=(d == 1))
        a_l2 = a * LOG2_E
        for s in range(n_sub):
            m = m0_ref[s, d][:, 0:1] if init_state else jnp.zeros((1, 1), F32)
            for c in (range(nc) if d == 0 else reversed(range(nc))):
                r = s * nc + c
                m_new = jnp.maximum(tot[r:r + 1] + m, gmax[r:r + 1])
                m_row = jnp.maximum(arun[r:r + 1], m)
                rows_scr[d, r, ROW_A:ROW_A + 1, :] = a_l2[r:r + 1]
                rows_scr[d, r, ROW_M:ROW_M + 1, :] = m_row * LOG2_E
                rows_scr[d, r, ROW_E:ROW_E + 1, :] = -(b[r:r + 1] + m_row) * LOG2_E
                rows_scr[d, r, ROW_WK:ROW_WK + 1, :] = jnp.exp(g[r:r + 1] - m_new)
                rows_scr[d, r, ROW_MPREV:ROW_MPREV + 1, :] = jnp.broadcast_to(m * LOG2_E, (1, L))
                rows_scr[d, r, ROW_DECAY:ROW_DECAY + 1, :] = jnp.broadcast_to(jnp.exp(tot[r:r + 1] + m - m_new), (1, L))
                m = m_new
            if emit_state:
                mo_ref[s, d] = jnp.broadcast_to(m, (1, LANES))
            if init_state:
                cext[s, d, :, 0:dk] = c0_ref[s, d]
                cext[s, d, :, dk:2 * dk] = _col_bcast(n0_ref[s, d], dk)
            else:
                cext[s, d] = jnp.zeros((dk, 2 * dk), F32)

    ones_ext = jnp.ones((L, dk), BF16)

    def prepare(s, d, c):
        rows = pl.ds(pl.multiple_of(s * seq_len + c * L, L), L)
        tile = rows_scr[d, s * nc + c]
        q = q_ref[rows, :].astype(BF16)
        k = k_ref[rows, :]
        v_ext = jnp.concatenate([v_ref[rows, :].astype(BF16), ones_ext], axis=1)
        m_b = _col_bcast(tile[ROW_M:ROW_M + 1], dk)
        floor = jnp.exp2(_col_bcast(tile[ROW_E:ROW_E + 1], dk))
        dmat = jnp.exp2(jnp.where(causal[d], tile[ROW_A:ROW_A + 1] - m_b, -jnp.inf))
        sc = (_dot_nt(q, k.astype(BF16)) * dmat).astype(BF16)
        w_int = jnp.exp2(tile[ROW_MPREV:ROW_MPREV + 1] - m_b)
        lhs = jnp.concatenate([sc, (w_int * q.astype(F32)).astype(BF16)], axis=1)
        kt_w = (jnp.transpose(k.astype(F32)) * tile[ROW_WK:ROW_WK + 1]).astype(BF16)
        return rows, lhs, v_ext, floor, kt_w, tile[ROW_DECAY:ROW_DECAY + 1, 0:1]

    def apply(s, d, prepared):
        rows, lhs, v_ext, floor, kt_w, decay = prepared
        c_prev = cext[s, d]
        nd = _dot(lhs, jnp.concatenate([v_ext, c_prev.astype(BF16)], axis=0))
        den = jnp.maximum(jnp.abs(nd[:, dk:2 * dk]), floor)
        (hs_f if d == 0 else hs_b)[rows, :] = nd[:, 0:dk] / den
        cext[s, d] = decay * c_prev + _dot(kt_w, v_ext)

    def chunk_body(i, carry):
        todo = []
        for u in range(unroll):
            c = i * unroll + u
            for s in range(n_sub):
                todo.append((s, 0, prepare(s, 0, c)))
                todo.append((s, 1, prepare(s, 1, nc - 1 - c)))
        for s, d, prepared in todo:
            apply(s, d, prepared)
        return carry

    lax.fori_loop(0, nc // unroll, chunk_body, 0)

    def finish(r, carry):
        rows = pl.ds(pl.multiple_of(r * FINISH_ROWS, FINISH_ROWS), FINISH_ROWS)
        y = _rms(hs_f[rows, :] + hs_b[rows, :]) * gm_ref[...] * jax.nn.sigmoid(om_ref[rows, :].astype(F32))
        o_ref[rows, :] = y.astype(o_ref.dtype)
        return carry

    lax.fori_loop(0, n_sub * seq_len // FINISH_ROWS, finish, 0)

    if emit_state:
        for s in range(n_sub):
            for d in range(2):
                co_ref[s, d] = cext[s, d, :, 0:dk]
                no_ref[s, d] = jnp.transpose(cext[s, d, :, dk:2 * dk])[0:1, :]


def _mlstm(cfg, p, gt, bias_rows, g_mlstm, *, n_seq, seq_len, row0, n_sub, unroll, name, state0=None,
           emit_state=False):
    heads, dk = cfg.m_heads, M_HEAD_DIM
    L = dk
    nc = seq_len // L
    blk_rows = n_sub * seq_len
    sb = row0 // blk_rows
    assert sb * blk_rows == row0 and nc * L == seq_len and n_seq % n_sub == 0 and nc % unroll == 0
    assert (n_sub * nc) % 8 == 0
    n_steps = n_seq // n_sub

    def col(block0):
        return pl.BlockSpec((blk_rows, dk), lambda b, h: (sb + b, block0 + h))

    gt4 = gt[:, row0:row0 + n_seq * seq_len].reshape(cfg.n_gates, n_steps, n_sub * nc, L)
    in_specs = [col(0), col(heads), col(2 * heads), col(3 * heads),
                pl.BlockSpec((cfg.n_gates, None, n_sub * nc, L), lambda b, h: (0, b, 0, 0)),
                pl.BlockSpec((cfg.n_gates, 1, L), lambda b, h: (0, 0, 0)),
                pl.BlockSpec((None, 1, dk), lambda b, h: (h, 0, 0))]
    args = [p, p, p, p, gt4, bias_rows, g_mlstm.reshape(heads, 1, dk)]
    state_spec = lambda r, c: pl.BlockSpec((n_sub, 2, None, r, c), lambda b, h: (b, 0, h, 0, 0))
    if state0 is not None:
        c0, n0, m0 = state0
        in_specs += [state_spec(dk, dk), state_spec(1, dk), state_spec(1, LANES)]
        args += [c0, n0.reshape(n_seq, 2, heads, 1, dk),
                 jnp.broadcast_to(m0[..., None, None], (n_seq, 2, heads, 1, LANES))]
    out_shape = [jax.ShapeDtypeStruct((n_seq * seq_len, cfg.m_width), BF16)]
    out_specs = [pl.BlockSpec((blk_rows, dk), lambda b, h: (b, h))]
    if emit_state:
        out_shape += [jax.ShapeDtypeStruct((n_seq, 2, heads, dk, dk), F32),
                      jax.ShapeDtypeStruct((n_seq, 2, heads, 1, dk), F32),
                      jax.ShapeDtypeStruct((n_seq, 2, heads, 1, LANES), F32)]
        out_specs += [state_spec(dk, dk), state_spec(1, dk), state_spec(1, LANES)]
    kern = functools.partial(_mlstm_kernel, seq_len=seq_len, n_sub=n_sub, heads=heads, unroll=unroll,
                             init_state=state0 is not None, emit_state=emit_state)
    return pl.pallas_call(
        kern,
        out_shape=tuple(out_shape),
        grid=(n_steps, heads),
        in_specs=in_specs,
        out_specs=tuple(out_specs),
        scratch_shapes=[pltpu.VMEM((blk_rows, dk), F32), pltpu.VMEM((blk_rows, dk), F32),
                        pltpu.VMEM((n_sub, 2, dk, 2 * dk), F32), pltpu.VMEM((2, n_sub * nc, 8, L), F32)],
        compiler_params=_cparams(("arbitrary", "arbitrary")),
        name=name,
    )(*args)


def _rope_tables(cfg):
    half = M_HEAD_DIM // 2
    nf = half // 2
    t = jnp.arange(cfg.dec_seq)
    inv = ROPE_BASE ** (-jnp.arange(nf, dtype=F32) / nf)
    ang_r = (t // GRID_W).astype(F32)[:, None] * inv[None, :]
    ang_c = (t % GRID_W).astype(F32)[:, None] * inv[None, :]
    cos = jnp.concatenate([jnp.cos(ang_r)] * 2 + [jnp.cos(ang_c)] * 2, axis=-1)
    sin = jnp.concatenate([-jnp.sin(ang_r), jnp.sin(ang_r), -jnp.sin(ang_c), jnp.sin(ang_c)], axis=-1)
    cos = jnp.concatenate([jnp.ones((cfg.n_ctx, M_HEAD_DIM), F32), jnp.tile(cos, (cfg.dec_batch, 1))], axis=0)
    sin = jnp.concatenate([jnp.zeros((cfg.n_ctx, M_HEAD_DIM), F32), jnp.tile(sin, (cfg.dec_batch, 1))], axis=0)
    return cos, sin


CONV_ROWS = 32


def _conv_kernel(a_ref, g_ref, ap_ref, gp_ref, an_ref, gn_ref, wdw_ref, bdw_ref, lng_ref, lnb_ref, wpw_ref,
                 o_ref, u_scr, *, n_ctx_tiles, ctx_tiles_per_seq, lat_tiles_per_seq):
    i = pl.program_id(0)
    tt = a_ref.shape[0]
    halo = CONV_HALO
    pos = jnp.where(i < n_ctx_tiles, i % ctx_tiles_per_seq, (i - n_ctx_tiles) % lat_tiles_per_seq)
    per_seq = jnp.where(i < n_ctx_tiles, ctx_tiles_per_seq, lat_tiles_per_seq)
    keep_prev = (pos > 0).astype(F32)
    keep_next = (pos < per_seq - 1).astype(F32)

    glu = lambda a, g: a[...].astype(F32) * jax.nn.sigmoid(g[...].astype(F32))
    u_scr[0, 0:halo, :] = glu(ap_ref, gp_ref) * keep_prev
    u_scr[0, halo:halo + tt, :] = glu(a_ref, g_ref)
    u_scr[0, halo + tt:2 * halo + tt, :] = glu(an_ref, gn_ref) * keep_next
    n_rows = tt + 2 * halo
    window = u_scr[0]
    for r in range(1, SUBLANES):
        u_scr[r] = pltpu.roll(window, n_rows - r, 0)

    for blk in range(tt // CONV_ROWS):
        base = blk * CONV_ROWS
        acc = jnp.broadcast_to(bdw_ref[...], (CONV_ROWS, a_ref.shape[1]))
        for tap in range(CONV_TAPS):
            start = base + halo - CONV_TAPS // 2 + tap
            r, aligned = start % SUBLANES, start - start % SUBLANES
            w_tap = jnp.concatenate([wdw_ref[tap]] * (CONV_ROWS // SUBLANES), axis=0)
            acc = acc + u_scr[r, aligned:aligned + CONV_ROWS, :] * w_tap
        mu = jnp.mean(acc, axis=-1, keepdims=True)
        cen = acc - mu
        var = jnp.mean(cen * cen, axis=-1, keepdims=True)
        y = cen * lax.rsqrt(var + EPS) * lng_ref[...] + lnb_ref[...]
        y = y * jax.nn.sigmoid(y)
        o_ref[base:base + CONV_ROWS, :] = _dot(y.astype(BF16), wpw_ref[...]).astype(o_ref.dtype)


def _conv_module(cfg, p, w_dw, b_dw, ln_g, ln_b, w_pw):
    tt, cv, halo = cfg.conv_tt, cfg.cv_width, CONV_HALO
    n_tiles = cfg.n_tok // tt
    cb = cfg.col_ca // cv
    hb = tt // halo
    last_halo = cfg.n_tok // halo - 1
    main = lambda c: pl.BlockSpec((tt, cv), lambda i: (i, c))
    prev = lambda c: pl.BlockSpec((halo, cv), lambda i: (jnp.maximum(i * hb - 1, 0), c))
    nxt = lambda c: pl.BlockSpec((halo, cv), lambda i: (jnp.minimum((i + 1) * hb, last_halo), c))
    vec = lambda: pl.BlockSpec((1, cv), lambda i: (0, 0))
    kern = functools.partial(_conv_kernel, n_ctx_tiles=cfg.n_ctx // tt, ctx_tiles_per_seq=cfg.seq // tt,
                             lat_tiles_per_seq=cfg.dec_seq // tt)
    return pl.pallas_call(
        kern,
        out_shape=jax.ShapeDtypeStruct((cfg.n_tok, cv), BF16),
        grid=(n_tiles,),
        in_specs=[main(cb), main(cb + 1), prev(cb), prev(cb + 1), nxt(cb), nxt(cb + 1),
                  pl.BlockSpec((CONV_TAPS, SUBLANES, cv), lambda i: (0, 0, 0)), vec(), vec(), vec(),
                  pl.BlockSpec((cv, cv), lambda i: (0, 0))],
        out_specs=pl.BlockSpec((tt, cv), lambda i: (i, 0)),
        scratch_shapes=[pltpu.VMEM((SUBLANES, tt + 2 * halo, cv), F32)],
        compiler_params=_cparams(("arbitrary",)),
        name="conv_module",
    )(p, p, p, p, p, p, jnp.broadcast_to(w_dw[:, None, :], (CONV_TAPS, SUBLANES, cv)),
      b_dw.reshape(1, cv), ln_g.reshape(1, cv), ln_b.reshape(1, cv), w_pw)


def _softmax_pv(scores, values):
    mx = functools.reduce(jnp.maximum, [jnp.max(s, axis=1, keepdims=True) for s in scores])
    ps = [jnp.exp(s - mx) for s in scores]
    den = functools.reduce(jnp.add, [jnp.sum(p, axis=1, keepdims=True) for p in ps])
    num = functools.reduce(jnp.add, [_dot(p.astype(BF16), v) for p, v in zip(ps, values)])
    return num / den


def _head_pair(q2, scores_fn, values):
    first_half = lax.broadcasted_iota(jnp.int32, q2.shape, 1) < NA_HEAD_DIM
    zero = jnp.zeros_like(q2)
    out_even = _softmax_pv(scores_fn(jnp.where(first_half, q2, zero), 0), values)
    out_odd = _softmax_pv(scores_fn(jnp.where(first_half, zero, q2), 1), values)
    return jnp.where(first_half, out_even, out_odd)


def _ctx_attn_kernel(q_ref, k_ref, v_ref, o_ref, *, heads):
    dh = NA_HEAD_DIM
    q = (q_ref[...] * (dh ** -0.5)).astype(BF16)
    k = k_ref[...].astype(BF16)
    v = v_ref[...].astype(BF16)
    outs = []
    for hp in range(heads // 2):
        sl = slice(hp * LANES, (hp + 1) * LANES)
        outs.append(_head_pair(q[:, sl], lambda qh, half: [_dot_nt(qh, k[:, sl])], [v[:, sl]]))
    o_ref[...] = jnp.concatenate(outs, axis=1).astype(o_ref.dtype)


def _ctx_attention(cfg, p):
    s, nw = cfg.seq, cfg.na_width
    qb, kb = cfg.col_qn // nw, cfg.col_kv // nw
    spec = lambda c: pl.BlockSpec((s, nw), lambda b: (b, c))
    return pl.pallas_call(
        functools.partial(_ctx_attn_kernel, heads=cfg.na_heads),
        out_shape=jax.ShapeDtypeStruct((cfg.n_ctx, nw), BF16),
        grid=(cfg.batch,),
        in_specs=[spec(qb), spec(kb), spec(kb + 1)],
        out_specs=pl.BlockSpec((s, nw), lambda b: (b, 0)),
        compiler_params=_cparams(("arbitrary",)),
        name="context_attention",
    )(p, p, p)


NA_QROWS = 8


def _na_kernel(q_ref, kp_ref, kc_ref, kn_ref, vp_ref, vc_ref, vn_ref, kx_ref, vx_ref, bt_ref, o_ref,
               kw_scr, vw_scr, *, heads, grid_rows):
    dh = NA_HEAD_DIM
    blk = NA_QROWS * GRID_W
    win = WIN_ROWS * GRID_W
    m = pl.program_id(1)
    for slot, (kr, vr) in enumerate(((kp_ref, vp_ref), (kc_ref, vc_ref), (kn_ref, vn_ref))):
        kw_scr[slot * blk:(slot + 1) * blk, :] = kr[...].astype(BF16)
        vw_scr[slot * blk:(slot + 1) * blk, :] = vr[...].astype(BF16)
    kx = kx_ref[...].astype(BF16)
    vx = vx_ref[...].astype(BF16)

    def row_body(i, carry):
        r = m * NA_QROWS + i
        rs = jnp.clip(r - WIN_ROWS // 2, 0, grid_rows - WIN_ROWS)
        start = rs - r + (WIN_ROWS - 1)
        off = pl.multiple_of((rs - NA_QROWS * (m - 1)) * GRID_W, GRID_W)
        qrows = pl.ds(pl.multiple_of(i * GRID_W, GRID_W), GRID_W)
        q = (q_ref[qrows, :] * (dh ** -0.5)).astype(BF16)
        kwin = kw_scr[pl.ds(off, win), :]
        vwin = vw_scr[pl.ds(off, win), :]
        first_half = lax.broadcasted_iota(jnp.int32, (GRID_W, LANES), 1) < dh
        pairs = [slice(hp * LANES, (hp + 1) * LANES) for hp in range(heads // 2)]
        scores = []
        for hp, sl in enumerate(pairs):
            q2 = q[:, sl]
            zero = jnp.zeros_like(q2)
            qm = jnp.concatenate([jnp.where(first_half, q2, zero), jnp.where(first_half, zero, q2)], axis=0)
            scores.append((_dot_nt(kwin[:, sl], qm) + bt_ref[start, hp],
                           _dot_nt(kx[:, sl], qm)))
        probs = []
        for s_loc, s_ctx in scores:
            mx = jnp.maximum(jnp.max(s_loc, axis=0, keepdims=True), jnp.max(s_ctx, axis=0, keepdims=True))
            p_loc = jnp.exp(s_loc - mx)
            p_ctx = jnp.exp(s_ctx - mx)
            inv = 1.0 / (jnp.sum(p_loc, axis=0, keepdims=True) + jnp.sum(p_ctx, axis=0, keepdims=True))
            probs.append(((p_loc * inv).astype(BF16), (p_ctx * inv).astype(BF16)))
        outs = []
        for (p_loc, p_ctx), sl in zip(probs, pairs):
            pv = _dot_tn(p_loc, vwin[:, sl]) + _dot_tn(p_ctx, vx[:, sl])
            outs.append(jnp.where(first_half, pv[0:GRID_W], pv[GRID_W:2 * GRID_W]))
        o_ref[qrows, :] = jnp.concatenate(outs, axis=1).astype(o_ref.dtype)
        return carry

    lax.fori_loop(0, NA_QROWS, row_body, 0)


def _na_bias_table(rpb):
    heads = rpb.shape[0]
    qc = np.arange(GRID_W)[:, None]
    kc = np.arange(GRID_W)[None, :]
    cs = np.clip(qc - WIN_COLS // 2, 0, GRID_W - WIN_COLS)
    valid = (kc >= cs) & (kc < cs + WIN_COLS)
    dc = np.clip(kc - qc + WIN_COLS - 1, 0, 2 * WIN_COLS - 2)
    dr = np.arange(WIN_ROWS)[:, None] + np.arange(WIN_ROWS)[None, :]
    tab = rpb[:, dr][:, :, :, dc]
    tab = jnp.where(jnp.asarray(valid)[None, None, None], tab, NEG_INF)
    tab = tab.reshape(heads // 2, 2, WIN_ROWS, WIN_ROWS, GRID_W, GRID_W)
    tab = tab.transpose(2, 0, 3, 5, 1, 4)
    return tab.reshape(WIN_ROWS, heads // 2, WIN_ROWS * GRID_W, 2 * GRID_W)


def _na_latent(cfg, p, k_ctx, v_ctx, bias_tab):
    nw, heads = cfg.na_width, cfg.na_heads
    blk = NA_QROWS * GRID_W
    grid_rows = cfg.dec_seq // GRID_W
    nblk = grid_rows // NA_QROWS
    qb, kb = cfg.col_qn // nw, cfg.col_kv // nw
    rb0 = cfg.n_ctx // blk
    assert rb0 * blk == cfg.n_ctx and nblk * NA_QROWS == grid_rows

    def rows(shift, c):
        return pl.BlockSpec((blk, nw), lambda b, m: (rb0 + b * nblk + jnp.clip(m + shift, 0, nblk - 1), c))

    ctx_spec = pl.BlockSpec((None, cfg.past_len, nw), lambda b, m: (b, 0, 0))
    return pl.pallas_call(
        functools.partial(_na_kernel, heads=heads, grid_rows=grid_rows),
        out_shape=jax.ShapeDtypeStruct((cfg.n_lat, nw), BF16),
        grid=(cfg.dec_batch, nblk),
        in_specs=[rows(0, qb), rows(-1, kb), rows(0, kb), rows(1, kb),
                  rows(-1, kb + 1), rows(0, kb + 1), rows(1, kb + 1), ctx_spec, ctx_spec,
                  pl.BlockSpec((WIN_ROWS, heads // 2, WIN_ROWS * GRID_W, 2 * GRID_W), lambda b, m: (0, 0, 0, 0))],
        out_specs=pl.BlockSpec((blk, nw), lambda b, m: (b * nblk + m, 0)),
        scratch_shapes=[pltpu.VMEM((3 * blk, nw), BF16), pltpu.VMEM((3 * blk, nw), BF16)],
        compiler_params=_cparams(("arbitrary", "arbitrary")),
        name="neighbourhood_attention",
    )(p, p, p, p, p, p, p, k_ctx, v_ctx, bias_tab)


OUT_ROWS = 256


def _outproj_kernel(ym_ref, yc_ref, yn_ref, x_ref, w_ref, ga_ref, gpost_ref, gpre_ref, sc_ref, sh_ref,
                    xo_ref, ho_ref, y_scr, *, m_width, cv_width):
    a, b = m_width, m_width + cv_width
    for blk in range(x_ref.shape[0] // OUT_ROWS):
        blk_rows = slice(blk * OUT_ROWS, (blk + 1) * OUT_ROWS)
        y_scr[blk_rows, :] = (_dot(ym_ref[blk_rows, :], w_ref[0:a, :]) + _dot(yc_ref[blk_rows, :], w_ref[a:b, :])
                              + _dot(yn_ref[blk_rows, :], w_ref[b:, :]))
        for r in range(OUT_ROWS // ROW_CHUNK):
            rows = slice(blk * OUT_ROWS + r * ROW_CHUNK, blk * OUT_ROWS + (r + 1) * ROW_CHUNK)
            x = x_ref[rows, :] + ga_ref[...] * (_rms(y_scr[rows, :]) * gpost_ref[...])
            xo_ref[rows, :] = x
            ho_ref[rows, :] = (_rms(x) * gpre_ref[...] * (1.0 + sc_ref[...]) + sh_ref[...]).astype(BF16)


def _outproj(cfg, ym, yc, yn, x, w_out, modl, g_post, g_pre):
    d, tm = cfg.d_model, cfg.tm_out
    rowblk = lambda w: pl.BlockSpec((tm, w), lambda i: (i, 0))
    vec = lambda: pl.BlockSpec((1, d), lambda i: (0, 0))
    return pl.pallas_call(
        functools.partial(_outproj_kernel, m_width=cfg.m_width, cv_width=cfg.cv_width),
        out_shape=(jax.ShapeDtypeStruct((cfg.n_tok, d), F32), jax.ShapeDtypeStruct((cfg.n_tok, d), BF16)),
        grid=(cfg.n_tok // tm,),
        in_specs=[rowblk(cfg.m_width), rowblk(cfg.cv_width), rowblk(cfg.na_width), rowblk(d),
                  pl.BlockSpec((d, d), lambda i: (0, 0)),
                  _mod_spec(cfg, 2, tm), vec(), vec(), _mod_spec(cfg, 4, tm), _mod_spec(cfg, 3, tm)],
        out_specs=(rowblk(d), rowblk(d)),
        scratch_shapes=[pltpu.VMEM((tm, d), F32)],
        compiler_params=_cparams(("arbitrary",)),
        name="out_projection",
    )(ym, yc, yn, x, w_out, modl, g_post.reshape(1, d), g_pre.reshape(1, d), modl, modl)


def _ffn_kernel(h_ref, x_ref, w1_ref, w2_ref, ga_ref, g_ref, o_ref, acc_ref):
    kk = pl.program_id(1)

    @pl.when(kk == 0)
    def _():
        acc_ref[...] = jnp.zeros_like(acc_ref)

    a = jnp.maximum(_dot(h_ref[...], w1_ref[...]), 0.0)
    acc_ref[...] += _dot((a * a).astype(BF16), w2_ref[...])

    @pl.when(kk == pl.num_programs(1) - 1)
    def _():
        def body(r, carry):
            rows = pl.ds(pl.multiple_of(r * ROW_CHUNK, ROW_CHUNK), ROW_CHUNK)
            o_ref[rows, :] = x_ref[rows, :] + ga_ref[...] * (_rms(acc_ref[rows, :]) * g_ref[...])
            return carry

        lax.fori_loop(0, x_ref.shape[0] // ROW_CHUNK, body, 0)


def _ffn(cfg, h, x, w1, w2, modl, g_post):
    d, tm, th = cfg.d_model, cfg.tm_ff, cfg.th_ff
    return pl.pallas_call(
        _ffn_kernel,
        out_shape=jax.ShapeDtypeStruct((cfg.n_tok, d), F32),
        grid=(cfg.n_tok // tm, cfg.d_ff // th),
        in_specs=[pl.BlockSpec((tm, d), lambda i, k: (i, 0)),
                  pl.BlockSpec((tm, d), lambda i, k: (i, 0)),
                  pl.BlockSpec((d, th), lambda i, k: (0, k)),
                  pl.BlockSpec((th, d), lambda i, k: (k, 0)),
                  _mod_spec(cfg, 5, tm),
                  pl.BlockSpec((1, d), lambda i, k: (0, 0))],
        out_specs=pl.BlockSpec((tm, d), lambda i, k: (i, 0)),
        scratch_shapes=[pltpu.VMEM((tm, d), F32)],
        compiler_params=_cparams(("arbitrary", "arbitrary")),
        name="relu2_mlp",
    )(h, x, w1, w2, modl, g_post.reshape(1, d))


def _prep_w_in(cfg, w_in_l):
    mw, cv, nw, ng = cfg.m_width, cfg.cv_width, cfg.na_width, cfg.n_gates
    g0 = 4 * mw
    c0 = g0 + ng
    n0 = c0 + 2 * cv
    gates = w_in_l[:, g0:c0]
    parts = [w_in_l[:, :g0], w_in_l[:, c0:n0], w_in_l[:, n0 + nw:n0 + 3 * nw], w_in_l[:, n0:n0 + nw], gates]
    pad = cfg.in_cols - (cfg.col_gates + ng)
    w_p = jnp.concatenate(parts + [jnp.zeros((cfg.d_model, pad), w_in_l.dtype)], axis=1)
    return w_p.astype(BF16), gates.T.astype(BF16)


def _forward(cfg, x_prompt, x_sample, c, cache_k, cache_v, state_C, state_n, state_m, c_ctx, w_ada, b_ada,
             g_pre_mix, g_post_mix, g_pre_mlp, g_post_mlp, w_in, b_gate, g_mlstm, w_dw, b_dw, ln_g, ln_b,
             w_pw, rpb, w_out, w_ff1, w_ff2):
    d = cfg.d_model
    x = jnp.concatenate([x_prompt.reshape(cfg.n_ctx, d), x_sample.reshape(cfg.n_lat, d)], axis=0)

    n_cond = 1 + cfg.dec_batch
    cond_rows = -(-n_cond // 8) * 8
    cond = jnp.concatenate([c_ctx[None, :], c, jnp.zeros((cond_rows - n_cond, d), F32)], axis=0)
    mod = _modulation(cond, w_ada, b_ada, tn=min(1024, 6 * d)).reshape(cfg.depth, cond_rows, 6, 1, d)

    rope_tabs = _rope_tables(cfg)
    nw = cfg.na_width
    ks, vs, cs, ns, ms = [], [], [], [], []
    for l in range(cfg.depth):
        w_in_p, w_gt = _prep_w_in(cfg, w_in[l])
        p, gt, kv_new = _inproj(cfg, x, mod[l], g_pre_mix[l], w_in_p, w_gt, rope_tabs)

        bias_rows = jnp.broadcast_to(b_gate[l].reshape(cfg.n_gates, 1, 1), (cfg.n_gates, 1, M_HEAD_DIM))
        ym_c, c_new, n_new, m_new = _mlstm(cfg, p, gt, bias_rows, g_mlstm[l], n_seq=cfg.batch, seq_len=cfg.seq,
                                           row0=0, n_sub=cfg.ctx_sub, unroll=1, name="mlstm_context",
                                           emit_state=True)
        (ym_l,) = _mlstm(cfg, p, gt, bias_rows, g_mlstm[l], n_seq=cfg.dec_batch, seq_len=cfg.dec_seq,
                         row0=cfg.n_ctx, n_sub=1, unroll=cfg.lat_unroll, name="mlstm_latent",
                         state0=(state_C[:, l], state_n[:, l], state_m[:, l]))
        yc = _conv_module(cfg, p, w_dw[l], b_dw[l], ln_g[l], ln_b[l], w_pw[l].astype(BF16))
        yn_c = _ctx_attention(cfg, p)
        yn_l = _na_latent(cfg, p, cache_k[:, l].reshape(cfg.dec_batch, cfg.past_len, nw),
                          cache_v[:, l].reshape(cfg.dec_batch, cfg.past_len, nw),
                          _na_bias_table(rpb[l]))
        ym = jnp.concatenate([ym_c, ym_l], axis=0)
        yn = jnp.concatenate([yn_c, yn_l], axis=0)

        x, h2 = _outproj(cfg, ym, yc, yn, x, w_out[l].astype(BF16), mod[l], g_post_mix[l], g_pre_mlp[l])
        x = _ffn(cfg, h2, x, w_ff1[l].astype(BF16), w_ff2[l].astype(BF16), mod[l], g_post_mlp[l])

        ks.append(kv_new[:, :nw].reshape(cfg.batch, cfg.seq, cfg.na_heads, NA_HEAD_DIM))
        vs.append(kv_new[:, nw:].reshape(cfg.batch, cfg.seq, cfg.na_heads, NA_HEAD_DIM))
        cs.append(c_new)
        ns.append(n_new[:, :, :, 0, :])
        ms.append(m_new[:, :, :, 0, 0])

    y_prompt = x[:cfg.n_ctx].reshape(cfg.batch, cfg.seq, d)
    y_sample = x[cfg.n_ctx:].reshape(cfg.dec_batch, cfg.dec_seq, d)
    return (y_prompt, y_sample, jnp.stack(ks, axis=1), jnp.stack(vs, axis=1),
            jnp.stack(cs, axis=1), jnp.stack(ns, axis=1), jnp.stack(ms, axis=1))


def _make_cfg(x_prompt, x_sample, cache_k, w_in, w_ff1, w_pw, **tiles):
    batch, seq, d = x_prompt.shape
    dec_batch, dec_seq, _ = x_sample.shape
    return Cfg(d_model=d, batch=batch, seq=seq, depth=w_in.shape[0], dec_batch=dec_batch, dec_seq=dec_seq,
               past_len=cache_k.shape[2], m_heads=d // (2 * M_HEAD_DIM), cv_width=w_pw.shape[-1],
               na_heads=cache_k.shape[3], d_ff=w_ff1.shape[-1], **tiles)


def kernel(x_prompt, x_sample, c, cache_k, cache_v, state_C, state_n, state_m, c_ctx, w_ada, b_ada, g_pre_mix,
           g_post_mix, g_pre_mlp, g_post_mlp, w_in, b_gate, g_mlstm, w_dw, b_dw, ln_g, ln_b, w_pw, rpb, w_out,
           w_ff1, w_ff2):
    cfg = _make_cfg(x_prompt, x_sample, cache_k, w_in, w_ff1, w_pw,
                    tm_in=1024, tm_out=512, tm_ff=512, tn_in=1024, th_ff=1024, ctx_sub=4, lat_unroll=4,
                    conv_tt=256)
    return _forward(cfg, x_prompt, x_sample, c, cache_k, cache_v, state_C, state_n, state_m, c_ctx, w_ada, b_ada,
                    g_pre_mix, g_post_mix, g_pre_mlp, g_post_mlp, w_in, b_gate, g_mlstm, w_dw, b_dw, ln_g, ln_b,
                    w_pw, rpb, w_out, w_ff1, w_ff2)
```

```python
import functools
from typing import NamedTuple

import numpy as np
import jax
import jax.numpy as jnp
from jax import lax
from jax.experimental import pallas as pl
from jax.experimental.pallas import tpu as pltpu

F32 = jnp.float32
BF16 = jnp.bfloat16

EPS = 1e-6
NEG_INF = -1e30
ROPE_BASE = 10000.0

LANES = 128
SUBLANES = 8
M_HEAD_DIM = 128
NA_HEAD_DIM = 64
GRID_W = 64
WIN_ROWS = 8
WIN_COLS = 16
CONV_TAPS = 31
CONV_HALO = 16
VMEM_LIMIT = 52 * 1024 * 1024


class Cfg(NamedTuple):
    d_model: int
    batch: int
    seq: int
    depth: int
    dec_batch: int
    dec_seq: int
    past_len: int
    m_heads: int
    cv_width: int
    na_heads: int
    d_ff: int
    tm_in: int
    tm_out: int
    tm_ff: int
    tn_in: int
    th_ff: int
    ctx_sub: int
    lat_unroll: int
    conv_tt: int

    @property
    def m_width(self):
        return self.m_heads * M_HEAD_DIM

    @property
    def na_width(self):
        return self.na_heads * NA_HEAD_DIM

    @property
    def n_gates(self):
        return 4 * self.m_heads

    @property
    def n_ctx(self):
        return self.batch * self.seq

    @property
    def n_lat(self):
        return self.dec_batch * self.dec_seq

    @property
    def n_tok(self):
        return self.n_ctx + self.n_lat

    @property
    def col_ca(self):
        return 4 * self.m_width

    @property
    def col_kv(self):
        return self.col_ca + 2 * self.cv_width

    @property
    def col_qn(self):
        return self.col_kv + 2 * self.na_width

    @property
    def col_gates(self):
        return self.col_qn + self.na_width

    @property
    def in_cols(self):
        raw = self.col_gates + LANES
        return -(-raw // self.tn_in) * self.tn_in


def _cparams(sem):
    return pltpu.CompilerParams(dimension_semantics=sem, vmem_limit_bytes=VMEM_LIMIT)


def _rms(x):
    return x * lax.rsqrt(jnp.mean(x * x, axis=-1, keepdims=True) + EPS)


def _dot(a, b):
    return jnp.dot(a, b, preferred_element_type=F32)


def _dot_nt(a, b):
    return lax.dot_general(a, b, (((1,), (1,)), ((), ())), preferred_element_type=F32)


def _dot_tn(a, b):
    return lax.dot_general(a, b, (((0,), (0,)), ((), ())), preferred_element_type=F32)


def _mod_kernel(c_ref, w_ref, b_ref, o_ref):
    c = c_ref[...]
    s = (c * jax.nn.sigmoid(c)).astype(BF16)
    o_ref[...] = _dot(s, w_ref[...].astype(BF16)) + b_ref[...]


def _modulation(cond, w_ada, b_ada, tn):
    depth, d, n6 = w_ada.shape
    rows = cond.shape[0]
    return pl.pallas_call(
        _mod_kernel,
        out_shape=jax.ShapeDtypeStruct((depth, rows, n6), F32),
        grid=(depth, n6 // tn),
        in_specs=[
            pl.BlockSpec((rows, d), lambda l, j: (0, 0)),
            pl.BlockSpec((None, d, tn), lambda l, j: (l, 0, j)),
            pl.BlockSpec((None, 1, tn), lambda l, j: (l, 0, j)),
        ],
        out_specs=pl.BlockSpec((None, rows, tn), lambda l, j: (l, 0, j)),
        compiler_params=_cparams(("arbitrary", "arbitrary")),
        name="adaln_modulation",
    )(cond, w_ada, b_ada.reshape(depth, 1, n6))


def _mod_row_map(cfg, which, tm):
    assert cfg.n_ctx % tm == 0 and cfg.dec_seq % tm == 0
    nct = cfg.n_ctx // tm
    tpb = cfg.dec_seq // tm

    def index_map(i, *_):
        row = jnp.where(i < nct, 0, 1 + (i - nct) // tpb)
        return (row, which, 0, 0)

    return index_map


def _mod_spec(cfg, which, tm):
    return pl.BlockSpec((None, None, 1, cfg.d_model), _mod_row_map(cfg, which, tm))


ROW_CHUNK = 128


def _rope(x, cos, sin_signed, even_block):
    swapped = jnp.where(even_block, pltpu.roll(x, 96, 1), pltpu.roll(x, 32, 1))
    return x * cos + swapped * sin_signed


IN_ROWS = 512


def _inproj_kernel(x_ref, sh_ref, sc_ref, g_ref, w_ref, wgt_ref, cos_ref, sin_ref, o_ref, gt_ref, kv_ref,
                   h_scr, acc_scr, *, q_tiles, k_tiles, kv_tile, n_ctx_tiles):
    i = pl.program_id(0)
    j = pl.program_id(1)
    tm, tn = o_ref.shape

    def norm_rows(rows):
        h = _rms(x_ref[rows, :]) * g_ref[...] * (1.0 + sc_ref[...]) + sh_ref[...]
        h_scr[rows, :] = h.astype(BF16)

    def matmul_rope(rows):
        acc_scr[rows, :] = _dot(h_scr[rows, :], w_ref[...])
        scale = jnp.where(j >= q_tiles, M_HEAD_DIM ** -0.5, 1.0).astype(F32)
        lane_blk = lax.broadcasted_iota(jnp.int32, (ROW_CHUNK, M_HEAD_DIM), 1) // 32
        even_block = (lane_blk % 2) == 0
        for r in range(rows.start, rows.stop, ROW_CHUNK):
            chunk = slice(r, r + ROW_CHUNK)
            cos, sin = cos_ref[chunk, :], sin_ref[chunk, :]
            for hh in range(tn // M_HEAD_DIM):
                cols = slice(hh * M_HEAD_DIM, (hh + 1) * M_HEAD_DIM)
                o_ref[chunk, cols] = (_rope(acc_scr[chunk, cols], cos, sin, even_block) * scale).astype(o_ref.dtype)

    @pl.when(j == 0)
    def _():
        blk_rows = min(IN_ROWS, tm)
        for blk in range(tm // blk_rows):
            for r in range(blk * blk_rows, (blk + 1) * blk_rows, ROW_CHUNK):
                norm_rows(slice(r, r + ROW_CHUNK))
            matmul_rope(slice(blk * blk_rows, (blk + 1) * blk_rows))
        gt_ref[...] = _dot_nt(wgt_ref[...], h_scr[...])

    @pl.when((j > 0) & (j < q_tiles + k_tiles))
    def _():
        matmul_rope(slice(0, tm))

    @pl.when((j >= q_tiles + k_tiles) & (j != kv_tile))
    def _():
        o_ref[...] = _dot(h_scr[...], w_ref[...]).astype(o_ref.dtype)

    @pl.when(j == kv_tile)
    def _():
        acc_scr[...] = _dot(h_scr[...], w_ref[...])
        o_ref[...] = acc_scr[...].astype(o_ref.dtype)

        @pl.when(i < n_ctx_tiles)
        def _():
            kv_ref[...] = acc_scr[...]


def _inproj(cfg, x, modl, g, w_in_p, w_gt, rope_tabs):
    d, tm, tn = cfg.d_model, cfg.tm_in, cfg.tn_in
    ng = cfg.n_gates
    nct = cfg.n_ctx // tm
    assert cfg.m_width % tn == 0 and cfg.col_kv % tn == 0 and 2 * cfg.na_width == tn and nct * tm == cfg.n_ctx
    kern = functools.partial(_inproj_kernel, q_tiles=cfg.m_width // tn, k_tiles=cfg.m_width // tn,
                             kv_tile=cfg.col_kv // tn, n_ctx_tiles=nct)
    return pl.pallas_call(
        kern,
        out_shape=(jax.ShapeDtypeStruct((cfg.n_tok, cfg.in_cols), BF16),
                   jax.ShapeDtypeStruct((ng, cfg.n_tok), F32),
                   jax.ShapeDtypeStruct((cfg.n_ctx, tn), F32)),
        grid=(cfg.n_tok // tm, cfg.in_cols // tn),
        in_specs=[
            pl.BlockSpec((tm, d), lambda i, j: (i, 0)),
            _mod_spec(cfg, 0, tm),
            _mod_spec(cfg, 1, tm),
            pl.BlockSpec((1, d), lambda i, j: (0, 0)),
            pl.BlockSpec((d, tn), lambda i, j: (0, j)),
            pl.BlockSpec((ng, d), lambda i, j: (0, 0)),
            pl.BlockSpec((tm, M_HEAD_DIM), lambda i, j: (i, 0)),
            pl.BlockSpec((tm, M_HEAD_DIM), lambda i, j: (i, 0)),
        ],
        out_specs=(pl.BlockSpec((tm, tn), lambda i, j: (i, j)),
                   pl.BlockSpec((ng, tm), lambda i, j: (0, i)),
                   pl.BlockSpec((tm, tn), lambda i, j: (jnp.minimum(i, nct - 1), 0))),
        scratch_shapes=[pltpu.VMEM((tm, d), BF16), pltpu.VMEM((tm, tn), F32)],
        compiler_params=_cparams(("arbitrary", "arbitrary")),
        name="in_projection",
    )(x, modl, modl, g.reshape(1, d), w_in_p, w_gt, *rope_tabs)


ROW_A, ROW_M, ROW_E, ROW_WK, ROW_MPREV, ROW_DECAY = range(6)
LOG2_E = 1.4426950408889634
FINISH_ROWS = 512


def _log_sigmoid(x):
    return jnp.minimum(x, 0.0) - jnp.log1p(jnp.exp(-jnp.abs(x)))


def _cumsum_lanes(x, cum_mat):
    hi = x.astype(BF16)
    r1 = x - hi.astype(F32)
    mid = r1.astype(BF16)
    lo = (r1 - mid.astype(F32)).astype(BF16)
    return _dot(hi, cum_mat) + _dot(mid, cum_mat) + _dot(lo, cum_mat)


def _cummax_lanes_both(xs):
    fwd, bwd = xs
    n = fwd.shape[1]
    lane = lax.broadcasted_iota(jnp.int32, fwd.shape, 1)
    k = 1
    while k < n:
        fwd = jnp.maximum(fwd, jnp.where(lane >= k, pltpu.roll(fwd, k, 1), -jnp.inf))
        bwd = jnp.maximum(bwd, jnp.where(lane < n - k, pltpu.roll(bwd, n - k, 1), -jnp.inf))
        k *= 2
    return [fwd, bwd]


def _col_bcast(row, n):
    return jnp.transpose(jnp.broadcast_to(row, (n, row.shape[1])))


def _mlstm_kernel(*refs, seq_len, n_sub, heads, unroll, init_state, emit_state):
    it = iter(refs)
    q_ref, k_ref, v_ref, om_ref, gt_ref, bias_ref, gm_ref = (next(it) for _ in range(7))
    c0_ref = n0_ref = m0_ref = None
    if init_state:
        c0_ref, n0_ref, m0_ref = next(it), next(it), next(it)
    o_ref = next(it)
    co_ref = no_ref = mo_ref = None
    if emit_state:
        co_ref, no_ref, mo_ref = next(it), next(it), next(it)
    hs_f, hs_b, cext, rows_scr = next(it), next(it), next(it), next(it)

    L = dk = M_HEAD_DIM
    head = pl.program_id(1)
    nc = seq_len // L

    tt = lax.broadcasted_iota(jnp.int32, (L, L), 0)
    ss = lax.broadcasted_iota(jnp.int32, (L, L), 1)
    cum_mats = ((tt <= ss).astype(BF16), (tt >= ss).astype(BF16))
    causal = (ss <= tt, ss >= tt)

    dirs = (0, 1)
    row_i = [d * 2 * heads + head for d in dirs]
    xi = [gt_ref[r] + bias_ref[r] for r in row_i]
    lf = [_log_sigmoid(gt_ref[r + heads] + bias_ref[r + heads]) for r in row_i]
    b = [_cumsum_lanes(lf[d], cum_mats[d]) for d in dirs]
    tot = [jnp.sum(lf[d], axis=1, keepdims=True) for d in dirs]
    a = [xi[d] - b[d] for d in dirs]
    g = [tot[d] - b[d] + xi[d] for d in dirs]
    gmax = [jnp.max(g[d], axis=1, keepdims=True) for d in dirs]
    arun = _cummax_lanes_both(a)
    a_l2 = [a[d] * LOG2_E for d in dirs]
    for s in range(n_sub):
        m = [m0_ref[s, d][:, 0:1] if init_state else jnp.zeros((1, 1), F32) for d in dirs]
        for step in range(nc):
            for d in dirs:
                r = s * nc + (step if d == 0 else nc - 1 - step)
                m_new = jnp.maximum(tot[d][r:r + 1] + m[d], gmax[d][r:r + 1])
                m_row = jnp.maximum(arun[d][r:r + 1], m[d])
                rows_scr[d, r, ROW_A:ROW_A + 1, :] = a_l2[d][r:r + 1]
                rows_scr[d, r, ROW_M:ROW_M + 1, :] = m_row * LOG2_E
                rows_scr[d, r, ROW_E:ROW_E + 1, :] = -(b[d][r:r + 1] + m_row) * LOG2_E
                rows_scr[d, r, ROW_WK:ROW_WK + 1, :] = jnp.exp(g[d][r:r + 1] - m_new)
                rows_scr[d, r, ROW_MPREV:ROW_MPREV + 1, :] = jnp.broadcast_to(m[d] * LOG2_E, (1, L))
                rows_scr[d, r, ROW_DECAY:ROW_DECAY + 1, :] = jnp.broadcast_to(
                    jnp.exp(tot[d][r:r + 1] + m[d] - m_new), (1, L))
                m[d] = m_new
        for d in dirs:
            if emit_state:
                mo_ref[s, d] = jnp.broadcast_to(m[d], (1, LANES))
            if init_state:
                cext[s, d, :, 0:dk] = c0_ref[s, d]
                cext[s, d, :, dk:2 * dk] = _col_bcast(n0_ref[s, d], dk)
            else:
                cext[s, d] = jnp.zeros((dk, 2 * dk), F32)

    ones_ext = jnp.ones((L, dk), BF16)

    def prepare(s, d, c):
        rows = pl.ds(pl.multiple_of(s * seq_len + c * L, L), L)
        tile = rows_scr[d, s * nc + c]
        q = q_ref[rows, :].astype(BF16)
        k = k_ref[rows, :]
        v_ext = jnp.concatenate([v_ref[rows, :].astype(BF16), ones_ext], axis=1)
        m_b = _col_bcast(tile[ROW_M:ROW_M + 1], dk)
        floor = jnp.exp2(_col_bcast(tile[ROW_E:ROW_E + 1], dk))
        dmat = jnp.exp2(jnp.where(causal[d], tile[ROW_A:ROW_A + 1] - m_b, -jnp.inf))
        sc = (_dot_nt(q, k.astype(BF16)) * dmat).astype(BF16)
        w_int = jnp.exp2(tile[ROW_MPREV:ROW_MPREV + 1] - m_b)
        lhs = jnp.concatenate([sc, (w_int * q.astype(F32)).astype(BF16)], axis=1)
        kt_w = (jnp.transpose(k.astype(F32)) * tile[ROW_WK:ROW_WK + 1]).astype(BF16)
        return rows, lhs, v_ext, floor, kt_w, tile[ROW_DECAY:ROW_DECAY + 1, 0:1]

    def apply(s, d, prepared):
        rows, lhs, v_ext, floor, kt_w, decay = prepared
        c_prev = cext[s, d]
        nd = _dot(lhs, jnp.concatenate([v_ext, c_prev.astype(BF16)], axis=0))
        den = jnp.maximum(jnp.abs(nd[:, dk:2 * dk]), floor)
        (hs_f if d == 0 else hs_b)[rows, :] = nd[:, 0:dk] / den
        cext[s, d] = decay * c_prev + _dot(kt_w, v_ext)

    def chunk_body(i, carry):
        todo = []
        for u in range(unroll):
            c = i * unroll + u
            for s in range(n_sub):
                todo.append((s, 0, prepare(s, 0, c)))
                todo.append((s, 1, prepare(s, 1, nc - 1 - c)))
        for s, d, prepared in todo:
            apply(s, d, prepared)
        return carry

    lax.fori_loop(0, nc // unroll, chunk_body, 0)

    def finish(r, carry):
        rows = pl.ds(pl.multiple_of(r * FINISH_ROWS, FINISH_ROWS), FINISH_ROWS)
        y = _rms(hs_f[rows, :] + hs_b[rows, :]) * gm_ref[...] * jax.nn.sigmoid(om_ref[rows, :].astype(F32))
        o_ref[rows, :] = y.astype(o_ref.dtype)
        return carry

    lax.fori_loop(0, n_sub * seq_len // FINISH_ROWS, finish, 0)

    if emit_state:
        for s in range(n_sub):
            for d in range(2):
                co_ref[s, d] = cext[s, d, :, 0:dk]
                no_ref[s, d] = jnp.transpose(cext[s, d, :, dk:2 * dk])[0:1, :]


def _mlstm(cfg, p, gt, bias_rows, g_mlstm, *, n_seq, seq_len, row0, n_sub, unroll, name, state0=None,
           emit_state=False):
    heads, dk = cfg.m_heads, M_HEAD_DIM
    L = dk
    nc = seq_len // L
    blk_rows = n_sub * seq_len
    sb = row0 // blk_rows
    assert sb * blk_rows == row0 and nc * L == seq_len and n_seq % n_sub == 0 and nc % unroll == 0
    assert (n_sub * nc) % 8 == 0
    n_steps = n_seq // n_sub

    def col(block0):
        return pl.BlockSpec((blk_rows, dk), lambda b, h: (sb + b, block0 + h))

    gt4 = gt[:, row0:row0 + n_seq * seq_len].reshape(cfg.n_gates, n_steps, n_sub * nc, L)
    in_specs = [col(0), col(heads), col(2 * heads), col(3 * heads),
                pl.BlockSpec((cfg.n_gates, None, n_sub * nc, L), lambda b, h: (0, b, 0, 0)),
                pl.BlockSpec((cfg.n_gates, 1, L), lambda b, h: (0, 0, 0)),
                pl.BlockSpec((None, 1, dk), lambda b, h: (h, 0, 0))]
    args = [p, p, p, p, gt4, bias_rows, g_mlstm.reshape(heads, 1, dk)]
    state_spec = lambda r, c: pl.BlockSpec((n_sub, 2, None, r, c), lambda b, h: (b, 0, h, 0, 0))
    if state0 is not None:
        c0, n0, m0 = state0
        in_specs += [state_spec(dk, dk), state_spec(1, dk), state_spec(1, LANES)]
        args += [c0, n0.reshape(n_seq, 2, heads, 1, dk),
                 jnp.broadcast_to(m0[..., None, None], (n_seq, 2, heads, 1, LANES))]
    out_shape = [jax.ShapeDtypeStruct((n_seq * seq_len, cfg.m_width), BF16)]
    out_specs = [pl.BlockSpec((blk_rows, dk), lambda b, h: (b, h))]
    if emit_state:
        out_shape += [jax.ShapeDtypeStruct((n_seq, 2, heads, dk, dk), F32),
                      jax.ShapeDtypeStruct((n_seq, 2, heads, 1, dk), F32),
                      jax.ShapeDtypeStruct((n_seq, 2, heads, 1, LANES), F32)]
        out_specs += [state_spec(dk, dk), state_spec(1, dk), state_spec(1, LANES)]
    kern = functools.partial(_mlstm_kernel, seq_len=seq_len, n_sub=n_sub, heads=heads, unroll=unroll,
                             init_state=state0 is not None, emit_state=emit_state)
    return pl.pallas_call(
        kern,
        out_shape=tuple(out_shape),
        grid=(n_steps, heads),
        in_specs=in_specs,
        out_specs=tuple(out_specs),
        scratch_shapes=[pltpu.VMEM((blk_rows, dk), F32), pltpu.VMEM((blk_rows, dk), F32),
                        pltpu.VMEM((n_sub, 2, dk, 2 * dk), F32), pltpu.VMEM((2, n_sub * nc, 8, L), F32)],
        compiler_params=_cparams(("arbitrary", "arbitrary")),
        name=name,
    )(*args)


def _rope_tables(cfg):
    half = M_HEAD_DIM // 2
    nf = half // 2
    t = jnp.arange(cfg.dec_seq)
    inv = ROPE_BASE ** (-jnp.arange(nf, dtype=F32) / nf)
    ang_r = (t // GRID_W).astype(F32)[:, None] * inv[None, :]
    ang_c = (t % GRID_W).astype(F32)[:, None] * inv[None, :]
    cos = jnp.concatenate([jnp.cos(ang_r)] * 2 + [jnp.cos(ang_c)] * 2, axis=-1)
    sin = jnp.concatenate([-jnp.sin(ang_r), jnp.sin(ang_r), -jnp.sin(ang_c), jnp.sin(ang_c)], axis=-1)
    cos = jnp.concatenate([jnp.ones((cfg.n_ctx, M_HEAD_DIM), F32), jnp.tile(cos, (cfg.dec_batch, 1))], axis=0)
    sin = jnp.concatenate([jnp.zeros((cfg.n_ctx, M_HEAD_DIM), F32), jnp.tile(sin, (cfg.dec_batch, 1))], axis=0)
    return cos, sin


CONV_ROWS = 32


def _conv_kernel(a_ref, g_ref, ap_ref, gp_ref, an_ref, gn_ref, wdw_ref, bdw_ref, lng_ref, lnb_ref, wpw_ref,
                 o_ref, u_scr, *, n_ctx_tiles, ctx_tiles_per_seq, lat_tiles_per_seq):
    i = pl.program_id(0)
    tt = a_ref.shape[0]
    halo = CONV_HALO
    pos = jnp.where(i < n_ctx_tiles, i % ctx_tiles_per_seq, (i - n_ctx_tiles) % lat_tiles_per_seq)
    per_seq = jnp.where(i < n_ctx_tiles, ctx_tiles_per_seq, lat_tiles_per_seq)
    keep_prev = (pos > 0).astype(F32)
    keep_next = (pos < per_seq - 1).astype(F32)

    glu = lambda a, g: a[...].astype(F32) * jax.nn.sigmoid(g[...].astype(F32))
    u_scr[0, 0:halo, :] = glu(ap_ref, gp_ref) * keep_prev
    u_scr[0, halo:halo + tt, :] = glu(a_ref, g_ref)
    u_scr[0, halo + tt:2 * halo + tt, :] = glu(an_ref, gn_ref) * keep_next
    n_rows = tt + 2 * halo
    window = u_scr[0]
    for r in range(1, SUBLANES):
        u_scr[r] = pltpu.roll(window, n_rows - r, 0)

    for blk in range(tt // CONV_ROWS):
        base = blk * CONV_ROWS
        acc = jnp.broadcast_to(bdw_ref[...], (CONV_ROWS, a_ref.shape[1]))
        for tap in range(CONV_TAPS):
            start = base + halo - CONV_TAPS // 2 + tap
            r, aligned = start % SUBLANES, start - start % SUBLANES
            w_tap = jnp.concatenate([wdw_ref[tap]] * (CONV_ROWS // SUBLANES), axis=0)
            acc = acc + u_scr[r, aligned:aligned + CONV_ROWS, :] * w_tap
        mu = jnp.mean(acc, axis=-1, keepdims=True)
        cen = acc - mu
        var = jnp.mean(cen * cen, axis=-1, keepdims=True)
        y = cen * lax.rsqrt(var + EPS) * lng_ref[...] + lnb_ref[...]
        y = y * jax.nn.sigmoid(y)
        o_ref[base:base + CONV_ROWS, :] = _dot(y.astype(BF16), wpw_ref[...]).astype(o_ref.dtype)


def _conv_module(cfg, p, w_dw, b_dw, ln_g, ln_b, w_pw):
    tt, cv, halo = cfg.conv_tt, cfg.cv_width, CONV_HALO
    n_tiles = cfg.n_tok // tt
    cb = cfg.col_ca // cv
    hb = tt // halo
    last_halo = cfg.n_tok // halo - 1
    main = lambda c: pl.BlockSpec((tt, cv), lambda i: (i, c))
    prev = lambda c: pl.BlockSpec((halo, cv), lambda i: (jnp.maximum(i * hb - 1, 0), c))
    nxt = lambda c: pl.BlockSpec((halo, cv), lambda i: (jnp.minimum((i + 1) * hb, last_halo), c))
    vec = lambda: pl.BlockSpec((1, cv), lambda i: (0, 0))
    kern = functools.partial(_conv_kernel, n_ctx_tiles=cfg.n_ctx // tt, ctx_tiles_per_seq=cfg.seq // tt,
                             lat_tiles_per_seq=cfg.dec_seq // tt)
    return pl.pallas_call(
        kern,
        out_shape=jax.ShapeDtypeStruct((cfg.n_tok, cv), BF16),
        grid=(n_tiles,),
        in_specs=[main(cb), main(cb + 1), prev(cb), prev(cb + 1), nxt(cb), nxt(cb + 1),
                  pl.BlockSpec((CONV_TAPS, SUBLANES, cv), lambda i: (0, 0, 0)), vec(), vec(), vec(),
                  pl.BlockSpec((cv, cv), lambda i: (0, 0))],
        out_specs=pl.BlockSpec((tt, cv), lambda i: (i, 0)),
        scratch_shapes=[pltpu.VMEM((SUBLANES, tt + 2 * halo, cv), F32)],
        compiler_params=_cparams(("arbitrary",)),
        name="conv_module",
    )(p, p, p, p, p, p, jnp.broadcast_to(w_dw[:, None, :], (CONV_TAPS, SUBLANES, cv)),
      b_dw.reshape(1, cv), ln_g.reshape(1, cv), ln_b.reshape(1, cv), w_pw)


def _softmax_pv(scores, values):
    mx = functools.reduce(jnp.maximum, [jnp.max(s, axis=1, keepdims=True) for s in scores])
    ps = [jnp.exp(s - mx) for s in scores]
    den = functools.reduce(jnp.add, [jnp.sum(p, axis=1, keepdims=True) for p in ps])
    num = functools.reduce(jnp.add, [_dot(p.astype(BF16), v) for p, v in zip(ps, values)])
    return num / den


def _head_pair(q2, scores_fn, values):
    first_half = lax.broadcasted_iota(jnp.int32, q2.shape, 1) < NA_HEAD_DIM
    zero = jnp.zeros_like(q2)
    out_even = _softmax_pv(scores_fn(jnp.where(first_half, q2, zero), 0), values)
    out_odd = _softmax_pv(scores_fn(jnp.where(first_half, zero, q2), 1), values)
    return jnp.where(first_half, out_even, out_odd)


def _ctx_attn_kernel(q_ref, k_ref, v_ref, o_ref, *, heads):
    dh = NA_HEAD_DIM
    q = (q_ref[...] * (dh ** -0.5)).astype(BF16)
    k = k_ref[...].astype(BF16)
    v = v_ref[...].astype(BF16)
    outs = []
    for hp in range(heads // 2):
        sl = slice(hp * LANES, (hp + 1) * LANES)
        outs.append(_head_pair(q[:, sl], lambda qh, half: [_dot_nt(qh, k[:, sl])], [v[:, sl]]))
    o_ref[...] = jnp.concatenate(outs, axis=1).astype(o_ref.dtype)


def _ctx_attention(cfg, p):
    s, nw = cfg.seq, cfg.na_width
    qb, kb = cfg.col_qn // nw, cfg.col_kv // nw
    spec = lambda c: pl.BlockSpec((s, nw), lambda b: (b, c))
    return pl.pallas_call(
        functools.partial(_ctx_attn_kernel, heads=cfg.na_heads),
        out_shape=jax.ShapeDtypeStruct((cfg.n_ctx, nw), BF16),
        grid=(cfg.batch,),
        in_specs=[spec(qb), spec(kb), spec(kb + 1)],
        out_specs=pl.BlockSpec((s, nw), lambda b: (b, 0)),
        compiler_params=_cparams(("arbitrary",)),
        name="context_attention",
    )(p, p, p)


NA_QROWS = 8


def _na_kernel(q_ref, kp_ref, kc_ref, kn_ref, vp_ref, vc_ref, vn_ref, kx_ref, vx_ref, bt_ref, o_ref,
               kw_scr, vw_scr, *, heads, grid_rows):
    dh = NA_HEAD_DIM
    blk = NA_QROWS * GRID_W
    win = WIN_ROWS * GRID_W
    m = pl.program_id(1)
    for slot, (kr, vr) in enumerate(((kp_ref, vp_ref), (kc_ref, vc_ref), (kn_ref, vn_ref))):
        kw_scr[slot * blk:(slot + 1) * blk, :] = kr[...].astype(BF16)
        vw_scr[slot * blk:(slot + 1) * blk, :] = vr[...].astype(BF16)
    kx = kx_ref[...].astype(BF16)
    vx = vx_ref[...].astype(BF16)

    def row_body(i, carry):
        r = m * NA_QROWS + i
        rs = jnp.clip(r - WIN_ROWS // 2, 0, grid_rows - WIN_ROWS)
        start = rs - r + (WIN_ROWS - 1)
        off = pl.multiple_of((rs - NA_QROWS * (m - 1)) * GRID_W, GRID_W)
        qrows = pl.ds(pl.multiple_of(i * GRID_W, GRID_W), GRID_W)
        q = (q_ref[qrows, :].astype(F32) * (dh ** -0.5 * LOG2_E)).astype(BF16)
        kwin = kw_scr[pl.ds(off, win), :]
        vwin = vw_scr[pl.ds(off, win), :]
        first_half = lax.broadcasted_iota(jnp.int32, (GRID_W, LANES), 1) < dh
        pairs = [slice(hp * LANES, (hp + 1) * LANES) for hp in range(heads // 2)]
        scores = []
        for hp, sl in enumerate(pairs):
            q2 = q[:, sl]
            zero = jnp.zeros_like(q2)
            qm = jnp.concatenate([jnp.where(first_half, q2, zero), jnp.where(first_half, zero, q2)], axis=0)
            scores.append((_dot_nt(kwin[:, sl], qm) + bt_ref[start, hp],
                           _dot_nt(kx[:, sl], qm)))
        probs = []
        for s_loc, s_ctx in scores:
            mx = jnp.maximum(jnp.max(s_loc, axis=0, keepdims=True), jnp.max(s_ctx, axis=0, keepdims=True))
            p_loc = jnp.exp2(s_loc - mx)
            p_ctx = jnp.exp2(s_ctx - mx)
            inv = 1.0 / (jnp.sum(p_loc, axis=0, keepdims=True) + jnp.sum(p_ctx, axis=0, keepdims=True))
            probs.append((p_loc.astype(BF16), p_ctx.astype(BF16), _col_bcast(inv, LANES)))
        outs = []
        for (p_loc, p_ctx, inv_col), sl in zip(probs, pairs):
            pv = (_dot_tn(p_loc, vwin[:, sl]) + _dot_tn(p_ctx, vx[:, sl])) * inv_col
            outs.append(jnp.where(first_half, pv[0:GRID_W], pv[GRID_W:2 * GRID_W]))
        o_ref[qrows, :] = jnp.concatenate(outs, axis=1).astype(o_ref.dtype)
        return carry

    lax.fori_loop(0, NA_QROWS, row_body, 0)


def _na_bias_table(rpb):
    heads = rpb.shape[0]
    qc = np.arange(GRID_W)[:, None]
    kc = np.arange(GRID_W)[None, :]
    cs = np.clip(qc - WIN_COLS // 2, 0, GRID_W - WIN_COLS)
    valid = (kc >= cs) & (kc < cs + WIN_COLS)
    dc = np.clip(kc - qc + WIN_COLS - 1, 0, 2 * WIN_COLS - 2)
    dr = np.arange(WIN_ROWS)[:, None] + np.arange(WIN_ROWS)[None, :]
    tab = (rpb * LOG2_E)[:, dr][:, :, :, dc]
    tab = jnp.where(jnp.asarray(valid)[None, None, None], tab, NEG_INF)
    tab = tab.reshape(heads // 2, 2, WIN_ROWS, WIN_ROWS, GRID_W, GRID_W)
    tab = tab.transpose(2, 0, 3, 5, 1, 4)
    return tab.reshape(WIN_ROWS, heads // 2, WIN_ROWS * GRID_W, 2 * GRID_W)


def _na_latent(cfg, p, k_ctx, v_ctx, bias_tab):
    nw, heads = cfg.na_width, cfg.na_heads
    blk = NA_QROWS * GRID_W
    grid_rows = cfg.dec_seq // GRID_W
    nblk = grid_rows // NA_QROWS
    qb, kb = cfg.col_qn // nw, cfg.col_kv // nw
    rb0 = cfg.n_ctx // blk
    assert rb0 * blk == cfg.n_ctx and nblk * NA_QROWS == grid_rows

    def rows(shift, c):
        return pl.BlockSpec((blk, nw), lambda b, m: (rb0 + b * nblk + jnp.clip(m + shift, 0, nblk - 1), c))

    ctx_spec = pl.BlockSpec((None, cfg.past_len, nw), lambda b, m: (b, 0, 0))
    return pl.pallas_call(
        functools.partial(_na_kernel, heads=heads, grid_rows=grid_rows),
        out_shape=jax.ShapeDtypeStruct((cfg.n_lat, nw), BF16),
        grid=(cfg.dec_batch, nblk),
        in_specs=[rows(0, qb), rows(-1, kb), rows(0, kb), rows(1, kb),
                  rows(-1, kb + 1), rows(0, kb + 1), rows(1, kb + 1), ctx_spec, ctx_spec,
                  pl.BlockSpec((WIN_ROWS, heads // 2, WIN_ROWS * GRID_W, 2 * GRID_W), lambda b, m: (0, 0, 0, 0))],
        out_specs=pl.BlockSpec((blk, nw), lambda b, m: (b * nblk + m, 0)),
        scratch_shapes=[pltpu.VMEM((3 * blk, nw), BF16), pltpu.VMEM((3 * blk, nw), BF16)],
        compiler_params=_cparams(("arbitrary", "arbitrary")),
        name="neighbourhood_attention",
    )(p, p, p, p, p, p, p, k_ctx, v_ctx, bias_tab)


OUT_ROWS = 256


def _outproj_kernel(ym_ref, yc_ref, yn_ref, x_ref, w_ref, ga_ref, gpost_ref, gpre_ref, sc_ref, sh_ref,
                    xo_ref, ho_ref, y_scr, *, m_width, cv_width):
    a, b = m_width, m_width + cv_width
    for blk in range(x_ref.shape[0] // OUT_ROWS):
        blk_rows = slice(blk * OUT_ROWS, (blk + 1) * OUT_ROWS)
        y_scr[blk_rows, :] = (_dot(ym_ref[blk_rows, :], w_ref[0:a, :]) + _dot(yc_ref[blk_rows, :], w_ref[a:b, :])
                              + _dot(yn_ref[blk_rows, :], w_ref[b:, :]))
        for r in range(OUT_ROWS // ROW_CHUNK):
            rows = slice(blk * OUT_ROWS + r * ROW_CHUNK, blk * OUT_ROWS + (r + 1) * ROW_CHUNK)
            x = x_ref[rows, :] + ga_ref[...] * (_rms(y_scr[rows, :]) * gpost_ref[...])
            xo_ref[rows, :] = x
            ho_ref[rows, :] = (_rms(x) * gpre_ref[...] * (1.0 + sc_ref[...]) + sh_ref[...]).astype(BF16)


def _outproj(cfg, ym, yc, yn, x, w_out, modl, g_post, g_pre):
    d, tm = cfg.d_model, cfg.tm_out
    rowblk = lambda w: pl.BlockSpec((tm, w), lambda i: (i, 0))
    vec = lambda: pl.BlockSpec((1, d), lambda i: (0, 0))
    return pl.pallas_call(
        functools.partial(_outproj_kernel, m_width=cfg.m_width, cv_width=cfg.cv_width),
        out_shape=(jax.ShapeDtypeStruct((cfg.n_tok, d), F32), jax.ShapeDtypeStruct((cfg.n_tok, d), BF16)),
        grid=(cfg.n_tok // tm,),
        in_specs=[rowblk(cfg.m_width), rowblk(cfg.cv_width), rowblk(cfg.na_width), rowblk(d),
                  pl.BlockSpec((d, d), lambda i: (0, 0)),
                  _mod_spec(cfg, 2, tm), vec(), vec(), _mod_spec(cfg, 4, tm), _mod_spec(cfg, 3, tm)],
        out_specs=(rowblk(d), rowblk(d)),
        scratch_shapes=[pltpu.VMEM((tm, d), F32)],
        compiler_params=_cparams(("arbitrary",)),
        name="out_projection",
    )(ym, yc, yn, x, w_out, modl, g_post.reshape(1, d), g_pre.reshape(1, d), modl, modl)


def _ffn_kernel(h_ref, x_ref, w1_ref, w2_ref, ga_ref, g_ref, o_ref, acc_ref):
    kk = pl.program_id(1)

    @pl.when(kk == 0)
    def _():
        acc_ref[...] = jnp.zeros_like(acc_ref)

    last = pl.num_programs(1) - 1
    tm = x_ref.shape[0]

    def accumulate(rows):
        a = jnp.maximum(_dot(h_ref[rows, :], w1_ref[...]), 0.0)
        acc_ref[rows, :] += _dot((a * a).astype(BF16), w2_ref[...])

    @pl.when(kk < last)
    def _():
        accumulate(slice(0, tm))

    @pl.when(kk == last)
    def _():
        blk_rows = min(OUT_ROWS, tm)
        for blk in range(tm // blk_rows):
            accumulate(slice(blk * blk_rows, (blk + 1) * blk_rows))
            for r in range(blk * blk_rows, (blk + 1) * blk_rows, ROW_CHUNK):
                rows = slice(r, r + ROW_CHUNK)
                o_ref[rows, :] = x_ref[rows, :] + ga_ref[...] * (_rms(acc_ref[rows, :]) * g_ref[...])


def _ffn(cfg, h, x, w1, w2, modl, g_post):
    d, tm, th = cfg.d_model, cfg.tm_ff, cfg.th_ff
    return pl.pallas_call(
        _ffn_kernel,
        out_shape=jax.ShapeDtypeStruct((cfg.n_tok, d), F32),
        grid=(cfg.n_tok // tm, cfg.d_ff // th),
        in_specs=[pl.BlockSpec((tm, d), lambda i, k: (i, 0)),
                  pl.BlockSpec((tm, d), lambda i, k: (i, 0)),
                  pl.BlockSpec((d, th), lambda i, k: (0, k)),
                  pl.BlockSpec((th, d), lambda i, k: (k, 0)),
                  _mod_spec(cfg, 5, tm),
                  pl.BlockSpec((1, d), lambda i, k: (0, 0))],
        out_specs=pl.BlockSpec((tm, d), lambda i, k: (i, 0)),
        scratch_shapes=[pltpu.VMEM((tm, d), F32)],
        compiler_params=_cparams(("arbitrary", "arbitrary")),
        name="relu2_mlp",
    )(h, x, w1, w2, modl, g_post.reshape(1, d))


def _prep_w_in(cfg, w_in_l):
    mw, cv, nw, ng = cfg.m_width, cfg.cv_width, cfg.na_width, cfg.n_gates
    g0 = 4 * mw
    c0 = g0 + ng
    n0 = c0 + 2 * cv
    gates = w_in_l[:, g0:c0]
    parts = [w_in_l[:, :g0], w_in_l[:, c0:n0], w_in_l[:, n0 + nw:n0 + 3 * nw], w_in_l[:, n0:n0 + nw], gates]
    pad = cfg.in_cols - (cfg.col_gates + ng)
    w_p = jnp.concatenate(parts + [jnp.zeros((cfg.d_model, pad), w_in_l.dtype)], axis=1)
    return w_p.astype(BF16), gates.T.astype(BF16)


def _forward(cfg, x_prompt, x_sample, c, cache_k, cache_v, state_C, state_n, state_m, c_ctx, w_ada, b_ada,
             g_pre_mix, g_post_mix, g_pre_mlp, g_post_mlp, w_in, b_gate, g_mlstm, w_dw, b_dw, ln_g, ln_b,
             w_pw, rpb, w_out, w_ff1, w_ff2):
    d = cfg.d_model
    x = jnp.concatenate([x_prompt.reshape(cfg.n_ctx, d), x_sample.reshape(cfg.n_lat, d)], axis=0)

    n_cond = 1 + cfg.dec_batch
    cond_rows = -(-n_cond // 8) * 8
    cond = jnp.concatenate([c_ctx[None, :], c, jnp.zeros((cond_rows - n_cond, d), F32)], axis=0)
    mod = _modulation(cond, w_ada, b_ada, tn=min(1024, 6 * d)).reshape(cfg.depth, cond_rows, 6, 1, d)

    rope_tabs = _rope_tables(cfg)
    nw = cfg.na_width
    ks, vs, cs, ns, ms = [], [], [], [], []
    for l in range(cfg.depth):
        w_in_p, w_gt = _prep_w_in(cfg, w_in[l])
        p, gt, kv_new = _inproj(cfg, x, mod[l], g_pre_mix[l], w_in_p, w_gt, rope_tabs)

        bias_rows = jnp.broadcast_to(b_gate[l].reshape(cfg.n_gates, 1, 1), (cfg.n_gates, 1, M_HEAD_DIM))
        ym_c, c_new, n_new, m_new = _mlstm(cfg, p, gt, bias_rows, g_mlstm[l], n_seq=cfg.batch, seq_len=cfg.seq,
                                           row0=0, n_sub=cfg.ctx_sub, unroll=1, name="mlstm_context",
                                           emit_state=True)
        (ym_l,) = _mlstm(cfg, p, gt, bias_rows, g_mlstm[l], n_seq=cfg.dec_batch, seq_len=cfg.dec_seq,
                         row0=cfg.n_ctx, n_sub=1, unroll=cfg.lat_unroll, name="mlstm_latent",
                         state0=(state_C[:, l], state_n[:, l], state_m[:, l]))
        yc = _conv_module(cfg, p, w_dw[l], b_dw[l], ln_g[l], ln_b[l], w_pw[l].astype(BF16))
        yn_c = _ctx_attention(cfg, p)
        yn_l = _na_latent(cfg, p, cache_k[:, l].reshape(cfg.dec_batch, cfg.past_len, nw),
                          cache_v[:, l].reshape(cfg.dec_batch, cfg.past_len, nw),
                          _na_bias_table(rpb[l]))
        ym = jnp.concatenate([ym_c, ym_l], axis=0)
        yn = jnp.concatenate([yn_c, yn_l], axis=0)

        x, h2 = _outproj(cfg, ym, yc, yn, x, w_out[l].astype(BF16), mod[l], g_post_mix[l], g_pre_mlp[l])
        x = _ffn(cfg, h2, x, w_ff1[l].astype(BF16), w_ff2[l].astype(BF16), mod[l], g_post_mlp[l])

        ks.append(kv_new[:, :nw].reshape(cfg.batch, cfg.seq, cfg.na_heads, NA_HEAD_DIM))
        vs.append(kv_new[:, nw:].reshape(cfg.batch, cfg.seq, cfg.na_heads, NA_HEAD_DIM))
        cs.append(c_new)
        ns.append(n_new[:, :, :, 0, :])
        ms.append(m_new[:, :, :, 0, 0])

    y_prompt = x[:cfg.n_ctx].reshape(cfg.batch, cfg.seq, d)
    y_sample = x[cfg.n_ctx:].reshape(cfg.dec_batch, cfg.dec_seq, d)
    return (y_prompt, y_sample, jnp.stack(ks, axis=1), jnp.stack(vs, axis=1),
            jnp.stack(cs, axis=1), jnp.stack(ns, axis=1), jnp.stack(ms, axis=1))


def _make_cfg(x_prompt, x_sample, cache_k, w_in, w_ff1, w_pw, **tiles):
    batch, seq, d = x_prompt.shape
    dec_batch, dec_seq, _ = x_sample.shape
    return Cfg(d_model=d, batch=batch, seq=seq, depth=w_in.shape[0], dec_batch=dec_batch, dec_seq=dec_seq,
               past_len=cache_k.shape[2], m_heads=d // (2 * M_HEAD_DIM), cv_width=w_pw.shape[-1],
               na_heads=cache_k.shape[3], d_ff=w_ff1.shape[-1], **tiles)


def kernel(x_prompt, x_sample, c, cache_k, cache_v, state_C, state_n, state_m, c_ctx, w_ada, b_ada, g_pre_mix,
           g_post_mix, g_pre_mlp, g_post_mlp, w_in, b_gate, g_mlstm, w_dw, b_dw, ln_g, ln_b, w_pw, rpb, w_out,
           w_ff1, w_ff2):
    cfg = _make_cfg(x_prompt, x_sample, cache_k, w_in, w_ff1, w_pw,
                    tm_in=1024, tm_out=512, tm_ff=512, tn_in=1024, th_ff=1024, ctx_sub=4, lat_unroll=4,
                    conv_tt=256)
    return _forward(cfg, x_prompt, x_sample, c, cache_k, cache_v, state_C, state_n, state_m, c_ctx, w_ada, b_ada,
                    g_pre_mix, g_post_mix, g_pre_mlp, g_post_mlp, w_in, b_gate, g_mlstm, w_dw, b_dw, ln_g, ln_b,
                    w_pw, rpb, w_out, w_ff1, w_ff2)
```

```python
import functools
from typing import NamedTuple

import numpy as np
import jax
import jax.numpy as jnp
from jax import lax
from jax.experimental import pallas as pl
from jax.experimental.pallas import tpu as pltpu

F32 = jnp.float32
BF16 = jnp.bfloat16

EPS = 1e-6
NEG_INF = -1e30
ROPE_BASE = 10000.0

LANES = 128
SUBLANES = 8
M_HEAD_DIM = 128
NA_HEAD_DIM = 64
GRID_W = 64
WIN_ROWS = 8
WIN_COLS = 16
CONV_TAPS = 31
CONV_HALO = 16
VMEM_LIMIT = 52 * 1024 * 1024
VMEM_LIMIT_MLP = 60 * 1024 * 1024


class Cfg(NamedTuple):
    d_model: int
    batch: int
    seq: int
    depth: int
    dec_batch: int
    dec_seq: int
    past_len: int
    m_heads: int
    cv_width: int
    na_heads: int
    d_ff: int
    tm_in: int
    tm_out: int
    tm_ff: int
    tn_in: int
    th_ff: int
    ctx_sub: int
    lat_unroll: int
    conv_tt: int

    @property
    def m_width(self):
        return self.m_heads * M_HEAD_DIM

    @property
    def na_width(self):
        return self.na_heads * NA_HEAD_DIM

    @property
    def n_gates(self):
        return 4 * self.m_heads

    @property
    def n_ctx(self):
        return self.batch * self.seq

    @property
    def n_lat(self):
        return self.dec_batch * self.dec_seq

    @property
    def n_tok(self):
        return self.n_ctx + self.n_lat

    @property
    def col_ca(self):
        return 4 * self.m_width

    @property
    def col_kv(self):
        return self.col_ca + 2 * self.cv_width

    @property
    def col_qn(self):
        return self.col_kv + 2 * self.na_width

    @property
    def col_gates(self):
        return self.col_qn + self.na_width

    @property
    def in_cols(self):
        raw = self.col_gates + LANES
        return -(-raw // self.tn_in) * self.tn_in


def _cparams(sem, vmem_limit=VMEM_LIMIT):
    return pltpu.CompilerParams(dimension_semantics=sem, vmem_limit_bytes=vmem_limit)


def _rms(x):
    return x * lax.rsqrt(jnp.mean(x * x, axis=-1, keepdims=True) + EPS)


def _dot(a, b):
    return jnp.dot(a, b, preferred_element_type=F32)


def _dot_nt(a, b):
    return lax.dot_general(a, b, (((1,), (1,)), ((), ())), preferred_element_type=F32)


def _dot_tn(a, b):
    return lax.dot_general(a, b, (((0,), (0,)), ((), ())), preferred_element_type=F32)


def _mod_kernel(c_ref, w_ref, b_ref, o_ref):
    c = c_ref[...]
    s = (c * jax.nn.sigmoid(c)).astype(BF16)
    o_ref[...] = _dot(s, w_ref[...].astype(BF16)) + b_ref[...]


def _modulation(cond, w_ada, b_ada, tn):
    depth, d, n6 = w_ada.shape
    rows = cond.shape[0]
    return pl.pallas_call(
        _mod_kernel,
        out_shape=jax.ShapeDtypeStruct((depth, rows, n6), F32),
        grid=(depth, n6 // tn),
        in_specs=[
            pl.BlockSpec((rows, d), lambda l, j: (0, 0)),
            pl.BlockSpec((None, d, tn), lambda l, j: (l, 0, j)),
            pl.BlockSpec((None, 1, tn), lambda l, j: (l, 0, j)),
        ],
        out_specs=pl.BlockSpec((None, rows, tn), lambda l, j: (l, 0, j)),
        compiler_params=_cparams(("arbitrary", "arbitrary")),
        name="adaln_modulation",
    )(cond, w_ada, b_ada.reshape(depth, 1, n6))


def _mod_row_map(cfg, which, tm, tile0=0):
    assert cfg.n_ctx % tm == 0 and cfg.dec_seq % tm == 0
    nct = cfg.n_ctx // tm
    tpb = cfg.dec_seq // tm

    def index_map(i, *_):
        t = tile0 + i
        row = jnp.where(t < nct, 0, 1 + (t - nct) // tpb)
        return (row, which, 0, 0)

    return index_map


def _mod_spec(cfg, which, tm, tile0=0):
    return pl.BlockSpec((None, None, 1, cfg.d_model), _mod_row_map(cfg, which, tm, tile0))


ROW_CHUNK = 128


def _rope(x, cos, sin_signed, even_block):
    swapped = jnp.where(even_block, pltpu.roll(x, 96, 1), pltpu.roll(x, 32, 1))
    return x * cos + swapped * sin_signed


IN_ROWS = 512


def _inproj_kernel(x_ref, sh_ref, sc_ref, g_ref, w_ref, wgt_ref, cos_ref, sin_ref, o_ref, gt_ref, kv_ref,
                   h_scr, acc_scr, *, q_tiles, k_tiles, kv_tile, n_ctx_tiles):
    i = pl.program_id(0)
    j = pl.program_id(1)
    tm, tn = o_ref.shape

    def norm_rows(rows):
        h = _rms(x_ref[rows, :]) * g_ref[...] * (1.0 + sc_ref[...]) + sh_ref[...]
        h_scr[rows, :] = h.astype(BF16)

    def matmul_rope(rows):
        acc_scr[rows, :] = _dot(h_scr[rows, :], w_ref[...])
        scale = jnp.where(j >= q_tiles, M_HEAD_DIM ** -0.5, 1.0).astype(F32)
        lane_blk = lax.broadcasted_iota(jnp.int32, (ROW_CHUNK, M_HEAD_DIM), 1) // 32
        even_block = (lane_blk % 2) == 0
        for r in range(rows.start, rows.stop, ROW_CHUNK):
            chunk = slice(r, r + ROW_CHUNK)
            cos, sin = cos_ref[chunk, :], sin_ref[chunk, :]
            for hh in range(tn // M_HEAD_DIM):
                cols = slice(hh * M_HEAD_DIM, (hh + 1) * M_HEAD_DIM)
                o_ref[chunk, cols] = (_rope(acc_scr[chunk, cols], cos, sin, even_block) * scale).astype(o_ref.dtype)

    @pl.when(j == 0)
    def _():
        blk_rows = min(IN_ROWS, tm)
        for blk in range(tm // blk_rows):
            for r in range(blk * blk_rows, (blk + 1) * blk_rows, ROW_CHUNK):
                norm_rows(slice(r, r + ROW_CHUNK))
            matmul_rope(slice(blk * blk_rows, (blk + 1) * blk_rows))
        gt_ref[...] = _dot_nt(wgt_ref[...], h_scr[...])

    @pl.when((j > 0) & (j < q_tiles + k_tiles))
    def _():
        matmul_rope(slice(0, tm))

    @pl.when((j >= q_tiles + k_tiles) & (j != kv_tile))
    def _():
        o_ref[...] = _dot(h_scr[...], w_ref[...]).astype(o_ref.dtype)

    @pl.when(j == kv_tile)
    def _():
        acc_scr[...] = _dot(h_scr[...], w_ref[...])
        o_ref[...] = acc_scr[...].astype(o_ref.dtype)

        @pl.when(i < n_ctx_tiles)
        def _():
            kv_ref[...] = acc_scr[...]


def _inproj(cfg, x, modl, g, w_in_p, w_gt, rope_tabs):
    d, tm, tn = cfg.d_model, cfg.tm_in, cfg.tn_in
    ng = cfg.n_gates
    nct = cfg.n_ctx // tm
    assert cfg.m_width % tn == 0 and cfg.col_kv % tn == 0 and 2 * cfg.na_width == tn and nct * tm == cfg.n_ctx
    kern = functools.partial(_inproj_kernel, q_tiles=cfg.m_width // tn, k_tiles=cfg.m_width // tn,
                             kv_tile=cfg.col_kv // tn, n_ctx_tiles=nct)
    return pl.pallas_call(
        kern,
        out_shape=(jax.ShapeDtypeStruct((cfg.n_tok, cfg.in_cols), BF16),
                   jax.ShapeDtypeStruct((ng, cfg.n_tok), F32),
                   jax.ShapeDtypeStruct((cfg.n_ctx, tn), F32)),
        grid=(cfg.n_tok // tm, cfg.in_cols // tn),
        in_specs=[
            pl.BlockSpec((tm, d), lambda i, j: (i, 0)),
            _mod_spec(cfg, 0, tm),
            _mod_spec(cfg, 1, tm),
            pl.BlockSpec((1, d), lambda i, j: (0, 0)),
            pl.BlockSpec((d, tn), lambda i, j: (0, j)),
            pl.BlockSpec((ng, d), lambda i, j: (0, 0)),
            pl.BlockSpec((tm, M_HEAD_DIM), lambda i, j: (i, 0)),
            pl.BlockSpec((tm, M_HEAD_DIM), lambda i, j: (i, 0)),
        ],
        out_specs=(pl.BlockSpec((tm, tn), lambda i, j: (i, j)),
                   pl.BlockSpec((ng, tm), lambda i, j: (0, i)),
                   pl.BlockSpec((tm, tn), lambda i, j: (jnp.minimum(i, nct - 1), 0))),
        scratch_shapes=[pltpu.VMEM((tm, d), BF16), pltpu.VMEM((tm, tn), F32)],
        compiler_params=_cparams(("arbitrary", "arbitrary")),
        name="in_projection",
    )(x, modl, modl, g.reshape(1, d), w_in_p, w_gt, *rope_tabs)


ROW_A, ROW_M, ROW_E, ROW_WK, ROW_MPREV, ROW_DECAY = range(6)
LOG2_E = 1.4426950408889634
FINISH_ROWS = 512


def _log_sigmoid(x):
    return jnp.minimum(x, 0.0) - jnp.log1p(jnp.exp(-jnp.abs(x)))


def _cumsum_lanes(x, cum_mat):
    hi = x.astype(BF16)
    r1 = x - hi.astype(F32)
    mid = r1.astype(BF16)
    lo = (r1 - mid.astype(F32)).astype(BF16)
    return _dot(hi, cum_mat) + _dot(mid, cum_mat) + _dot(lo, cum_mat)


def _cummax_lanes_both(xs):
    fwd, bwd = xs
    n = fwd.shape[1]
    lane = lax.broadcasted_iota(jnp.int32, fwd.shape, 1)
    k = 1
    while k < n:
        fwd = jnp.maximum(fwd, jnp.where(lane >= k, pltpu.roll(fwd, k, 1), -jnp.inf))
        bwd = jnp.maximum(bwd, jnp.where(lane < n - k, pltpu.roll(bwd, n - k, 1), -jnp.inf))
        k *= 2
    return [fwd, bwd]


def _col_bcast(row, n):
    return jnp.transpose(jnp.broadcast_to(row, (n, row.shape[1])))


def _mlstm_kernel(*refs, seq_len, n_sub, heads, unroll, init_state, emit_state, aliased):
    it = iter(refs)
    q_ref, k_ref, v_ref, om_ref, gt_ref, bias_ref, gm_ref = (next(it) for _ in range(7))
    c0_ref = n0_ref = m0_ref = None
    if init_state:
        c0_ref, n0_ref, m0_ref = next(it), next(it), next(it)
    if aliased:
        next(it)
    o_ref = next(it)
    co_ref = no_ref = mo_ref = None
    if emit_state:
        co_ref, no_ref, mo_ref = next(it), next(it), next(it)
    hs_f, hs_b, cext, rows_scr = next(it), next(it), next(it), next(it)

    L = dk = M_HEAD_DIM
    head = pl.program_id(1)
    nc = seq_len // L

    tt = lax.broadcasted_iota(jnp.int32, (L, L), 0)
    ss = lax.broadcasted_iota(jnp.int32, (L, L), 1)
    cum_mats = ((tt <= ss).astype(BF16), (tt >= ss).astype(BF16))
    causal = (ss <= tt, ss >= tt)

    dirs = (0, 1)
    row_i = [d * 2 * heads + head for d in dirs]
    xi = [gt_ref[r] + bias_ref[r] for r in row_i]
    lf = [_log_sigmoid(gt_ref[r + heads] + bias_ref[r + heads]) for r in row_i]
    b = [_cumsum_lanes(lf[d], cum_mats[d]) for d in dirs]
    tot = [jnp.sum(lf[d], axis=1, keepdims=True) for d in dirs]
    a = [xi[d] - b[d] for d in dirs]
    g = [tot[d] - b[d] + xi[d] for d in dirs]
    gmax = [jnp.max(g[d], axis=1, keepdims=True) for d in dirs]
    arun = _cummax_lanes_both(a)
    a_l2 = [a[d] * LOG2_E for d in dirs]
    for s in range(n_sub):
        m = [m0_ref[s, d][:, 0:1] if init_state else jnp.zeros((1, 1), F32) for d in dirs]
        for step in range(nc):
            for d in dirs:
                r = s * nc + (step if d == 0 else nc - 1 - step)
                m_new = jnp.maximum(tot[d][r:r + 1] + m[d], gmax[d][r:r + 1])
                m_row = jnp.maximum(arun[d][r:r + 1], m[d])
                rows_scr[d, r, ROW_A:ROW_A + 1, :] = a_l2[d][r:r + 1]
                rows_scr[d, r, ROW_M:ROW_M + 1, :] = m_row * LOG2_E
                rows_scr[d, r, ROW_E:ROW_E + 1, :] = -(b[d][r:r + 1] + m_row) * LOG2_E
                rows_scr[d, r, ROW_WK:ROW_WK + 1, :] = jnp.exp(g[d][r:r + 1] - m_new)
                rows_scr[d, r, ROW_MPREV:ROW_MPREV + 1, :] = jnp.broadcast_to(m[d] * LOG2_E, (1, L))
                rows_scr[d, r, ROW_DECAY:ROW_DECAY + 1, :] = jnp.broadcast_to(
                    jnp.exp(tot[d][r:r + 1] + m[d] - m_new), (1, L))
                m[d] = m_new
        for d in dirs:
            if emit_state:
                mo_ref[s, d] = jnp.broadcast_to(m[d], (1, LANES))
            if init_state:
                cext[s, d, :, 0:dk] = c0_ref[s, d]
                cext[s, d, :, dk:2 * dk] = _col_bcast(n0_ref[s, d], dk)
            else:
                cext[s, d] = jnp.zeros((dk, 2 * dk), F32)

    ones_ext = jnp.ones((L, dk), BF16)

    def prepare(s, d, c):
        rows = pl.ds(pl.multiple_of(s * seq_len + c * L, L), L)
        tile = rows_scr[d, s * nc + c]
        q = q_ref[rows, :].astype(BF16)
        k = k_ref[rows, :]
        v_ext = jnp.concatenate([v_ref[rows, :].astype(BF16), ones_ext], axis=1)
        m_b = _col_bcast(tile[ROW_M:ROW_M + 1], dk)
        floor = jnp.exp2(_col_bcast(tile[ROW_E:ROW_E + 1], dk))
        dmat = jnp.exp2(jnp.where(causal[d], tile[ROW_A:ROW_A + 1] - m_b, -jnp.inf))
        sc = (_dot_nt(q, k.astype(BF16)) * dmat).astype(BF16)
        w_int = jnp.exp2(tile[ROW_MPREV:ROW_MPREV + 1] - m_b)
        lhs = jnp.concatenate([sc, (w_int * q.astype(F32)).astype(BF16)], axis=1)
        kt_w = (jnp.transpose(k.astype(F32)) * tile[ROW_WK:ROW_WK + 1]).astype(BF16)
        return rows, lhs, v_ext, floor, kt_w, tile[ROW_DECAY:ROW_DECAY + 1, 0:1]

    def apply(s, d, prepared):
        rows, lhs, v_ext, floor, kt_w, decay = prepared
        c_prev = cext[s, d]
        nd = _dot(lhs, jnp.concatenate([v_ext, c_prev.astype(BF16)], axis=0))
        den = jnp.maximum(jnp.abs(nd[:, dk:2 * dk]), floor)
        (hs_f if d == 0 else hs_b)[rows, :] = nd[:, 0:dk] / den
        cext[s, d] = decay * c_prev + _dot(kt_w, v_ext)

    def chunk_body(i, carry):
        todo = []
        for u in range(unroll):
            c = i * unroll + u
            for s in range(n_sub):
                todo.append((s, 0, prepare(s, 0, c)))
                todo.append((s, 1, prepare(s, 1, nc - 1 - c)))
        for s, d, prepared in todo:
            apply(s, d, prepared)
        return carry

    lax.fori_loop(0, nc // unroll, chunk_body, 0)

    def finish(r, carry):
        rows = pl.ds(pl.multiple_of(r * FINISH_ROWS, FINISH_ROWS), FINISH_ROWS)
        y = _rms(hs_f[rows, :] + hs_b[rows, :]) * gm_ref[...] * jax.nn.sigmoid(om_ref[rows, :].astype(F32))
        o_ref[rows, :] = y.astype(o_ref.dtype)
        return carry

    lax.fori_loop(0, n_sub * seq_len // FINISH_ROWS, finish, 0)

    if emit_state:
        for s in range(n_sub):
            for d in range(2):
                co_ref[s, d] = cext[s, d, :, 0:dk]
                no_ref[s, d] = jnp.transpose(cext[s, d, :, dk:2 * dk])[0:1, :]


def _mlstm(cfg, p, gt, bias_rows, g_mlstm, *, n_seq, seq_len, row0, n_sub, unroll, name, state0=None,
           emit_state=False, into=None):
    heads, dk = cfg.m_heads, M_HEAD_DIM
    L = dk
    nc = seq_len // L
    blk_rows = n_sub * seq_len
    sb = row0 // blk_rows
    assert sb * blk_rows == row0 and nc * L == seq_len and n_seq % n_sub == 0 and nc % unroll == 0
    assert (n_sub * nc) % 8 == 0
    n_steps = n_seq // n_sub

    def col(block0):
        return pl.BlockSpec((blk_rows, dk), lambda b, h: (sb + b, block0 + h))

    gt4 = gt[:, row0:row0 + n_seq * seq_len].reshape(cfg.n_gates, n_steps, n_sub * nc, L)
    in_specs = [col(0), col(heads), col(2 * heads), col(3 * heads),
                pl.BlockSpec((cfg.n_gates, None, n_sub * nc, L), lambda b, h: (0, b, 0, 0)),
                pl.BlockSpec((cfg.n_gates, 1, L), lambda b, h: (0, 0, 0)),
                pl.BlockSpec((None, 1, dk), lambda b, h: (h, 0, 0))]
    args = [p, p, p, p, gt4, bias_rows, g_mlstm.reshape(heads, 1, dk)]
    state_spec = lambda r, c: pl.BlockSpec((n_sub, 2, None, r, c), lambda b, h: (b, 0, h, 0, 0))
    if state0 is not None:
        c0, n0, m0 = state0
        in_specs += [state_spec(dk, dk), state_spec(1, dk), state_spec(1, LANES)]
        args += [c0, n0.reshape(n_seq, 2, heads, 1, dk),
                 jnp.broadcast_to(m0[..., None, None], (n_seq, 2, heads, 1, LANES))]
    aliases = {}
    if into is not None:
        aliases = {len(args): 0}
        in_specs.append(pl.BlockSpec(memory_space=pl.ANY))
        args.append(into)
    out_shape = [jax.ShapeDtypeStruct((cfg.n_tok, cfg.m_width), BF16)]
    out_specs = [pl.BlockSpec((blk_rows, dk), lambda b, h: (sb + b, h))]
    if emit_state:
        out_shape += [jax.ShapeDtypeStruct((n_seq, 2, heads, dk, dk), F32),
                      jax.ShapeDtypeStruct((n_seq, 2, heads, 1, dk), F32),
                      jax.ShapeDtypeStruct((n_seq, 2, heads, 1, LANES), F32)]
        out_specs += [state_spec(dk, dk), state_spec(1, dk), state_spec(1, LANES)]
    kern = functools.partial(_mlstm_kernel, seq_len=seq_len, n_sub=n_sub, heads=heads, unroll=unroll,
                             init_state=state0 is not None, emit_state=emit_state, aliased=into is not None)
    return pl.pallas_call(
        kern,
        out_shape=tuple(out_shape),
        grid=(n_steps, heads),
        in_specs=in_specs,
        out_specs=tuple(out_specs),
        input_output_aliases=aliases,
        scratch_shapes=[pltpu.VMEM((blk_rows, dk), F32), pltpu.VMEM((blk_rows, dk), F32),
                        pltpu.VMEM((n_sub, 2, dk, 2 * dk), F32), pltpu.VMEM((2, n_sub * nc, 8, L), F32)],
        compiler_params=_cparams(("arbitrary", "arbitrary")),
        name=name,
    )(*args)


def _rope_tables(cfg):
    half = M_HEAD_DIM // 2
    nf = half // 2
    t = jnp.arange(cfg.dec_seq)
    inv = ROPE_BASE ** (-jnp.arange(nf, dtype=F32) / nf)
    ang_r = (t // GRID_W).astype(F32)[:, None] * inv[None, :]
    ang_c = (t % GRID_W).astype(F32)[:, None] * inv[None, :]
    cos = jnp.concatenate([jnp.cos(ang_r)] * 2 + [jnp.cos(ang_c)] * 2, axis=-1)
    sin = jnp.concatenate([-jnp.sin(ang_r), jnp.sin(ang_r), -jnp.sin(ang_c), jnp.sin(ang_c)], axis=-1)
    cos = jnp.concatenate([jnp.ones((cfg.n_ctx, M_HEAD_DIM), F32), jnp.tile(cos, (cfg.dec_batch, 1))], axis=0)
    sin = jnp.concatenate([jnp.zeros((cfg.n_ctx, M_HEAD_DIM), F32), jnp.tile(sin, (cfg.dec_batch, 1))], axis=0)
    return cos, sin


CONV_ROWS = 32


def _conv_kernel(a_ref, g_ref, ap_ref, gp_ref, an_ref, gn_ref, wdw_ref, bdw_ref, lng_ref, lnb_ref, wpw_ref,
                 o_ref, u_scr, *, n_ctx_tiles, ctx_tiles_per_seq, lat_tiles_per_seq):
    i = pl.program_id(0)
    tt = a_ref.shape[0]
    halo = CONV_HALO
    pos = jnp.where(i < n_ctx_tiles, i % ctx_tiles_per_seq, (i - n_ctx_tiles) % lat_tiles_per_seq)
    per_seq = jnp.where(i < n_ctx_tiles, ctx_tiles_per_seq, lat_tiles_per_seq)
    keep_prev = (pos > 0).astype(F32)
    keep_next = (pos < per_seq - 1).astype(F32)

    glu = lambda a, g: a[...].astype(F32) * jax.nn.sigmoid(g[...].astype(F32))
    u_scr[0, 0:halo, :] = glu(ap_ref, gp_ref) * keep_prev
    u_scr[0, halo:halo + tt, :] = glu(a_ref, g_ref)
    u_scr[0, halo + tt:2 * halo + tt, :] = glu(an_ref, gn_ref) * keep_next
    n_rows = tt + 2 * halo
    window = u_scr[0]
    for r in range(1, SUBLANES):
        u_scr[r] = pltpu.roll(window, n_rows - r, 0)

    for blk in range(tt // CONV_ROWS):
        base = blk * CONV_ROWS
        acc = jnp.broadcast_to(bdw_ref[...], (CONV_ROWS, a_ref.shape[1]))
        for tap in range(CONV_TAPS):
            start = base + halo - CONV_TAPS // 2 + tap
            r, aligned = start % SUBLANES, start - start % SUBLANES
            w_tap = jnp.concatenate([wdw_ref[tap]] * (CONV_ROWS // SUBLANES), axis=0)
            acc = acc + u_scr[r, aligned:aligned + CONV_ROWS, :] * w_tap
        mu = jnp.mean(acc, axis=-1, keepdims=True)
        cen = acc - mu
        var = jnp.mean(cen * cen, axis=-1, keepdims=True)
        y = cen * lax.rsqrt(var + EPS) * lng_ref[...] + lnb_ref[...]
        y = y * jax.nn.sigmoid(y)
        o_ref[base:base + CONV_ROWS, :] = _dot(y.astype(BF16), wpw_ref[...]).astype(o_ref.dtype)


def _conv_module(cfg, p, w_dw, b_dw, ln_g, ln_b, w_pw):
    tt, cv, halo = cfg.conv_tt, cfg.cv_width, CONV_HALO
    n_tiles = cfg.n_tok // tt
    cb = cfg.col_ca // cv
    hb = tt // halo
    last_halo = cfg.n_tok // halo - 1
    main = lambda c: pl.BlockSpec((tt, cv), lambda i: (i, c))
    prev = lambda c: pl.BlockSpec((halo, cv), lambda i: (jnp.maximum(i * hb - 1, 0), c))
    nxt = lambda c: pl.BlockSpec((halo, cv), lambda i: (jnp.minimum((i + 1) * hb, last_halo), c))
    vec = lambda: pl.BlockSpec((1, cv), lambda i: (0, 0))
    kern = functools.partial(_conv_kernel, n_ctx_tiles=cfg.n_ctx // tt, ctx_tiles_per_seq=cfg.seq // tt,
                             lat_tiles_per_seq=cfg.dec_seq // tt)
    return pl.pallas_call(
        kern,
        out_shape=jax.ShapeDtypeStruct((cfg.n_tok, cv), BF16),
        grid=(n_tiles,),
        in_specs=[main(cb), main(cb + 1), prev(cb), prev(cb + 1), nxt(cb), nxt(cb + 1),
                  pl.BlockSpec((CONV_TAPS, SUBLANES, cv), lambda i: (0, 0, 0)), vec(), vec(), vec(),
                  pl.BlockSpec((cv, cv), lambda i: (0, 0))],
        out_specs=pl.BlockSpec((tt, cv), lambda i: (i, 0)),
        scratch_shapes=[pltpu.VMEM((SUBLANES, tt + 2 * halo, cv), F32)],
        compiler_params=_cparams(("arbitrary",)),
        name="conv_module",
    )(p, p, p, p, p, p, jnp.broadcast_to(w_dw[:, None, :], (CONV_TAPS, SUBLANES, cv)),
      b_dw.reshape(1, cv), ln_g.reshape(1, cv), ln_b.reshape(1, cv), w_pw)


def _softmax_pv(scores, values):
    mx = functools.reduce(jnp.maximum, [jnp.max(s, axis=1, keepdims=True) for s in scores])
    ps = [jnp.exp(s - mx) for s in scores]
    den = functools.reduce(jnp.add, [jnp.sum(p, axis=1, keepdims=True) for p in ps])
    num = functools.reduce(jnp.add, [_dot(p.astype(BF16), v) for p, v in zip(ps, values)])
    return num / den


def _head_pair(q2, scores_fn, values):
    first_half = lax.broadcasted_iota(jnp.int32, q2.shape, 1) < NA_HEAD_DIM
    zero = jnp.zeros_like(q2)
    out_even = _softmax_pv(scores_fn(jnp.where(first_half, q2, zero), 0), values)
    out_odd = _softmax_pv(scores_fn(jnp.where(first_half, zero, q2), 1), values)
    return jnp.where(first_half, out_even, out_odd)


def _ctx_attn_kernel(q_ref, k_ref, v_ref, o_ref, *, heads):
    dh = NA_HEAD_DIM
    q = (q_ref[...] * (dh ** -0.5)).astype(BF16)
    k = k_ref[...].astype(BF16)
    v = v_ref[...].astype(BF16)
    outs = []
    for hp in range(heads // 2):
        sl = slice(hp * LANES, (hp + 1) * LANES)
        outs.append(_head_pair(q[:, sl], lambda qh, half: [_dot_nt(qh, k[:, sl])], [v[:, sl]]))
    o_ref[...] = jnp.concatenate(outs, axis=1).astype(o_ref.dtype)


def _ctx_attention(cfg, p):
    s, nw = cfg.seq, cfg.na_width
    qb, kb = cfg.col_qn // nw, cfg.col_kv // nw
    spec = lambda c: pl.BlockSpec((s, nw), lambda b: (b, c))
    return pl.pallas_call(
        functools.partial(_ctx_attn_kernel, heads=cfg.na_heads),
        out_shape=jax.ShapeDtypeStruct((cfg.n_tok, nw), BF16),
        grid=(cfg.batch,),
        in_specs=[spec(qb), spec(kb), spec(kb + 1)],
        out_specs=pl.BlockSpec((s, nw), lambda b: (b, 0)),
        compiler_params=_cparams(("arbitrary",)),
        name="context_attention",
    )(p, p, p)


NA_QROWS = 8


def _na_kernel(q_ref, kp_ref, kc_ref, kn_ref, vp_ref, vc_ref, vn_ref, kx_ref, vx_ref, bt_ref, into_ref, o_ref,
               kw_scr, vw_scr, *, heads, grid_rows):
    del into_ref
    dh = NA_HEAD_DIM
    blk = NA_QROWS * GRID_W
    win = WIN_ROWS * GRID_W
    m = pl.program_id(1)
    for slot, (kr, vr) in enumerate(((kp_ref, vp_ref), (kc_ref, vc_ref), (kn_ref, vn_ref))):
        kw_scr[slot * blk:(slot + 1) * blk, :] = kr[...].astype(BF16)
        vw_scr[slot * blk:(slot + 1) * blk, :] = vr[...].astype(BF16)
    kx = kx_ref[...].astype(BF16)
    vx = vx_ref[...].astype(BF16)

    def row_body(i, carry):
        r = m * NA_QROWS + i
        rs = jnp.clip(r - WIN_ROWS // 2, 0, grid_rows - WIN_ROWS)
        start = rs - r + (WIN_ROWS - 1)
        off = pl.multiple_of((rs - NA_QROWS * (m - 1)) * GRID_W, GRID_W)
        qrows = pl.ds(pl.multiple_of(i * GRID_W, GRID_W), GRID_W)
        q = (q_ref[qrows, :].astype(F32) * (dh ** -0.5 * LOG2_E)).astype(BF16)
        kwin = kw_scr[pl.ds(off, win), :]
        vwin = vw_scr[pl.ds(off, win), :]
        first_half = lax.broadcasted_iota(jnp.int32, (GRID_W, LANES), 1) < dh
        pairs = [slice(hp * LANES, (hp + 1) * LANES) for hp in range(heads // 2)]
        scores = []
        for hp, sl in enumerate(pairs):
            q2 = q[:, sl]
            zero = jnp.zeros_like(q2)
            qm = jnp.concatenate([jnp.where(first_half, q2, zero), jnp.where(first_half, zero, q2)], axis=0)
            scores.append((_dot_nt(kwin[:, sl], qm) + bt_ref[start, hp],
                           _dot_nt(kx[:, sl], qm)))
        probs = []
        for s_loc, s_ctx in scores:
            mx = jnp.maximum(jnp.max(s_loc, axis=0, keepdims=True), jnp.max(s_ctx, axis=0, keepdims=True))
            p_loc = jnp.exp2(s_loc - mx)
            p_ctx = jnp.exp2(s_ctx - mx)
            inv = 1.0 / (jnp.sum(p_loc, axis=0, keepdims=True) + jnp.sum(p_ctx, axis=0, keepdims=True))
            probs.append((p_loc.astype(BF16), p_ctx.astype(BF16), _col_bcast(inv, LANES)))
        outs = []
        for (p_loc, p_ctx, inv_col), sl in zip(probs, pairs):
            pv = (_dot_tn(p_loc, vwin[:, sl]) + _dot_tn(p_ctx, vx[:, sl])) * inv_col
            outs.append(jnp.where(first_half, pv[0:GRID_W], pv[GRID_W:2 * GRID_W]))
        o_ref[qrows, :] = jnp.concatenate(outs, axis=1).astype(o_ref.dtype)
        return carry

    lax.fori_loop(0, NA_QROWS, row_body, 0)


def _na_bias_table(rpb):
    heads = rpb.shape[0]
    qc = np.arange(GRID_W)[:, None]
    kc = np.arange(GRID_W)[None, :]
    cs = np.clip(qc - WIN_COLS // 2, 0, GRID_W - WIN_COLS)
    valid = (kc >= cs) & (kc < cs + WIN_COLS)
    dc = np.clip(kc - qc + WIN_COLS - 1, 0, 2 * WIN_COLS - 2)
    dr = np.arange(WIN_ROWS)[:, None] + np.arange(WIN_ROWS)[None, :]
    tab = (rpb * LOG2_E)[:, dr][:, :, :, dc]
    tab = jnp.where(jnp.asarray(valid)[None, None, None], tab, NEG_INF)
    tab = tab.reshape(heads // 2, 2, WIN_ROWS, WIN_ROWS, GRID_W, GRID_W)
    tab = tab.transpose(2, 0, 3, 5, 1, 4)
    return tab.reshape(WIN_ROWS, heads // 2, WIN_ROWS * GRID_W, 2 * GRID_W)


def _na_latent(cfg, p, k_ctx, v_ctx, bias_tab, into):
    nw, heads = cfg.na_width, cfg.na_heads
    blk = NA_QROWS * GRID_W
    grid_rows = cfg.dec_seq // GRID_W
    nblk = grid_rows // NA_QROWS
    qb, kb = cfg.col_qn // nw, cfg.col_kv // nw
    rb0 = cfg.n_ctx // blk
    assert rb0 * blk == cfg.n_ctx and nblk * NA_QROWS == grid_rows

    def rows(shift, c):
        return pl.BlockSpec((blk, nw), lambda b, m: (rb0 + b * nblk + jnp.clip(m + shift, 0, nblk - 1), c))

    ctx_spec = pl.BlockSpec((None, cfg.past_len, nw), lambda b, m: (b, 0, 0))
    return pl.pallas_call(
        functools.partial(_na_kernel, heads=heads, grid_rows=grid_rows),
        out_shape=jax.ShapeDtypeStruct((cfg.n_tok, nw), BF16),
        grid=(cfg.dec_batch, nblk),
        in_specs=[rows(0, qb), rows(-1, kb), rows(0, kb), rows(1, kb),
                  rows(-1, kb + 1), rows(0, kb + 1), rows(1, kb + 1), ctx_spec, ctx_spec,
                  pl.BlockSpec((WIN_ROWS, heads // 2, WIN_ROWS * GRID_W, 2 * GRID_W), lambda b, m: (0, 0, 0, 0)),
                  pl.BlockSpec(memory_space=pl.ANY)],
        out_specs=pl.BlockSpec((blk, nw), lambda b, m: (rb0 + b * nblk + m, 0)),
        scratch_shapes=[pltpu.VMEM((3 * blk, nw), BF16), pltpu.VMEM((3 * blk, nw), BF16)],
        input_output_aliases={10: 0},
        compiler_params=_cparams(("arbitrary", "arbitrary")),
        name="neighbourhood_attention",
    )(p, p, p, p, p, p, p, k_ctx, v_ctx, bias_tab, into)


OUT_ROWS = 256


def _outproj_kernel(ym_ref, yc_ref, yn_ref, x_ref, w_ref, ga_ref, gpost_ref, gpre_ref, sc_ref, sh_ref,
                    xo_ref, ho_ref, y_scr, *, m_width, cv_width):
    a, b = m_width, m_width + cv_width
    for blk in range(x_ref.shape[0] // OUT_ROWS):
        blk_rows = slice(blk * OUT_ROWS, (blk + 1) * OUT_ROWS)
        y_scr[blk_rows, :] = (_dot(ym_ref[blk_rows, :], w_ref[0:a, :]) + _dot(yc_ref[blk_rows, :], w_ref[a:b, :])
                              + _dot(yn_ref[blk_rows, :], w_ref[b:, :]))
        for r in range(OUT_ROWS // ROW_CHUNK):
            rows = slice(blk * OUT_ROWS + r * ROW_CHUNK, blk * OUT_ROWS + (r + 1) * ROW_CHUNK)
            x = x_ref[rows, :] + ga_ref[...] * (_rms(y_scr[rows, :]) * gpost_ref[...])
            xo_ref[rows, :] = x
            ho_ref[rows, :] = (_rms(x) * gpre_ref[...] * (1.0 + sc_ref[...]) + sh_ref[...]).astype(BF16)


def _outproj(cfg, ym, yc, yn, x, w_out, modl, g_post, g_pre):
    d, tm = cfg.d_model, cfg.tm_out
    rowblk = lambda w: pl.BlockSpec((tm, w), lambda i: (i, 0))
    vec = lambda: pl.BlockSpec((1, d), lambda i: (0, 0))
    return pl.pallas_call(
        functools.partial(_outproj_kernel, m_width=cfg.m_width, cv_width=cfg.cv_width),
        out_shape=(jax.ShapeDtypeStruct((cfg.n_tok, d), F32), jax.ShapeDtypeStruct((cfg.n_tok, d), BF16)),
        grid=(cfg.n_tok // tm,),
        in_specs=[rowblk(cfg.m_width), rowblk(cfg.cv_width), rowblk(cfg.na_width), rowblk(d),
                  pl.BlockSpec((d, d), lambda i: (0, 0)),
                  _mod_spec(cfg, 2, tm), vec(), vec(), _mod_spec(cfg, 4, tm), _mod_spec(cfg, 3, tm)],
        out_specs=(rowblk(d), rowblk(d)),
        scratch_shapes=[pltpu.VMEM((tm, d), F32)],
        compiler_params=_cparams(("arbitrary",)),
        name="out_projection",
    )(ym, yc, yn, x, w_out, modl, g_post.reshape(1, d), g_pre.reshape(1, d), modl, modl)


FFN_HIDDEN = 1024


def _ffn_kernel(h_ref, x_ref, w1_ref, w2_ref, ga_ref, g_ref, o_ref):
    kk = pl.program_id(1)
    acc_ref = o_ref

    @pl.when(kk == 0)
    def _():
        acc_ref[...] = jnp.zeros_like(acc_ref)

    last = pl.num_programs(1) - 1
    tm = x_ref.shape[0]

    def accumulate(rows):
        step = min(FFN_HIDDEN, w1_ref.shape[1])
        for hb in range(0, w1_ref.shape[1], step):
            hidden = slice(hb, hb + step)
            a = jnp.maximum(_dot(h_ref[rows, :], w1_ref[:, hidden]), 0.0)
            acc_ref[rows, :] += _dot((a * a).astype(BF16), w2_ref[hidden, :])

    @pl.when(kk < last)
    def _():
        accumulate(slice(0, tm))

    @pl.when(kk == last)
    def _():
        blk_rows = min(OUT_ROWS, tm)
        for blk in range(tm // blk_rows):
            accumulate(slice(blk * blk_rows, (blk + 1) * blk_rows))
            for r in range(blk * blk_rows, (blk + 1) * blk_rows, ROW_CHUNK):
                rows = slice(r, r + ROW_CHUNK)
                o_ref[rows, :] = x_ref[rows, :] + ga_ref[...] * (_rms(acc_ref[rows, :]) * g_ref[...])


def _ffn(cfg, h, x, w1, w2, modl, g_post, row0=0, n_rows=None):
    d, tm, th = cfg.d_model, cfg.tm_ff, cfg.th_ff
    n_rows = cfg.n_tok if n_rows is None else n_rows
    tile0 = row0 // tm
    assert tile0 * tm == row0 and n_rows % tm == 0
    return pl.pallas_call(
        _ffn_kernel,
        out_shape=jax.ShapeDtypeStruct((n_rows, d), F32),
        grid=(n_rows // tm, cfg.d_ff // th),
        in_specs=[pl.BlockSpec((tm, d), lambda i, k: (tile0 + i, 0)),
                  pl.BlockSpec((tm, d), lambda i, k: (tile0 + i, 0)),
                  pl.BlockSpec((d, th), lambda i, k: (0, k)),
                  pl.BlockSpec((th, d), lambda i, k: (k, 0)),
                  _mod_spec(cfg, 5, tm, tile0),
                  pl.BlockSpec((1, d), lambda i, k: (0, 0))],
        out_specs=pl.BlockSpec((tm, d), lambda i, k: (i, 0)),
        compiler_params=_cparams(("arbitrary", "arbitrary"), VMEM_LIMIT_MLP),
        name="relu2_mlp",
    )(h, x, w1, w2, modl, g_post.reshape(1, d))


def _prep_w_in(cfg, w_in_l):
    mw, cv, nw, ng = cfg.m_width, cfg.cv_width, cfg.na_width, cfg.n_gates
    g0 = 4 * mw
    c0 = g0 + ng
    n0 = c0 + 2 * cv
    gates = w_in_l[:, g0:c0]
    parts = [w_in_l[:, :g0], w_in_l[:, c0:n0], w_in_l[:, n0 + nw:n0 + 3 * nw], w_in_l[:, n0:n0 + nw], gates]
    pad = cfg.in_cols - (cfg.col_gates + ng)
    w_p = jnp.concatenate(parts + [jnp.zeros((cfg.d_model, pad), w_in_l.dtype)], axis=1)
    return w_p.astype(BF16), gates.T.astype(BF16)


def _forward(cfg, x_prompt, x_sample, c, cache_k, cache_v, state_C, state_n, state_m, c_ctx, w_ada, b_ada,
             g_pre_mix, g_post_mix, g_pre_mlp, g_post_mlp, w_in, b_gate, g_mlstm, w_dw, b_dw, ln_g, ln_b,
             w_pw, rpb, w_out, w_ff1, w_ff2):
    d = cfg.d_model
    x = jnp.concatenate([x_prompt.reshape(cfg.n_ctx, d), x_sample.reshape(cfg.n_lat, d)], axis=0)

    n_cond = 1 + cfg.dec_batch
    cond_rows = -(-n_cond // 8) * 8
    cond = jnp.concatenate([c_ctx[None, :], c, jnp.zeros((cond_rows - n_cond, d), F32)], axis=0)
    mod = _modulation(cond, w_ada, b_ada, tn=min(1024, 6 * d)).reshape(cfg.depth, cond_rows, 6, 1, d)

    rope_tabs = _rope_tables(cfg)
    nw = cfg.na_width
    ks, vs, cs, ns, ms = [], [], [], [], []
    for l in range(cfg.depth):
        w_in_p, w_gt = _prep_w_in(cfg, w_in[l])
        p, gt, kv_new = _inproj(cfg, x, mod[l], g_pre_mix[l], w_in_p, w_gt, rope_tabs)

        bias_rows = jnp.broadcast_to(b_gate[l].reshape(cfg.n_gates, 1, 1), (cfg.n_gates, 1, M_HEAD_DIM))
        ym_c, c_new, n_new, m_new = _mlstm(cfg, p, gt, bias_rows, g_mlstm[l], n_seq=cfg.batch, seq_len=cfg.seq,
                                           row0=0, n_sub=cfg.ctx_sub, unroll=1, name="mlstm_context",
                                           emit_state=True)
        (ym,) = _mlstm(cfg, p, gt, bias_rows, g_mlstm[l], n_seq=cfg.dec_batch, seq_len=cfg.dec_seq,
                       row0=cfg.n_ctx, n_sub=1, unroll=cfg.lat_unroll, name="mlstm_latent",
                       state0=(state_C[:, l], state_n[:, l], state_m[:, l]), into=ym_c)
        yc = _conv_module(cfg, p, w_dw[l], b_dw[l], ln_g[l], ln_b[l], w_pw[l].astype(BF16))
        yn = _na_latent(cfg, p, cache_k[:, l].reshape(cfg.dec_batch, cfg.past_len, nw),
                        cache_v[:, l].reshape(cfg.dec_batch, cfg.past_len, nw),
                        _na_bias_table(rpb[l]), into=_ctx_attention(cfg, p))

        x, h2 = _outproj(cfg, ym, yc, yn, x, w_out[l].astype(BF16), mod[l], g_post_mix[l], g_pre_mlp[l])
        ffn_args = (w_ff1[l].astype(BF16), w_ff2[l].astype(BF16), mod[l], g_post_mlp[l])
        if l + 1 < cfg.depth:
            x = _ffn(cfg, h2, x, *ffn_args)
        else:
            y_prompt = _ffn(cfg, h2, x, *ffn_args, row0=0, n_rows=cfg.n_ctx)
            y_sample = _ffn(cfg, h2, x, *ffn_args, row0=cfg.n_ctx, n_rows=cfg.n_lat)

        ks.append(kv_new[:, :nw].reshape(cfg.batch, cfg.seq, cfg.na_heads, NA_HEAD_DIM))
        vs.append(kv_new[:, nw:].reshape(cfg.batch, cfg.seq, cfg.na_heads, NA_HEAD_DIM))
        cs.append(c_new)
        ns.append(n_new[:, :, :, 0, :])
        ms.append(m_new[:, :, :, 0, 0])

    y_prompt = y_prompt.reshape(cfg.batch, cfg.seq, d)
    y_sample = y_sample.reshape(cfg.dec_batch, cfg.dec_seq, d)
    return (y_prompt, y_sample, jnp.stack(ks, axis=1), jnp.stack(vs, axis=1),
            jnp.stack(cs, axis=1), jnp.stack(ns, axis=1), jnp.stack(ms, axis=1))


def _make_cfg(x_prompt, x_sample, cache_k, w_in, w_ff1, w_pw, **tiles):
    batch, seq, d = x_prompt.shape
    dec_batch, dec_seq, _ = x_sample.shape
    return Cfg(d_model=d, batch=batch, seq=seq, depth=w_in.shape[0], dec_batch=dec_batch, dec_seq=dec_seq,
               past_len=cache_k.shape[2], m_heads=d // (2 * M_HEAD_DIM), cv_width=w_pw.shape[-1],
               na_heads=cache_k.shape[3], d_ff=w_ff1.shape[-1], **tiles)


def kernel(x_prompt, x_sample, c, cache_k, cache_v, state_C, state_n, state_m, c_ctx, w_ada, b_ada, g_pre_mix,
           g_post_mix, g_pre_mlp, g_post_mlp, w_in, b_gate, g_mlstm, w_dw, b_dw, ln_g, ln_b, w_pw, rpb, w_out,
           w_ff1, w_ff2):
    cfg = _make_cfg(x_prompt, x_sample, cache_k, w_in, w_ff1, w_pw,
                    tm_in=1024, tm_out=512, tm_ff=512, tn_in=1024, th_ff=2048, ctx_sub=4, lat_unroll=4,
                    conv_tt=256)
    return _forward(cfg, x_prompt, x_sample, c, cache_k, cache_v, state_C, state_n, state_m, c_ctx, w_ada, b_ada,
                    g_pre_mix, g_post_mix, g_pre_mlp, g_post_mlp, w_in, b_gate, g_mlstm, w_dw, b_dw, ln_g, ln_b,
                    w_pw, rpb, w_out, w_ff1, w_ff2)
```

```python
import functools
from typing import NamedTuple

import numpy as np
import jax
import jax.numpy as jnp
from jax import lax
from jax.experimental import pallas as pl
from jax.experimental.pallas import tpu as pltpu

F32 = jnp.float32
BF16 = jnp.bfloat16

EPS = 1e-6
NEG_INF = -1e30
ROPE_BASE = 10000.0

LANES = 128
SUBLANES = 8
M_HEAD_DIM = 128
NA_HEAD_DIM = 64
GRID_W = 64
WIN_ROWS = 8
WIN_COLS = 16
CONV_TAPS = 31
CONV_HALO = 16
VMEM_LIMIT = 52 * 1024 * 1024
VMEM_LIMIT_MLP = 60 * 1024 * 1024


class Cfg(NamedTuple):
    d_model: int
    batch: int
    seq: int
    depth: int
    dec_batch: int
    dec_seq: int
    past_len: int
    m_heads: int
    cv_width: int
    na_heads: int
    d_ff: int
    tm_in: int
    tm_out: int
    tm_ff: int
    tn_in: int
    th_ff: int
    ctx_sub: int
    lat_unroll: int
    conv_tt: int

    @property
    def m_width(self):
        return self.m_heads * M_HEAD_DIM

    @property
    def na_width(self):
        return self.na_heads * NA_HEAD_DIM

    @property
    def n_gates(self):
        return 4 * self.m_heads

    @property
    def n_ctx(self):
        return self.batch * self.seq

    @property
    def n_lat(self):
        return self.dec_batch * self.dec_seq

    @property
    def n_tok(self):
        return self.n_ctx + self.n_lat

    @property
    def col_ca(self):
        return 4 * self.m_width

    @property
    def col_kv(self):
        return self.col_ca + 2 * self.cv_width

    @property
    def col_qn(self):
        return self.col_kv + 2 * self.na_width

    @property
    def col_gates(self):
        return self.col_qn + self.na_width

    @property
    def in_cols(self):
        raw = self.col_gates + LANES
        return -(-raw // self.tn_in) * self.tn_in


def _cparams(sem, vmem_limit=VMEM_LIMIT):
    return pltpu.CompilerParams(dimension_semantics=sem, vmem_limit_bytes=vmem_limit)


def _rms(x):
    return x * lax.rsqrt(jnp.mean(x * x, axis=-1, keepdims=True) + EPS)


def _dot(a, b):
    return jnp.dot(a, b, preferred_element_type=F32)


def _dot_nt(a, b):
    return lax.dot_general(a, b, (((1,), (1,)), ((), ())), preferred_element_type=F32)


def _dot_tn(a, b):
    return lax.dot_general(a, b, (((0,), (0,)), ((), ())), preferred_element_type=F32)


def _mod_kernel(c_ref, w_ref, b_ref, o_ref):
    c = c_ref[...]
    s = (c * jax.nn.sigmoid(c)).astype(BF16)
    o_ref[...] = _dot(s, w_ref[...].astype(BF16)) + b_ref[...]


def _modulation(cond, w_ada, b_ada, tn):
    depth, d, n6 = w_ada.shape
    rows = cond.shape[0]
    return pl.pallas_call(
        _mod_kernel,
        out_shape=jax.ShapeDtypeStruct((depth, rows, n6), F32),
        grid=(depth, n6 // tn),
        in_specs=[
            pl.BlockSpec((rows, d), lambda l, j: (0, 0)),
            pl.BlockSpec((None, d, tn), lambda l, j: (l, 0, j)),
            pl.BlockSpec((None, 1, tn), lambda l, j: (l, 0, j)),
        ],
        out_specs=pl.BlockSpec((None, rows, tn), lambda l, j: (l, 0, j)),
        compiler_params=_cparams(("arbitrary", "arbitrary")),
        name="adaln_modulation",
    )(cond, w_ada, b_ada.reshape(depth, 1, n6))


def _mod_row_map(cfg, which, tm, tile0=0):
    assert cfg.n_ctx % tm == 0 and cfg.dec_seq % tm == 0
    nct = cfg.n_ctx // tm
    tpb = cfg.dec_seq // tm

    def index_map(i, *_):
        t = tile0 + i
        row = jnp.where(t < nct, 0, 1 + (t - nct) // tpb)
        return (row, which, 0, 0)

    return index_map


def _mod_spec(cfg, which, tm, tile0=0):
    return pl.BlockSpec((None, None, 1, cfg.d_model), _mod_row_map(cfg, which, tm, tile0))


ROW_CHUNK = 128


def _rope(x, cos, sin_signed, even_block):
    swapped = jnp.where(even_block, pltpu.roll(x, 96, 1), pltpu.roll(x, 32, 1))
    return x * cos + swapped * sin_signed


IN_ROWS = 512


def _inproj_kernel(x_ref, sh_ref, sc_ref, g_ref, w_ref, wgt_ref, cos_ref, sin_ref, o_ref, gt_ref, kv_ref,
                   h_scr, acc_scr, *, q_tiles, k_tiles, kv_tile, n_ctx_tiles):
    i = pl.program_id(0)
    j = pl.program_id(1)
    tm, tn = o_ref.shape

    def norm_rows(rows):
        h = _rms(x_ref[rows, :]) * g_ref[...] * (1.0 + sc_ref[...]) + sh_ref[...]
        h_scr[rows, :] = h.astype(BF16)

    def matmul_rope(rows):
        acc_scr[rows, :] = _dot(h_scr[rows, :], w_ref[...])
        scale = jnp.where(j >= q_tiles, M_HEAD_DIM ** -0.5, 1.0).astype(F32)
        lane_blk = lax.broadcasted_iota(jnp.int32, (ROW_CHUNK, M_HEAD_DIM), 1) // 32
        even_block = (lane_blk % 2) == 0
        for r in range(rows.start, rows.stop, ROW_CHUNK):
            chunk = slice(r, r + ROW_CHUNK)
            cos, sin = cos_ref[chunk, :], sin_ref[chunk, :]
            for hh in range(tn // M_HEAD_DIM):
                cols = slice(hh * M_HEAD_DIM, (hh + 1) * M_HEAD_DIM)
                o_ref[chunk, cols] = (_rope(acc_scr[chunk, cols], cos, sin, even_block) * scale).astype(o_ref.dtype)

    @pl.when(j == 0)
    def _():
        blk_rows = min(IN_ROWS, tm)
        for blk in range(tm // blk_rows):
            for r in range(blk * blk_rows, (blk + 1) * blk_rows, ROW_CHUNK):
                norm_rows(slice(r, r + ROW_CHUNK))
            matmul_rope(slice(blk * blk_rows, (blk + 1) * blk_rows))
        gt_ref[...] = _dot_nt(wgt_ref[...], h_scr[...])

    @pl.when((j > 0) & (j < q_tiles + k_tiles))
    def _():
        blk_rows = min(IN_ROWS, tm)
        for blk in range(tm // blk_rows):
            matmul_rope(slice(blk * blk_rows, (blk + 1) * blk_rows))

    @pl.when((j >= q_tiles + k_tiles) & (j != kv_tile))
    def _():
        o_ref[...] = _dot(h_scr[...], w_ref[...]).astype(o_ref.dtype)

    @pl.when(j == kv_tile)
    def _():
        acc_scr[...] = _dot(h_scr[...], w_ref[...])
        o_ref[...] = acc_scr[...].astype(o_ref.dtype)

        @pl.when(i < n_ctx_tiles)
        def _():
            kv_ref[...] = acc_scr[...]


def _inproj(cfg, x, modl, g, w_in_p, w_gt, rope_tabs):
    d, tm, tn = cfg.d_model, cfg.tm_in, cfg.tn_in
    ng = cfg.n_gates
    nct = cfg.n_ctx // tm
    assert cfg.m_width % tn == 0 and cfg.col_kv % tn == 0 and 2 * cfg.na_width == tn and nct * tm == cfg.n_ctx
    kern = functools.partial(_inproj_kernel, q_tiles=cfg.m_width // tn, k_tiles=cfg.m_width // tn,
                             kv_tile=cfg.col_kv // tn, n_ctx_tiles=nct)
    return pl.pallas_call(
        kern,
        out_shape=(jax.ShapeDtypeStruct((cfg.n_tok, cfg.in_cols), BF16),
                   jax.ShapeDtypeStruct((ng, cfg.n_tok), F32),
                   jax.ShapeDtypeStruct((cfg.n_ctx, tn), F32)),
        grid=(cfg.n_tok // tm, cfg.in_cols // tn),
        in_specs=[
            pl.BlockSpec((tm, d), lambda i, j: (i, 0)),
            _mod_spec(cfg, 0, tm),
            _mod_spec(cfg, 1, tm),
            pl.BlockSpec((1, d), lambda i, j: (0, 0)),
            pl.BlockSpec((d, tn), lambda i, j: (0, j)),
            pl.BlockSpec((ng, d), lambda i, j: (0, 0)),
            pl.BlockSpec((tm, M_HEAD_DIM), lambda i, j: (i, 0)),
            pl.BlockSpec((tm, M_HEAD_DIM), lambda i, j: (i, 0)),
        ],
        out_specs=(pl.BlockSpec((tm, tn), lambda i, j: (i, j)),
                   pl.BlockSpec((ng, tm), lambda i, j: (0, i)),
                   pl.BlockSpec((tm, tn), lambda i, j: (jnp.minimum(i, nct - 1), 0))),
        scratch_shapes=[pltpu.VMEM((tm, d), BF16), pltpu.VMEM((tm, tn), F32)],
        compiler_params=_cparams(("arbitrary", "arbitrary")),
        name="in_projection",
    )(x, modl, modl, g.reshape(1, d), w_in_p, w_gt, *rope_tabs)


ROW_A, ROW_M, ROW_E, ROW_WK, ROW_MPREV, ROW_DECAY = range(6)
LOG2_E = 1.4426950408889634
FINISH_ROWS = 512


def _log_sigmoid(x):
    return jnp.minimum(x, 0.0) - jnp.log1p(jnp.exp(-jnp.abs(x)))


def _cumsum_lanes(x, cum_mat):
    hi = x.astype(BF16)
    r1 = x - hi.astype(F32)
    mid = r1.astype(BF16)
    lo = (r1 - mid.astype(F32)).astype(BF16)
    return _dot(hi, cum_mat) + _dot(mid, cum_mat) + _dot(lo, cum_mat)


def _cummax_lanes_both(xs):
    fwd, bwd = xs
    n = fwd.shape[1]
    lane = lax.broadcasted_iota(jnp.int32, fwd.shape, 1)
    k = 1
    while k < n:
        fwd = jnp.maximum(fwd, jnp.where(lane >= k, pltpu.roll(fwd, k, 1), -jnp.inf))
        bwd = jnp.maximum(bwd, jnp.where(lane < n - k, pltpu.roll(bwd, n - k, 1), -jnp.inf))
        k *= 2
    return [fwd, bwd]


def _col_bcast(row, n):
    return jnp.transpose(jnp.broadcast_to(row, (n, row.shape[1])))


def _mlstm_kernel(*refs, seq_len, n_sub, heads, unroll, init_state, emit_state, aliased):
    it = iter(refs)
    q_ref, k_ref, v_ref, om_ref, gt_ref, bias_ref, gm_ref = (next(it) for _ in range(7))
    c0_ref = n0_ref = m0_ref = None
    if init_state:
        c0_ref, n0_ref, m0_ref = next(it), next(it), next(it)
    if aliased:
        next(it)
    o_ref = next(it)
    co_ref = no_ref = mo_ref = None
    if emit_state:
        co_ref, no_ref, mo_ref = next(it), next(it), next(it)
    hs_f, hs_b, cext, rows_scr = next(it), next(it), next(it), next(it)

    L = dk = M_HEAD_DIM
    head = pl.program_id(1)
    nc = seq_len // L

    tt = lax.broadcasted_iota(jnp.int32, (L, L), 0)
    ss = lax.broadcasted_iota(jnp.int32, (L, L), 1)
    cum_mats = ((tt <= ss).astype(BF16), (tt >= ss).astype(BF16))
    causal = (ss <= tt, ss >= tt)

    dirs = (0, 1)
    row_i = [d * 2 * heads + head for d in dirs]
    xi = [gt_ref[r] + bias_ref[r] for r in row_i]
    lf = [_log_sigmoid(gt_ref[r + heads] + bias_ref[r + heads]) for r in row_i]
    b = [_cumsum_lanes(lf[d], cum_mats[d]) for d in dirs]
    tot = [jnp.sum(lf[d], axis=1, keepdims=True) for d in dirs]
    a = [xi[d] - b[d] for d in dirs]
    g = [tot[d] - b[d] + xi[d] for d in dirs]
    gmax = [jnp.max(g[d], axis=1, keepdims=True) for d in dirs]
    arun = _cummax_lanes_both(a)
    a_l2 = [a[d] * LOG2_E for d in dirs]
    for s in range(n_sub):
        m = [m0_ref[s, d][:, 0:1] if init_state else jnp.zeros((1, 1), F32) for d in dirs]
        for step in range(nc):
            for d in dirs:
                r = s * nc + (step if d == 0 else nc - 1 - step)
                m_new = jnp.maximum(tot[d][r:r + 1] + m[d], gmax[d][r:r + 1])
                m_row = jnp.maximum(arun[d][r:r + 1], m[d])
                rows_scr[d, r, ROW_A:ROW_A + 1, :] = a_l2[d][r:r + 1]
                rows_scr[d, r, ROW_M:ROW_M + 1, :] = m_row * LOG2_E
                rows_scr[d, r, ROW_E:ROW_E + 1, :] = -(b[d][r:r + 1] + m_row) * LOG2_E
                rows_scr[d, r, ROW_WK:ROW_WK + 1, :] = jnp.exp(g[d][r:r + 1] - m_new)
                rows_scr[d, r, ROW_MPREV:ROW_MPREV + 1, :] = jnp.broadcast_to(m[d] * LOG2_E, (1, L))
                rows_scr[d, r, ROW_DECAY:ROW_DECAY + 1, :] = jnp.broadcast_to(
                    jnp.exp(tot[d][r:r + 1] + m[d] - m_new), (1, L))
                m[d] = m_new
        for d in dirs:
            if emit_state:
                mo_ref[s, d] = jnp.broadcast_to(m[d], (1, LANES))
            if init_state:
                cext[s, d, :, 0:dk] = c0_ref[s, d]
                cext[s, d, :, dk:2 * dk] = _col_bcast(n0_ref[s, d], dk)
            else:
                cext[s, d] = jnp.zeros((dk, 2 * dk), F32)

    ones_ext = jnp.ones((L, dk), BF16)

    def prepare(s, d, c):
        rows = pl.ds(pl.multiple_of(s * seq_len + c * L, L), L)
        tile = rows_scr[d, s * nc + c]
        q = q_ref[rows, :].astype(BF16)
        k = k_ref[rows, :]
        v_ext = jnp.concatenate([v_ref[rows, :].astype(BF16), ones_ext], axis=1)
        m_b = _col_bcast(tile[ROW_M:ROW_M + 1], dk)
        floor = jnp.exp2(_col_bcast(tile[ROW_E:ROW_E + 1], dk))
        dmat = jnp.exp2(jnp.where(causal[d], tile[ROW_A:ROW_A + 1] - m_b, -jnp.inf))
        sc = (_dot_nt(q, k.astype(BF16)) * dmat).astype(BF16)
        w_int = jnp.exp2(tile[ROW_MPREV:ROW_MPREV + 1] - m_b)
        lhs = jnp.concatenate([sc, (w_int * q.astype(F32)).astype(BF16)], axis=1)
        kt_w = (jnp.transpose(k.astype(F32)) * tile[ROW_WK:ROW_WK + 1]).astype(BF16)
        return rows, lhs, v_ext, floor, kt_w, tile[ROW_DECAY:ROW_DECAY + 1, 0:1]

    def apply(s, d, prepared):
        rows, lhs, v_ext, floor, kt_w, decay = prepared
        c_prev = cext[s, d]
        nd = _dot(lhs, jnp.concatenate([v_ext, c_prev.astype(BF16)], axis=0))
        den = jnp.maximum(jnp.abs(nd[:, dk:2 * dk]), floor)
        (hs_f if d == 0 else hs_b)[rows, :] = nd[:, 0:dk] / den
        cext[s, d] = decay * c_prev + _dot(kt_w, v_ext)

    def chunk_body(i, carry):
        todo = []
        for u in range(unroll):
            c = i * unroll + u
            for s in range(n_sub):
                todo.append((s, 0, prepare(s, 0, c)))
                todo.append((s, 1, prepare(s, 1, nc - 1 - c)))
        for s, d, prepared in todo:
            apply(s, d, prepared)
        return carry

    lax.fori_loop(0, nc // unroll, chunk_body, 0)

    def finish(r, carry):
        rows = pl.ds(pl.multiple_of(r * FINISH_ROWS, FINISH_ROWS), FINISH_ROWS)
        y = _rms(hs_f[rows, :] + hs_b[rows, :]) * gm_ref[...] * jax.nn.sigmoid(om_ref[rows, :].astype(F32))
        o_ref[rows, :] = y.astype(o_ref.dtype)
        return carry

    lax.fori_loop(0, n_sub * seq_len // FINISH_ROWS, finish, 0)

    if emit_state:
        for s in range(n_sub):
            for d in range(2):
                co_ref[s, d] = cext[s, d, :, 0:dk]
                no_ref[s, d] = jnp.transpose(cext[s, d, :, dk:2 * dk])[0:1, :]


def _mlstm(cfg, p, gt, bias_rows, g_mlstm, *, n_seq, seq_len, row0, n_sub, unroll, name, state0=None,
           emit_state=False, into=None):
    heads, dk = cfg.m_heads, M_HEAD_DIM
    L = dk
    nc = seq_len // L
    blk_rows = n_sub * seq_len
    sb = row0 // blk_rows
    assert sb * blk_rows == row0 and nc * L == seq_len and n_seq % n_sub == 0 and nc % unroll == 0
    assert (n_sub * nc) % 8 == 0
    n_steps = n_seq // n_sub

    def col(block0):
        return pl.BlockSpec((blk_rows, dk), lambda b, h: (sb + b, block0 + h))

    gt4 = gt[:, row0:row0 + n_seq * seq_len].reshape(cfg.n_gates, n_steps, n_sub * nc, L)
    in_specs = [col(0), col(heads), col(2 * heads), col(3 * heads),
                pl.BlockSpec((cfg.n_gates, None, n_sub * nc, L), lambda b, h: (0, b, 0, 0)),
                pl.BlockSpec((cfg.n_gates, 1, L), lambda b, h: (0, 0, 0)),
                pl.BlockSpec((None, 1, dk), lambda b, h: (h, 0, 0))]
    args = [p, p, p, p, gt4, bias_rows, g_mlstm.reshape(heads, 1, dk)]
    state_spec = lambda r, c: pl.BlockSpec((n_sub, 2, None, r, c), lambda b, h: (b, 0, h, 0, 0))
    if state0 is not None:
        c0, n0, m0 = state0
        in_specs += [state_spec(dk, dk), state_spec(1, dk), state_spec(1, LANES)]
        args += [c0, n0.reshape(n_seq, 2, heads, 1, dk),
                 jnp.broadcast_to(m0[..., None, None], (n_seq, 2, heads, 1, LANES))]
    aliases = {}
    if into is not None:
        aliases = {len(args): 0}
        in_specs.append(pl.BlockSpec(memory_space=pl.ANY))
        args.append(into)
    out_shape = [jax.ShapeDtypeStruct((cfg.n_tok, cfg.m_width), BF16)]
    out_specs = [pl.BlockSpec((blk_rows, dk), lambda b, h: (sb + b, h))]
    if emit_state:
        out_shape += [jax.ShapeDtypeStruct((n_seq, 2, heads, dk, dk), F32),
                      jax.ShapeDtypeStruct((n_seq, 2, heads, 1, dk), F32),
                      jax.ShapeDtypeStruct((n_seq, 2, heads, 1, LANES), F32)]
        out_specs += [state_spec(dk, dk), state_spec(1, dk), state_spec(1, LANES)]
    kern = functools.partial(_mlstm_kernel, seq_len=seq_len, n_sub=n_sub, heads=heads, unroll=unroll,
                             init_state=state0 is not None, emit_state=emit_state, aliased=into is not None)
    return pl.pallas_call(
        kern,
        out_shape=tuple(out_shape),
        grid=(n_steps, heads),
        in_specs=in_specs,
        out_specs=tuple(out_specs),
        input_output_aliases=aliases,
        scratch_shapes=[pltpu.VMEM((blk_rows, dk), F32), pltpu.VMEM((blk_rows, dk), F32),
                        pltpu.VMEM((n_sub, 2, dk, 2 * dk), F32), pltpu.VMEM((2, n_sub * nc, 8, L), F32)],
        compiler_params=_cparams(("arbitrary", "arbitrary")),
        name=name,
    )(*args)


def _rope_tables(cfg):
    half = M_HEAD_DIM // 2
    nf = half // 2
    t = jnp.arange(cfg.dec_seq)
    inv = ROPE_BASE ** (-jnp.arange(nf, dtype=F32) / nf)
    ang_r = (t // GRID_W).astype(F32)[:, None] * inv[None, :]
    ang_c = (t % GRID_W).astype(F32)[:, None] * inv[None, :]
    cos = jnp.concatenate([jnp.cos(ang_r)] * 2 + [jnp.cos(ang_c)] * 2, axis=-1)
    sin = jnp.concatenate([-jnp.sin(ang_r), jnp.sin(ang_r), -jnp.sin(ang_c), jnp.sin(ang_c)], axis=-1)
    cos = jnp.concatenate([jnp.ones((cfg.n_ctx, M_HEAD_DIM), F32), jnp.tile(cos, (cfg.dec_batch, 1))], axis=0)
    sin = jnp.concatenate([jnp.zeros((cfg.n_ctx, M_HEAD_DIM), F32), jnp.tile(sin, (cfg.dec_batch, 1))], axis=0)
    return cos, sin


CONV_ROWS = 32


def _conv_kernel(a_ref, g_ref, ap_ref, gp_ref, an_ref, gn_ref, wdw_ref, bdw_ref, lng_ref, lnb_ref, wpw_ref,
                 o_ref, u_scr, *, n_ctx_tiles, ctx_tiles_per_seq, lat_tiles_per_seq):
    i = pl.program_id(0)
    tt = a_ref.shape[0]
    halo = CONV_HALO
    pos = jnp.where(i < n_ctx_tiles, i % ctx_tiles_per_seq, (i - n_ctx_tiles) % lat_tiles_per_seq)
    per_seq = jnp.where(i < n_ctx_tiles, ctx_tiles_per_seq, lat_tiles_per_seq)
    keep_prev = (pos > 0).astype(F32)
    keep_next = (pos < per_seq - 1).astype(F32)

    glu = lambda a, g: a[...].astype(F32) * jax.nn.sigmoid(g[...].astype(F32))
    u_scr[0, 0:halo, :] = glu(ap_ref, gp_ref) * keep_prev
    u_scr[0, halo:halo + tt, :] = glu(a_ref, g_ref)
    u_scr[0, halo + tt:2 * halo + tt, :] = glu(an_ref, gn_ref) * keep_next
    n_rows = tt + 2 * halo
    window = u_scr[0]
    for r in range(1, SUBLANES):
        u_scr[r] = pltpu.roll(window, n_rows - r, 0)

    for blk in range(tt // CONV_ROWS):
        base = blk * CONV_ROWS
        acc = jnp.broadcast_to(bdw_ref[...], (CONV_ROWS, a_ref.shape[1]))
        for tap in range(CONV_TAPS):
            start = base + halo - CONV_TAPS // 2 + tap
            r, aligned = start % SUBLANES, start - start % SUBLANES
            w_tap = jnp.concatenate([wdw_ref[tap]] * (CONV_ROWS // SUBLANES), axis=0)
            acc = acc + u_scr[r, aligned:aligned + CONV_ROWS, :] * w_tap
        mu = jnp.mean(acc, axis=-1, keepdims=True)
        cen = acc - mu
        var = jnp.mean(cen * cen, axis=-1, keepdims=True)
        y = cen * lax.rsqrt(var + EPS) * lng_ref[...] + lnb_ref[...]
        y = y * jax.nn.sigmoid(y)
        o_ref[base:base + CONV_ROWS, :] = _dot(y.astype(BF16), wpw_ref[...]).astype(o_ref.dtype)


def _conv_module(cfg, p, w_dw, b_dw, ln_g, ln_b, w_pw):
    tt, cv, halo = cfg.conv_tt, cfg.cv_width, CONV_HALO
    n_tiles = cfg.n_tok // tt
    cb = cfg.col_ca // cv
    hb = tt // halo
    last_halo = cfg.n_tok // halo - 1
    main = lambda c: pl.BlockSpec((tt, cv), lambda i: (i, c))
    prev = lambda c: pl.BlockSpec((halo, cv), lambda i: (jnp.maximum(i * hb - 1, 0), c))
    nxt = lambda c: pl.BlockSpec((halo, cv), lambda i: (jnp.minimum((i + 1) * hb, last_halo), c))
    vec = lambda: pl.BlockSpec((1, cv), lambda i: (0, 0))
    kern = functools.partial(_conv_kernel, n_ctx_tiles=cfg.n_ctx // tt, ctx_tiles_per_seq=cfg.seq // tt,
                             lat_tiles_per_seq=cfg.dec_seq // tt)
    return pl.pallas_call(
        kern,
        out_shape=jax.ShapeDtypeStruct((cfg.n_tok, cv), BF16),
        grid=(n_tiles,),
        in_specs=[main(cb), main(cb + 1), prev(cb), prev(cb + 1), nxt(cb), nxt(cb + 1),
                  pl.BlockSpec((CONV_TAPS, SUBLANES, cv), lambda i: (0, 0, 0)), vec(), vec(), vec(),
                  pl.BlockSpec((cv, cv), lambda i: (0, 0))],
        out_specs=pl.BlockSpec((tt, cv), lambda i: (i, 0)),
        scratch_shapes=[pltpu.VMEM((SUBLANES, tt + 2 * halo, cv), F32)],
        compiler_params=_cparams(("arbitrary",)),
        name="conv_module",
    )(p, p, p, p, p, p, jnp.broadcast_to(w_dw[:, None, :], (CONV_TAPS, SUBLANES, cv)),
      b_dw.reshape(1, cv), ln_g.reshape(1, cv), ln_b.reshape(1, cv), w_pw)


def _softmax_pv(scores, values):
    mx = functools.reduce(jnp.maximum, [jnp.max(s, axis=1, keepdims=True) for s in scores])
    ps = [jnp.exp(s - mx) for s in scores]
    den = functools.reduce(jnp.add, [jnp.sum(p, axis=1, keepdims=True) for p in ps])
    num = functools.reduce(jnp.add, [_dot(p.astype(BF16), v) for p, v in zip(ps, values)])
    return num / den


def _head_pair(q2, scores_fn, values):
    first_half = lax.broadcasted_iota(jnp.int32, q2.shape, 1) < NA_HEAD_DIM
    zero = jnp.zeros_like(q2)
    out_even = _softmax_pv(scores_fn(jnp.where(first_half, q2, zero), 0), values)
    out_odd = _softmax_pv(scores_fn(jnp.where(first_half, zero, q2), 1), values)
    return jnp.where(first_half, out_even, out_odd)


def _ctx_attn_kernel(q_ref, k_ref, v_ref, o_ref, *, heads):
    dh = NA_HEAD_DIM
    q = (q_ref[...] * (dh ** -0.5)).astype(BF16)
    k = k_ref[...].astype(BF16)
    v = v_ref[...].astype(BF16)
    outs = []
    for hp in range(heads // 2):
        sl = slice(hp * LANES, (hp + 1) * LANES)
        outs.append(_head_pair(q[:, sl], lambda qh, half: [_dot_nt(qh, k[:, sl])], [v[:, sl]]))
    o_ref[...] = jnp.concatenate(outs, axis=1).astype(o_ref.dtype)


def _ctx_attention(cfg, p):
    s, nw = cfg.seq, cfg.na_width
    qb, kb = cfg.col_qn // nw, cfg.col_kv // nw
    spec = lambda c: pl.BlockSpec((s, nw), lambda b: (b, c))
    return pl.pallas_call(
        functools.partial(_ctx_attn_kernel, heads=cfg.na_heads),
        out_shape=jax.ShapeDtypeStruct((cfg.n_tok, nw), BF16),
        grid=(cfg.batch,),
        in_specs=[spec(qb), spec(kb), spec(kb + 1)],
        out_specs=pl.BlockSpec((s, nw), lambda b: (b, 0)),
        compiler_params=_cparams(("arbitrary",)),
        name="context_attention",
    )(p, p, p)


NA_QROWS = 8


def _na_kernel(q_ref, kp_ref, kc_ref, kn_ref, vp_ref, vc_ref, vn_ref, kx_ref, vx_ref, bt_ref, into_ref, o_ref,
               kw_scr, vw_scr, *, heads, grid_rows):
    del into_ref
    dh = NA_HEAD_DIM
    blk = NA_QROWS * GRID_W
    win = WIN_ROWS * GRID_W
    m = pl.program_id(1)
    for slot, (kr, vr) in enumerate(((kp_ref, vp_ref), (kc_ref, vc_ref), (kn_ref, vn_ref))):
        kw_scr[slot * blk:(slot + 1) * blk, :] = kr[...].astype(BF16)
        vw_scr[slot * blk:(slot + 1) * blk, :] = vr[...].astype(BF16)
    kx = kx_ref[...].astype(BF16)
    vx = vx_ref[...].astype(BF16)

    def row_body(i, carry):
        r = m * NA_QROWS + i
        rs = jnp.clip(r - WIN_ROWS // 2, 0, grid_rows - WIN_ROWS)
        start = rs - r + (WIN_ROWS - 1)
        off = pl.multiple_of((rs - NA_QROWS * (m - 1)) * GRID_W, GRID_W)
        qrows = pl.ds(pl.multiple_of(i * GRID_W, GRID_W), GRID_W)
        q = (q_ref[qrows, :].astype(F32) * (dh ** -0.5 * LOG2_E)).astype(BF16)
        kwin = kw_scr[pl.ds(off, win), :]
        vwin = vw_scr[pl.ds(off, win), :]
        first_half = lax.broadcasted_iota(jnp.int32, (GRID_W, LANES), 1) < dh
        pairs = [slice(hp * LANES, (hp + 1) * LANES) for hp in range(heads // 2)]
        scores = []
        for hp, sl in enumerate(pairs):
            q2 = q[:, sl]
            zero = jnp.zeros_like(q2)
            qm = jnp.concatenate([jnp.where(first_half, q2, zero), jnp.where(first_half, zero, q2)], axis=0)
            scores.append((_dot_nt(kwin[:, sl], qm) + bt_ref[start, hp],
                           _dot_nt(kx[:, sl], qm)))
        probs = []
        for s_loc, s_ctx in scores:
            mx = jnp.maximum(jnp.max(s_loc, axis=0, keepdims=True), jnp.max(s_ctx, axis=0, keepdims=True))
            p_loc = jnp.exp2(s_loc - mx)
            p_ctx = jnp.exp2(s_ctx - mx)
            inv = 1.0 / (jnp.sum(p_loc, axis=0, keepdims=True) + jnp.sum(p_ctx, axis=0, keepdims=True))
            probs.append((p_loc.astype(BF16), p_ctx.astype(BF16), _col_bcast(inv, LANES)))
        outs = []
        for (p_loc, p_ctx, inv_col), sl in zip(probs, pairs):
            pv = (_dot_tn(p_loc, vwin[:, sl]) + _dot_tn(p_ctx, vx[:, sl])) * inv_col
            outs.append(jnp.where(first_half, pv[0:GRID_W], pv[GRID_W:2 * GRID_W]))
        o_ref[qrows, :] = jnp.concatenate(outs, axis=1).astype(o_ref.dtype)
        return carry

    lax.fori_loop(0, NA_QROWS, row_body, 0)


def _na_bias_table(rpb):
    heads = rpb.shape[0]
    qc = np.arange(GRID_W)[:, None]
    kc = np.arange(GRID_W)[None, :]
    cs = np.clip(qc - WIN_COLS // 2, 0, GRID_W - WIN_COLS)
    valid = (kc >= cs) & (kc < cs + WIN_COLS)
    pad = GRID_W - WIN_COLS
    flipped = jnp.pad(jnp.flip(rpb * LOG2_E, axis=-1), ((0, 0), (0, 0), (pad, pad)))
    tab = jnp.stack([flipped[..., GRID_W - 1 - k:2 * GRID_W - 1 - k] for k in range(GRID_W)], axis=-2)
    tab = jnp.where(jnp.asarray(valid.T)[None, None], tab, NEG_INF)
    tab = jnp.stack([tab[:, s:s + WIN_ROWS] for s in range(WIN_ROWS)], axis=0)
    tab = tab.reshape(WIN_ROWS, heads // 2, 2, WIN_ROWS, GRID_W, GRID_W)
    tab = tab.transpose(0, 1, 3, 4, 2, 5)
    return tab.reshape(WIN_ROWS, heads // 2, WIN_ROWS * GRID_W, 2 * GRID_W)


def _na_latent(cfg, p, k_ctx, v_ctx, bias_tab, into):
    nw, heads = cfg.na_width, cfg.na_heads
    blk = NA_QROWS * GRID_W
    grid_rows = cfg.dec_seq // GRID_W
    nblk = grid_rows // NA_QROWS
    qb, kb = cfg.col_qn // nw, cfg.col_kv // nw
    rb0 = cfg.n_ctx // blk
    assert rb0 * blk == cfg.n_ctx and nblk * NA_QROWS == grid_rows

    def rows(shift, c):
        return pl.BlockSpec((blk, nw), lambda b, m: (rb0 + b * nblk + jnp.clip(m + shift, 0, nblk - 1), c))

    ctx_spec = pl.BlockSpec((None, cfg.past_len, nw), lambda b, m: (b, 0, 0))
    return pl.pallas_call(
        functools.partial(_na_kernel, heads=heads, grid_rows=grid_rows),
        out_shape=jax.ShapeDtypeStruct((cfg.n_tok, nw), BF16),
        grid=(cfg.dec_batch, nblk),
        in_specs=[rows(0, qb), rows(-1, kb), rows(0, kb), rows(1, kb),
                  rows(-1, kb + 1), rows(0, kb + 1), rows(1, kb + 1), ctx_spec, ctx_spec,
                  pl.BlockSpec((WIN_ROWS, heads // 2, WIN_ROWS * GRID_W, 2 * GRID_W), lambda b, m: (0, 0, 0, 0)),
                  pl.BlockSpec(memory_space=pl.ANY)],
        out_specs=pl.BlockSpec((blk, nw), lambda b, m: (rb0 + b * nblk + m, 0)),
        scratch_shapes=[pltpu.VMEM((3 * blk, nw), BF16), pltpu.VMEM((3 * blk, nw), BF16)],
        input_output_aliases={10: 0},
        compiler_params=_cparams(("arbitrary", "arbitrary")),
        name="neighbourhood_attention",
    )(p, p, p, p, p, p, p, k_ctx, v_ctx, bias_tab, into)


OUT_ROWS = 256


def _outproj_kernel(ym_ref, yc_ref, yn_ref, x_ref, w_ref, ga_ref, gpost_ref, gpre_ref, sc_ref, sh_ref,
                    xo_ref, ho_ref, y_scr, *, m_width, cv_width):
    a, b = m_width, m_width + cv_width
    for blk in range(x_ref.shape[0] // OUT_ROWS):
        blk_rows = slice(blk * OUT_ROWS, (blk + 1) * OUT_ROWS)
        y_scr[blk_rows, :] = (_dot(ym_ref[blk_rows, :], w_ref[0:a, :]) + _dot(yc_ref[blk_rows, :], w_ref[a:b, :])
                              + _dot(yn_ref[blk_rows, :], w_ref[b:, :]))
        for r in range(OUT_ROWS // ROW_CHUNK):
            rows = slice(blk * OUT_ROWS + r * ROW_CHUNK, blk * OUT_ROWS + (r + 1) * ROW_CHUNK)
            x = x_ref[rows, :] + ga_ref[...] * (_rms(y_scr[rows, :]) * gpost_ref[...])
            xo_ref[rows, :] = x
            ho_ref[rows, :] = (_rms(x) * gpre_ref[...] * (1.0 + sc_ref[...]) + sh_ref[...]).astype(BF16)


def _outproj(cfg, ym, yc, yn, x, w_out, modl, g_post, g_pre):
    d, tm = cfg.d_model, cfg.tm_out
    rowblk = lambda w: pl.BlockSpec((tm, w), lambda i: (i, 0))
    vec = lambda: pl.BlockSpec((1, d), lambda i: (0, 0))
    return pl.pallas_call(
        functools.partial(_outproj_kernel, m_width=cfg.m_width, cv_width=cfg.cv_width),
        out_shape=(jax.ShapeDtypeStruct((cfg.n_tok, d), F32), jax.ShapeDtypeStruct((cfg.n_tok, d), BF16)),
        grid=(cfg.n_tok // tm,),
        in_specs=[rowblk(cfg.m_width), rowblk(cfg.cv_width), rowblk(cfg.na_width), rowblk(d),
                  pl.BlockSpec((d, d), lambda i: (0, 0)),
                  _mod_spec(cfg, 2, tm), vec(), vec(), _mod_spec(cfg, 4, tm), _mod_spec(cfg, 3, tm)],
        out_specs=(rowblk(d), rowblk(d)),
        scratch_shapes=[pltpu.VMEM((tm, d), F32)],
        compiler_params=_cparams(("arbitrary",)),
        name="out_projection",
    )(ym, yc, yn, x, w_out, modl, g_post.reshape(1, d), g_pre.reshape(1, d), modl, modl)


FFN_HIDDEN = 1024


def _ffn_kernel(h_ref, x_ref, w1_ref, w2_ref, ga_ref, g_ref, o_ref):
    kk = pl.program_id(1)
    acc_ref = o_ref

    @pl.when(kk == 0)
    def _():
        acc_ref[...] = jnp.zeros_like(acc_ref)

    last = pl.num_programs(1) - 1
    tm = x_ref.shape[0]

    def accumulate(rows):
        step = min(FFN_HIDDEN, w1_ref.shape[1])
        for hb in range(0, w1_ref.shape[1], step):
            hidden = slice(hb, hb + step)
            a = jnp.maximum(_dot(h_ref[rows, :], w1_ref[:, hidden]), 0.0)
            acc_ref[rows, :] += _dot((a * a).astype(BF16), w2_ref[hidden, :])

    @pl.when(kk < last)
    def _():
        accumulate(slice(0, tm))

    @pl.when(kk == last)
    def _():
        blk_rows = min(OUT_ROWS, tm)
        for blk in range(tm // blk_rows):
            accumulate(slice(blk * blk_rows, (blk + 1) * blk_rows))
            for r in range(blk * blk_rows, (blk + 1) * blk_rows, ROW_CHUNK):
                rows = slice(r, r + ROW_CHUNK)
                o_ref[rows, :] = x_ref[rows, :] + ga_ref[...] * (_rms(acc_ref[rows, :]) * g_ref[...])


def _ffn(cfg, h, x, w1, w2, modl, g_post, row0=0, n_rows=None):
    d, tm, th = cfg.d_model, cfg.tm_ff, cfg.th_ff
    n_rows = cfg.n_tok if n_rows is None else n_rows
    tile0 = row0 // tm
    assert tile0 * tm == row0 and n_rows % tm == 0
    return pl.pallas_call(
        _ffn_kernel,
        out_shape=jax.ShapeDtypeStruct((n_rows, d), F32),
        grid=(n_rows // tm, cfg.d_ff // th),
        in_specs=[pl.BlockSpec((tm, d), lambda i, k: (tile0 + i, 0)),
                  pl.BlockSpec((tm, d), lambda i, k: (tile0 + i, 0)),
                  pl.BlockSpec((d, th), lambda i, k: (0, k)),
                  pl.BlockSpec((th, d), lambda i, k: (k, 0)),
                  _mod_spec(cfg, 5, tm, tile0),
                  pl.BlockSpec((1, d), lambda i, k: (0, 0))],
        out_specs=pl.BlockSpec((tm, d), lambda i, k: (i, 0)),
        compiler_params=_cparams(("arbitrary", "arbitrary"), VMEM_LIMIT_MLP),
        name="relu2_mlp",
    )(h, x, w1, w2, modl, g_post.reshape(1, d))


def _prep_w_in(cfg, w_in_l):
    mw, cv, nw, ng = cfg.m_width, cfg.cv_width, cfg.na_width, cfg.n_gates
    g0 = 4 * mw
    c0 = g0 + ng
    n0 = c0 + 2 * cv
    gates = w_in_l[:, g0:c0]
    parts = [w_in_l[:, :g0], w_in_l[:, c0:n0], w_in_l[:, n0 + nw:n0 + 3 * nw], w_in_l[:, n0:n0 + nw], gates]
    pad = cfg.in_cols - (cfg.col_gates + ng)
    w_p = jnp.concatenate(parts + [jnp.zeros((cfg.d_model, pad), w_in_l.dtype)], axis=1)
    return w_p.astype(BF16), gates.T.astype(BF16)


def _forward(cfg, x_prompt, x_sample, c, cache_k, cache_v, state_C, state_n, state_m, c_ctx, w_ada, b_ada,
             g_pre_mix, g_post_mix, g_pre_mlp, g_post_mlp, w_in, b_gate, g_mlstm, w_dw, b_dw, ln_g, ln_b,
             w_pw, rpb, w_out, w_ff1, w_ff2):
    d = cfg.d_model
    x = jnp.concatenate([x_prompt.reshape(cfg.n_ctx, d), x_sample.reshape(cfg.n_lat, d)], axis=0)

    n_cond = 1 + cfg.dec_batch
    cond_rows = -(-n_cond // 8) * 8
    cond = jnp.concatenate([c_ctx[None, :], c, jnp.zeros((cond_rows - n_cond, d), F32)], axis=0)
    mod = _modulation(cond, w_ada, b_ada, tn=min(1024, 6 * d)).reshape(cfg.depth, cond_rows, 6, 1, d)

    rope_tabs = _rope_tables(cfg)
    nw = cfg.na_width
    ks, vs, cs, ns, ms = [], [], [], [], []
    for l in range(cfg.depth):
        w_in_p, w_gt = _prep_w_in(cfg, w_in[l])
        p, gt, kv_new = _inproj(cfg, x, mod[l], g_pre_mix[l], w_in_p, w_gt, rope_tabs)

        bias_rows = jnp.broadcast_to(b_gate[l].reshape(cfg.n_gates, 1, 1), (cfg.n_gates, 1, M_HEAD_DIM))
        ym_c, c_new, n_new, m_new = _mlstm(cfg, p, gt, bias_rows, g_mlstm[l], n_seq=cfg.batch, seq_len=cfg.seq,
                                           row0=0, n_sub=cfg.ctx_sub, unroll=1, name="mlstm_context",
                                           emit_state=True)
        (ym,) = _mlstm(cfg, p, gt, bias_rows, g_mlstm[l], n_seq=cfg.dec_batch, seq_len=cfg.dec_seq,
                       row0=cfg.n_ctx, n_sub=1, unroll=cfg.lat_unroll, name="mlstm_latent",
                       state0=(state_C[:, l], state_n[:, l], state_m[:, l]), into=ym_c)
        yc = _conv_module(cfg, p, w_dw[l], b_dw[l], ln_g[l], ln_b[l], w_pw[l].astype(BF16))
        yn = _na_latent(cfg, p, cache_k[:, l].reshape(cfg.dec_batch, cfg.past_len, nw),
                        cache_v[:, l].reshape(cfg.dec_batch, cfg.past_len, nw),
                        _na_bias_table(rpb[l]), into=_ctx_attention(cfg, p))

        x, h2 = _outproj(cfg, ym, yc, yn, x, w_out[l].astype(BF16), mod[l], g_post_mix[l], g_pre_mlp[l])
        ffn_args = (w_ff1[l].astype(BF16), w_ff2[l].astype(BF16), mod[l], g_post_mlp[l])
        if l + 1 < cfg.depth:
            x = _ffn(cfg, h2, x, *ffn_args)
        else:
            y_prompt = _ffn(cfg, h2, x, *ffn_args, row0=0, n_rows=cfg.n_ctx)
            y_sample = _ffn(cfg, h2, x, *ffn_args, row0=cfg.n_ctx, n_rows=cfg.n_lat)

        ks.append(kv_new[:, :nw].reshape(cfg.batch, cfg.seq, cfg.na_heads, NA_HEAD_DIM))
        vs.append(kv_new[:, nw:].reshape(cfg.batch, cfg.seq, cfg.na_heads, NA_HEAD_DIM))
        cs.append(c_new)
        ns.append(n_new[:, :, :, 0, :])
        ms.append(m_new[:, :, :, 0, 0])

    y_prompt = y_prompt.reshape(cfg.batch, cfg.seq, d)
    y_sample = y_sample.reshape(cfg.dec_batch, cfg.dec_seq, d)
    return (y_prompt, y_sample, jnp.stack(ks, axis=1), jnp.stack(vs, axis=1),
            jnp.stack(cs, axis=1), jnp.stack(ns, axis=1), jnp.stack(ms, axis=1))


def _make_cfg(x_prompt, x_sample, cache_k, w_in, w_ff1, w_pw, **tiles):
    batch, seq, d = x_prompt.shape
    dec_batch, dec_seq, _ = x_sample.shape
    return Cfg(d_model=d, batch=batch, seq=seq, depth=w_in.shape[0], dec_batch=dec_batch, dec_seq=dec_seq,
               past_len=cache_k.shape[2], m_heads=d // (2 * M_HEAD_DIM), cv_width=w_pw.shape[-1],
               na_heads=cache_k.shape[3], d_ff=w_ff1.shape[-1], **tiles)


def kernel(x_prompt, x_sample, c, cache_k, cache_v, state_C, state_n, state_m, c_ctx, w_ada, b_ada, g_pre_mix,
           g_post_mix, g_pre_mlp, g_post_mlp, w_in, b_gate, g_mlstm, w_dw, b_dw, ln_g, ln_b, w_pw, rpb, w_out,
           w_ff1, w_ff2):
    cfg = _make_cfg(x_prompt, x_sample, cache_k, w_in, w_ff1, w_pw,
                    tm_in=1024, tm_out=512, tm_ff=512, tn_in=1024, th_ff=2048, ctx_sub=4, lat_unroll=4,
                    conv_tt=256)
    return _forward(cfg, x_prompt, x_sample, c, cache_k, cache_v, state_C, state_n, state_m, c_ctx, w_ada, b_ada,
                    g_pre_mix, g_post_mix, g_pre_mlp, g_post_mlp, w_in, b_gate, g_mlstm, w_dw, b_dw, ln_g, ln_b,
                    w_pw, rpb, w_out, w_ff1, w_ff2)
```

```python
import functools
from typing import NamedTuple

import numpy as np
import jax
import jax.numpy as jnp
from jax import lax
from jax.experimental import pallas as pl
from jax.experimental.pallas import tpu as pltpu

F32 = jnp.float32
BF16 = jnp.bfloat16

EPS = 1e-6
NEG_INF = -1e30
ROPE_BASE = 10000.0

LANES = 128
SUBLANES = 8
M_HEAD_DIM = 128
NA_HEAD_DIM = 64
GRID_W = 64
WIN_ROWS = 8
WIN_COLS = 16
CONV_TAPS = 31
CONV_HALO = 16
VMEM_LIMIT = 52 * 1024 * 1024
VMEM_LIMIT_MLP = 60 * 1024 * 1024


class Cfg(NamedTuple):
    d_model: int
    batch: int
    seq: int
    depth: int
    dec_batch: int
    dec_seq: int
    past_len: int
    m_heads: int
    cv_width: int
    na_heads: int
    d_ff: int
    tm_in: int
    tm_out: int
    tm_ff: int
    tn_in: int
    th_ff: int
    ctx_sub: int
    lat_unroll: int
    conv_tt: int

    @property
    def m_width(self):
        return self.m_heads * M_HEAD_DIM

    @property
    def na_width(self):
        return self.na_heads * NA_HEAD_DIM

    @property
    def n_gates(self):
        return 4 * self.m_heads

    @property
    def n_ctx(self):
        return self.batch * self.seq

    @property
    def n_lat(self):
        return self.dec_batch * self.dec_seq

    @property
    def n_tok(self):
        return self.n_ctx + self.n_lat

    @property
    def col_ca(self):
        return 4 * self.m_width

    @property
    def col_kv(self):
        return self.col_ca + 2 * self.cv_width

    @property
    def col_qn(self):
        return self.col_kv + 2 * self.na_width

    @property
    def col_gates(self):
        return self.col_qn + self.na_width

    @property
    def in_cols(self):
        raw = self.col_gates + LANES
        return -(-raw // self.tn_in) * self.tn_in


def _cparams(sem, vmem_limit=VMEM_LIMIT):
    return pltpu.CompilerParams(dimension_semantics=sem, vmem_limit_bytes=vmem_limit)


def _rms(x):
    return x * lax.rsqrt(jnp.mean(x * x, axis=-1, keepdims=True) + EPS)


def _dot(a, b):
    return jnp.dot(a, b, preferred_element_type=F32)


def _dot_nt(a, b):
    return lax.dot_general(a, b, (((1,), (1,)), ((), ())), preferred_element_type=F32)


def _dot_tn(a, b):
    return lax.dot_general(a, b, (((0,), (0,)), ((), ())), preferred_element_type=F32)


def _mod_kernel(c_ref, w_ref, b_ref, o_ref):
    c = c_ref[...]
    s = (c * jax.nn.sigmoid(c)).astype(BF16)
    o_ref[...] = _dot(s, w_ref[...].astype(BF16)) + b_ref[...]


def _modulation(cond, w_ada, b_ada, tn):
    depth, d, n6 = w_ada.shape
    rows = cond.shape[0]
    return pl.pallas_call(
        _mod_kernel,
        out_shape=jax.ShapeDtypeStruct((depth, rows, n6), F32),
        grid=(depth, n6 // tn),
        in_specs=[
            pl.BlockSpec((rows, d), lambda l, j: (0, 0)),
            pl.BlockSpec((None, d, tn), lambda l, j: (l, 0, j)),
            pl.BlockSpec((None, 1, tn), lambda l, j: (l, 0, j)),
        ],
        out_specs=pl.BlockSpec((None, rows, tn), lambda l, j: (l, 0, j)),
        compiler_params=_cparams(("arbitrary", "arbitrary")),
        name="adaln_modulation",
    )(cond, w_ada, b_ada.reshape(depth, 1, n6))


def _mod_row_map(cfg, which, tm, tile0=0):
    assert cfg.n_ctx % tm == 0 and cfg.dec_seq % tm == 0
    nct = cfg.n_ctx // tm
    tpb = cfg.dec_seq // tm

    def index_map(i, *_):
        t = tile0 + i
        row = jnp.where(t < nct, 0, 1 + (t - nct) // tpb)
        return (row, which, 0, 0)

    return index_map


def _mod_spec(cfg, which, tm, tile0=0):
    return pl.BlockSpec((None, None, 1, cfg.d_model), _mod_row_map(cfg, which, tm, tile0))


ROW_CHUNK = 128


def _rope(x, cos, sin_signed, even_block):
    swapped = jnp.where(even_block, pltpu.roll(x, 96, 1), pltpu.roll(x, 32, 1))
    return x * cos + swapped * sin_signed


IN_ROWS = 256


def _inproj_kernel(x_ref, sh_ref, sc_ref, g_ref, w_ref, wgt_ref, cos_ref, sin_ref, o_ref, gt_ref, kv_ref,
                   h_scr, acc_scr, *, q_tiles, k_tiles, kv_tile, n_ctx_tiles):
    i = pl.program_id(0)
    j = pl.program_id(1)
    tm, tn = o_ref.shape

    def norm_rows(rows):
        h = _rms(x_ref[rows, :]) * g_ref[...] * (1.0 + sc_ref[...]) + sh_ref[...]
        h_scr[rows, :] = h.astype(BF16)

    def matmul_rope(rows):
        acc_scr[rows, :] = _dot(h_scr[rows, :], w_ref[...])
        scale = jnp.where(j >= q_tiles, M_HEAD_DIM ** -0.5, 1.0).astype(F32)
        lane_blk = lax.broadcasted_iota(jnp.int32, (ROW_CHUNK, M_HEAD_DIM), 1) // 32
        even_block = (lane_blk % 2) == 0
        for r in range(rows.start, rows.stop, ROW_CHUNK):
            chunk = slice(r, r + ROW_CHUNK)
            cos, sin = cos_ref[chunk, :], sin_ref[chunk, :]
            for hh in range(tn // M_HEAD_DIM):
                cols = slice(hh * M_HEAD_DIM, (hh + 1) * M_HEAD_DIM)
                o_ref[chunk, cols] = (_rope(acc_scr[chunk, cols], cos, sin, even_block) * scale).astype(o_ref.dtype)

    @pl.when(j == 0)
    def _():
        blk_rows = min(2 * IN_ROWS, tm)
        for blk in range(tm // blk_rows):
            for r in range(blk * blk_rows, (blk + 1) * blk_rows, ROW_CHUNK):
                norm_rows(slice(r, r + ROW_CHUNK))
            matmul_rope(slice(blk * blk_rows, (blk + 1) * blk_rows))
        gt_ref[...] = _dot_nt(wgt_ref[...], h_scr[...])

    @pl.when((j > 0) & (j < q_tiles + k_tiles))
    def _():
        blk_rows = min(IN_ROWS, tm)
        for blk in range(tm // blk_rows):
            matmul_rope(slice(blk * blk_rows, (blk + 1) * blk_rows))

    @pl.when((j >= q_tiles + k_tiles) & (j != kv_tile))
    def _():
        o_ref[...] = _dot(h_scr[...], w_ref[...]).astype(o_ref.dtype)

    @pl.when(j == kv_tile)
    def _():
        acc_scr[...] = _dot(h_scr[...], w_ref[...])
        o_ref[...] = acc_scr[...].astype(o_ref.dtype)

        @pl.when(i < n_ctx_tiles)
        def _():
            kv_ref[...] = acc_scr[...]


def _inproj(cfg, x, modl, g, w_in_p, w_gt, rope_tabs):
    d, tm, tn = cfg.d_model, cfg.tm_in, cfg.tn_in
    ng = cfg.n_gates
    nct = cfg.n_ctx // tm
    assert cfg.m_width % tn == 0 and cfg.col_kv % tn == 0 and 2 * cfg.na_width == tn and nct * tm == cfg.n_ctx
    kern = functools.partial(_inproj_kernel, q_tiles=cfg.m_width // tn, k_tiles=cfg.m_width // tn,
                             kv_tile=cfg.col_kv // tn, n_ctx_tiles=nct)
    return pl.pallas_call(
        kern,
        out_shape=(jax.ShapeDtypeStruct((cfg.n_tok, cfg.in_cols), BF16),
                   jax.ShapeDtypeStruct((ng, cfg.n_tok), F32),
                   jax.ShapeDtypeStruct((cfg.n_ctx, tn), F32)),
        grid=(cfg.n_tok // tm, cfg.in_cols // tn),
        in_specs=[
            pl.BlockSpec((tm, d), lambda i, j: (i, 0)),
            _mod_spec(cfg, 0, tm),
            _mod_spec(cfg, 1, tm),
            pl.BlockSpec((1, d), lambda i, j: (0, 0)),
            pl.BlockSpec((d, tn), lambda i, j: (0, j)),
            pl.BlockSpec((ng, d), lambda i, j: (0, 0)),
            pl.BlockSpec((tm, M_HEAD_DIM), lambda i, j: (i, 0)),
            pl.BlockSpec((tm, M_HEAD_DIM), lambda i, j: (i, 0)),
        ],
        out_specs=(pl.BlockSpec((tm, tn), lambda i, j: (i, j)),
                   pl.BlockSpec((ng, tm), lambda i, j: (0, i)),
                   pl.BlockSpec((tm, tn), lambda i, j: (jnp.minimum(i, nct - 1), 0))),
        scratch_shapes=[pltpu.VMEM((tm, d), BF16), pltpu.VMEM((tm, tn), F32)],
        compiler_params=_cparams(("arbitrary", "arbitrary")),
        name="in_projection",
    )(x, modl, modl, g.reshape(1, d), w_in_p, w_gt, *rope_tabs)


ROW_A, ROW_M, ROW_E, ROW_WK, ROW_MPREV, ROW_DECAY = range(6)
LOG2_E = 1.4426950408889634
FINISH_ROWS = 512


def _log_sigmoid(x):
    return jnp.minimum(x, 0.0) - jnp.log1p(jnp.exp(-jnp.abs(x)))


def _cumsum_lanes(x, cum_mat):
    hi = x.astype(BF16)
    r1 = x - hi.astype(F32)
    mid = r1.astype(BF16)
    lo = (r1 - mid.astype(F32)).astype(BF16)
    return _dot(hi, cum_mat) + _dot(mid, cum_mat) + _dot(lo, cum_mat)


def _cummax_lanes_both(xs):
    fwd, bwd = xs
    n = fwd.shape[1]
    lane = lax.broadcasted_iota(jnp.int32, fwd.shape, 1)
    k = 1
    while k < n:
        fwd = jnp.maximum(fwd, jnp.where(lane >= k, pltpu.roll(fwd, k, 1), -jnp.inf))
        bwd = jnp.maximum(bwd, jnp.where(lane < n - k, pltpu.roll(bwd, n - k, 1), -jnp.inf))
        k *= 2
    return [fwd, bwd]


def _col_bcast(row, n):
    return jnp.transpose(jnp.broadcast_to(row, (n, row.shape[1])))


def _mlstm_kernel(*refs, seq_len, n_sub, heads, unroll, init_state, emit_state, aliased):
    it = iter(refs)
    q_ref, k_ref, v_ref, om_ref, gt_ref, bias_ref, gm_ref = (next(it) for _ in range(7))
    c0_ref = n0_ref = m0_ref = None
    if init_state:
        c0_ref, n0_ref, m0_ref = next(it), next(it), next(it)
    if aliased:
        next(it)
    o_ref = next(it)
    co_ref = no_ref = mo_ref = None
    if emit_state:
        co_ref, no_ref, mo_ref = next(it), next(it), next(it)
    hs_f, hs_b, cext, rows_scr = next(it), next(it), next(it), next(it)

    L = dk = M_HEAD_DIM
    head = pl.program_id(1)
    nc = seq_len // L

    tt = lax.broadcasted_iota(jnp.int32, (L, L), 0)
    ss = lax.broadcasted_iota(jnp.int32, (L, L), 1)
    cum_mats = ((tt <= ss).astype(BF16), (tt >= ss).astype(BF16))
    causal = (ss <= tt, ss >= tt)

    dirs = (0, 1)
    row_i = [d * 2 * heads + head for d in dirs]
    xi = [gt_ref[r] + bias_ref[r] for r in row_i]
    lf = [_log_sigmoid(gt_ref[r + heads] + bias_ref[r + heads]) for r in row_i]
    b = [_cumsum_lanes(lf[d], cum_mats[d]) for d in dirs]
    tot = [jnp.sum(lf[d], axis=1, keepdims=True) for d in dirs]
    a = [xi[d] - b[d] for d in dirs]
    g = [tot[d] - b[d] + xi[d] for d in dirs]
    gmax = [jnp.max(g[d], axis=1, keepdims=True) for d in dirs]
    arun = _cummax_lanes_both(a)
    a_l2 = [a[d] * LOG2_E for d in dirs]
    for s in range(n_sub):
        m = [m0_ref[s, d][:, 0:1] if init_state else jnp.zeros((1, 1), F32) for d in dirs]
        for step in range(nc):
            for d in dirs:
                r = s * nc + (step if d == 0 else nc - 1 - step)
                m_new = jnp.maximum(tot[d][r:r + 1] + m[d], gmax[d][r:r + 1])
                m_row = jnp.maximum(arun[d][r:r + 1], m[d])
                rows_scr[d, r, ROW_A:ROW_A + 1, :] = a_l2[d][r:r + 1]
                rows_scr[d, r, ROW_M:ROW_M + 1, :] = m_row * LOG2_E
                rows_scr[d, r, ROW_E:ROW_E + 1, :] = -(b[d][r:r + 1] + m_row) * LOG2_E
                rows_scr[d, r, ROW_WK:ROW_WK + 1, :] = jnp.exp(g[d][r:r + 1] - m_new)
                rows_scr[d, r, ROW_MPREV:ROW_MPREV + 1, :] = jnp.broadcast_to(m[d] * LOG2_E, (1, L))
                rows_scr[d, r, ROW_DECAY:ROW_DECAY + 1, :] = jnp.broadcast_to(
                    jnp.exp(tot[d][r:r + 1] + m[d] - m_new), (1, L))
                m[d] = m_new
        for d in dirs:
            if emit_state:
                mo_ref[s, d] = jnp.broadcast_to(m[d], (1, LANES))
            if init_state:
                cext[s, d, :, 0:dk] = c0_ref[s, d]
                cext[s, d, :, dk:2 * dk] = _col_bcast(n0_ref[s, d], dk)
            else:
                cext[s, d] = jnp.zeros((dk, 2 * dk), F32)

    ones_ext = jnp.ones((L, dk), BF16)

    def prepare(s, d, c):
        rows = pl.ds(pl.multiple_of(s * seq_len + c * L, L), L)
        tile = rows_scr[d, s * nc + c]
        q = q_ref[rows, :].astype(BF16)
        k = k_ref[rows, :]
        v_ext = jnp.concatenate([v_ref[rows, :].astype(BF16), ones_ext], axis=1)
        m_b = _col_bcast(tile[ROW_M:ROW_M + 1], dk)
        floor = jnp.exp2(_col_bcast(tile[ROW_E:ROW_E + 1], dk))
        dmat = jnp.exp2(jnp.where(causal[d], tile[ROW_A:ROW_A + 1] - m_b, -jnp.inf))
        sc = (_dot_nt(q, k.astype(BF16)) * dmat).astype(BF16)
        w_int = jnp.exp2(tile[ROW_MPREV:ROW_MPREV + 1] - m_b)
        lhs = jnp.concatenate([sc, (w_int * q.astype(F32)).astype(BF16)], axis=1)
        kt_w = (jnp.transpose(k.astype(F32)) * tile[ROW_WK:ROW_WK + 1]).astype(BF16)
        return rows, lhs, v_ext, floor, kt_w, tile[ROW_DECAY:ROW_DECAY + 1, 0:1]

    def apply(s, d, prepared):
        rows, lhs, v_ext, floor, kt_w, decay = prepared
        c_prev = cext[s, d]
        nd = _dot(lhs, jnp.concatenate([v_ext, c_prev.astype(BF16)], axis=0))
        den = jnp.maximum(jnp.abs(nd[:, dk:2 * dk]), floor)
        (hs_f if d == 0 else hs_b)[rows, :] = nd[:, 0:dk] / den
        cext[s, d] = decay * c_prev + _dot(kt_w, v_ext)

    def chunk_body(i, carry):
        todo = []
        for u in range(unroll):
            c = i * unroll + u
            for s in range(n_sub):
                todo.append((s, 0, prepare(s, 0, c)))
                todo.append((s, 1, prepare(s, 1, nc - 1 - c)))
        for s, d, prepared in todo:
            apply(s, d, prepared)
        return carry

    lax.fori_loop(0, nc // unroll, chunk_body, 0)

    def finish(r, carry):
        rows = pl.ds(pl.multiple_of(r * FINISH_ROWS, FINISH_ROWS), FINISH_ROWS)
        y = _rms(hs_f[rows, :] + hs_b[rows, :]) * gm_ref[...] * jax.nn.sigmoid(om_ref[rows, :].astype(F32))
        o_ref[rows, :] = y.astype(o_ref.dtype)
        return carry

    lax.fori_loop(0, n_sub * seq_len // FINISH_ROWS, finish, 0)

    if emit_state:
        for s in range(n_sub):
            for d in range(2):
                co_ref[s, d] = cext[s, d, :, 0:dk]
                no_ref[s, d] = jnp.transpose(cext[s, d, :, dk:2 * dk])[0:1, :]


def _mlstm(cfg, p, gt, bias_rows, g_mlstm, *, n_seq, seq_len, row0, n_sub, unroll, name, state0=None,
           emit_state=False, into=None):
    heads, dk = cfg.m_heads, M_HEAD_DIM
    L = dk
    nc = seq_len // L
    blk_rows = n_sub * seq_len
    sb = row0 // blk_rows
    assert sb * blk_rows == row0 and nc * L == seq_len and n_seq % n_sub == 0 and nc % unroll == 0
    assert (n_sub * nc) % 8 == 0
    n_steps = n_seq // n_sub

    def col(block0):
        return pl.BlockSpec((blk_rows, dk), lambda b, h: (sb + b, block0 + h))

    gt4 = gt[:, row0:row0 + n_seq * seq_len].reshape(cfg.n_gates, n_steps, n_sub * nc, L)
    in_specs = [col(0), col(heads), col(2 * heads), col(3 * heads),
                pl.BlockSpec((cfg.n_gates, None, n_sub * nc, L), lambda b, h: (0, b, 0, 0)),
                pl.BlockSpec((cfg.n_gates, 1, L), lambda b, h: (0, 0, 0)),
                pl.BlockSpec((None, 1, dk), lambda b, h: (h, 0, 0))]
    args = [p, p, p, p, gt4, bias_rows, g_mlstm.reshape(heads, 1, dk)]
    state_spec = lambda r, c: pl.BlockSpec((n_sub, 2, None, r, c), lambda b, h: (b, 0, h, 0, 0))
    if state0 is not None:
        c0, n0, m0 = state0
        in_specs += [state_spec(dk, dk), state_spec(1, dk), state_spec(1, LANES)]
        args += [c0, n0.reshape(n_seq, 2, heads, 1, dk),
                 jnp.broadcast_to(m0[..., None, None], (n_seq, 2, heads, 1, LANES))]
    aliases = {}
    if into is not None:
        aliases = {len(args): 0}
        in_specs.append(pl.BlockSpec(memory_space=pl.ANY))
        args.append(into)
    out_shape = [jax.ShapeDtypeStruct((cfg.n_tok, cfg.m_width), BF16)]
    out_specs = [pl.BlockSpec((blk_rows, dk), lambda b, h: (sb + b, h))]
    if emit_state:
        out_shape += [jax.ShapeDtypeStruct((n_seq, 2, heads, dk, dk), F32),
                      jax.ShapeDtypeStruct((n_seq, 2, heads, 1, dk), F32),
                      jax.ShapeDtypeStruct((n_seq, 2, heads, 1, LANES), F32)]
        out_specs += [state_spec(dk, dk), state_spec(1, dk), state_spec(1, LANES)]
    kern = functools.partial(_mlstm_kernel, seq_len=seq_len, n_sub=n_sub, heads=heads, unroll=unroll,
                             init_state=state0 is not None, emit_state=emit_state, aliased=into is not None)
    return pl.pallas_call(
        kern,
        out_shape=tuple(out_shape),
        grid=(n_steps, heads),
        in_specs=in_specs,
        out_specs=tuple(out_specs),
        input_output_aliases=aliases,
        scratch_shapes=[pltpu.VMEM((blk_rows, dk), F32), pltpu.VMEM((blk_rows, dk), F32),
                        pltpu.VMEM((n_sub, 2, dk, 2 * dk), F32), pltpu.VMEM((2, n_sub * nc, 8, L), F32)],
        compiler_params=_cparams(("arbitrary", "arbitrary")),
        name=name,
    )(*args)


def _rope_tables(cfg):
    half = M_HEAD_DIM // 2
    nf = half // 2
    t = jnp.arange(cfg.dec_seq)
    inv = ROPE_BASE ** (-jnp.arange(nf, dtype=F32) / nf)
    ang_r = (t // GRID_W).astype(F32)[:, None] * inv[None, :]
    ang_c = (t % GRID_W).astype(F32)[:, None] * inv[None, :]
    cos = jnp.concatenate([jnp.cos(ang_r)] * 2 + [jnp.cos(ang_c)] * 2, axis=-1)
    sin = jnp.concatenate([-jnp.sin(ang_r), jnp.sin(ang_r), -jnp.sin(ang_c), jnp.sin(ang_c)], axis=-1)
    cos = jnp.concatenate([jnp.ones((cfg.n_ctx, M_HEAD_DIM), F32), jnp.tile(cos, (cfg.dec_batch, 1))], axis=0)
    sin = jnp.concatenate([jnp.zeros((cfg.n_ctx, M_HEAD_DIM), F32), jnp.tile(sin, (cfg.dec_batch, 1))], axis=0)
    return cos, sin


CONV_ROWS = 32


def _conv_kernel(a_ref, g_ref, ap_ref, gp_ref, an_ref, gn_ref, wdw_ref, bdw_ref, lng_ref, lnb_ref, wpw_ref,
                 o_ref, u_scr, *, n_ctx_tiles, ctx_tiles_per_seq, lat_tiles_per_seq):
    i = pl.program_id(0)
    tt = a_ref.shape[0]
    halo = CONV_HALO
    pos = jnp.where(i < n_ctx_tiles, i % ctx_tiles_per_seq, (i - n_ctx_tiles) % lat_tiles_per_seq)
    per_seq = jnp.where(i < n_ctx_tiles, ctx_tiles_per_seq, lat_tiles_per_seq)
    keep_prev = (pos > 0).astype(F32)
    keep_next = (pos < per_seq - 1).astype(F32)

    glu = lambda a, g: a[...].astype(F32) * jax.nn.sigmoid(g[...].astype(F32))
    u_scr[0, 0:halo, :] = glu(ap_ref, gp_ref) * keep_prev
    u_scr[0, halo:halo + tt, :] = glu(a_ref, g_ref)
    u_scr[0, halo + tt:2 * halo + tt, :] = glu(an_ref, gn_ref) * keep_next
    n_rows = tt + 2 * halo
    window = u_scr[0]
    for r in range(1, SUBLANES):
        u_scr[r] = pltpu.roll(window, n_rows - r, 0)

    for blk in range(tt // CONV_ROWS):
        base = blk * CONV_ROWS
        acc = jnp.broadcast_to(bdw_ref[...], (CONV_ROWS, a_ref.shape[1]))
        for tap in range(CONV_TAPS):
            start = base + halo - CONV_TAPS // 2 + tap
            r, aligned = start % SUBLANES, start - start % SUBLANES
            w_tap = jnp.concatenate([wdw_ref[tap]] * (CONV_ROWS // SUBLANES), axis=0)
            acc = acc + u_scr[r, aligned:aligned + CONV_ROWS, :] * w_tap
        mu = jnp.mean(acc, axis=-1, keepdims=True)
        cen = acc - mu
        var = jnp.mean(cen * cen, axis=-1, keepdims=True)
        y = cen * lax.rsqrt(var + EPS) * lng_ref[...] + lnb_ref[...]
        y = y * jax.nn.sigmoid(y)
        o_ref[base:base + CONV_ROWS, :] = _dot(y.astype(BF16), wpw_ref[...]).astype(o_ref.dtype)


def _conv_module(cfg, p, w_dw, b_dw, ln_g, ln_b, w_pw):
    tt, cv, halo = cfg.conv_tt, cfg.cv_width, CONV_HALO
    n_tiles = cfg.n_tok // tt
    cb = cfg.col_ca // cv
    hb = tt // halo
    last_halo = cfg.n_tok // halo - 1
    main = lambda c: pl.BlockSpec((tt, cv), lambda i: (i, c))
    prev = lambda c: pl.BlockSpec((halo, cv), lambda i: (jnp.maximum(i * hb - 1, 0), c))
    nxt = lambda c: pl.BlockSpec((halo, cv), lambda i: (jnp.minimum((i + 1) * hb, last_halo), c))
    vec = lambda: pl.BlockSpec((1, cv), lambda i: (0, 0))
    kern = functools.partial(_conv_kernel, n_ctx_tiles=cfg.n_ctx // tt, ctx_tiles_per_seq=cfg.seq // tt,
                             lat_tiles_per_seq=cfg.dec_seq // tt)
    return pl.pallas_call(
        kern,
        out_shape=jax.ShapeDtypeStruct((cfg.n_tok, cv), BF16),
        grid=(n_tiles,),
        in_specs=[main(cb), main(cb + 1), prev(cb), prev(cb + 1), nxt(cb), nxt(cb + 1),
                  pl.BlockSpec((CONV_TAPS, SUBLANES, cv), lambda i: (0, 0, 0)), vec(), vec(), vec(),
                  pl.BlockSpec((cv, cv), lambda i: (0, 0))],
        out_specs=pl.BlockSpec((tt, cv), lambda i: (i, 0)),
        scratch_shapes=[pltpu.VMEM((SUBLANES, tt + 2 * halo, cv), F32)],
        compiler_params=_cparams(("arbitrary",)),
        name="conv_module",
    )(p, p, p, p, p, p, jnp.broadcast_to(w_dw[:, None, :], (CONV_TAPS, SUBLANES, cv)),
      b_dw.reshape(1, cv), ln_g.reshape(1, cv), ln_b.reshape(1, cv), w_pw)


def _softmax_pv(scores, values):
    mx = functools.reduce(jnp.maximum, [jnp.max(s, axis=1, keepdims=True) for s in scores])
    ps = [jnp.exp(s - mx) for s in scores]
    den = functools.reduce(jnp.add, [jnp.sum(p, axis=1, keepdims=True) for p in ps])
    num = functools.reduce(jnp.add, [_dot(p.astype(BF16), v) for p, v in zip(ps, values)])
    return num / den


def _head_pair(q2, scores_fn, values):
    first_half = lax.broadcasted_iota(jnp.int32, q2.shape, 1) < NA_HEAD_DIM
    zero = jnp.zeros_like(q2)
    out_even = _softmax_pv(scores_fn(jnp.where(first_half, q2, zero), 0), values)
    out_odd = _softmax_pv(scores_fn(jnp.where(first_half, zero, q2), 1), values)
    return jnp.where(first_half, out_even, out_odd)


def _ctx_attn_kernel(q_ref, k_ref, v_ref, o_ref, *, heads):
    dh = NA_HEAD_DIM
    q = (q_ref[...] * (dh ** -0.5)).astype(BF16)
    k = k_ref[...].astype(BF16)
    v = v_ref[...].astype(BF16)
    outs = []
    for hp in range(heads // 2):
        sl = slice(hp * LANES, (hp + 1) * LANES)
        outs.append(_head_pair(q[:, sl], lambda qh, half: [_dot_nt(qh, k[:, sl])], [v[:, sl]]))
    o_ref[...] = jnp.concatenate(outs, axis=1).astype(o_ref.dtype)


def _ctx_attention(cfg, p):
    s, nw = cfg.seq, cfg.na_width
    qb, kb = cfg.col_qn // nw, cfg.col_kv // nw
    spec = lambda c: pl.BlockSpec((s, nw), lambda b: (b, c))
    return pl.pallas_call(
        functools.partial(_ctx_attn_kernel, heads=cfg.na_heads),
        out_shape=jax.ShapeDtypeStruct((cfg.n_tok, nw), BF16),
        grid=(cfg.batch,),
        in_specs=[spec(qb), spec(kb), spec(kb + 1)],
        out_specs=pl.BlockSpec((s, nw), lambda b: (b, 0)),
        compiler_params=_cparams(("arbitrary",)),
        name="context_attention",
    )(p, p, p)


NA_QROWS = 8


def _na_kernel(q_ref, kp_ref, kc_ref, kn_ref, vp_ref, vc_ref, vn_ref, kx_ref, vx_ref, bt_ref, into_ref, o_ref,
               kw_scr, vw_scr, *, heads, grid_rows):
    del into_ref
    dh = NA_HEAD_DIM
    blk = NA_QROWS * GRID_W
    win = WIN_ROWS * GRID_W
    m = pl.program_id(1)
    for slot, (kr, vr) in enumerate(((kp_ref, vp_ref), (kc_ref, vc_ref), (kn_ref, vn_ref))):
        kw_scr[slot * blk:(slot + 1) * blk, :] = kr[...].astype(BF16)
        vw_scr[slot * blk:(slot + 1) * blk, :] = vr[...].astype(BF16)
    kx = kx_ref[...].astype(BF16)
    vx = vx_ref[...].astype(BF16)

    def row_body(i, carry):
        r = m * NA_QROWS + i
        rs = jnp.clip(r - WIN_ROWS // 2, 0, grid_rows - WIN_ROWS)
        start = rs - r + (WIN_ROWS - 1)
        off = pl.multiple_of((rs - NA_QROWS * (m - 1)) * GRID_W, GRID_W)
        qrows = pl.ds(pl.multiple_of(i * GRID_W, GRID_W), GRID_W)
        q = (q_ref[qrows, :].astype(F32) * (dh ** -0.5 * LOG2_E)).astype(BF16)
        kwin = kw_scr[pl.ds(off, win), :]
        vwin = vw_scr[pl.ds(off, win), :]
        first_half = lax.broadcasted_iota(jnp.int32, (GRID_W, LANES), 1) < dh
        pairs = [slice(hp * LANES, (hp + 1) * LANES) for hp in range(heads // 2)]
        scores = []
        for hp, sl in enumerate(pairs):
            q2 = q[:, sl]
            zero = jnp.zeros_like(q2)
            qm = jnp.concatenate([jnp.where(first_half, q2, zero), jnp.where(first_half, zero, q2)], axis=0)
            scores.append((_dot_nt(kwin[:, sl], qm) + bt_ref[start, hp],
                           _dot_nt(kx[:, sl], qm)))
        probs = []
        for s_loc, s_ctx in scores:
            mx = jnp.maximum(jnp.max(s_loc, axis=0, keepdims=True), jnp.max(s_ctx, axis=0, keepdims=True))
            p_loc = jnp.exp2(s_loc - mx)
            p_ctx = jnp.exp2(s_ctx - mx)
            inv = 1.0 / (jnp.sum(p_loc, axis=0, keepdims=True) + jnp.sum(p_ctx, axis=0, keepdims=True))
            probs.append((p_loc.astype(BF16), p_ctx.astype(BF16), _col_bcast(inv, LANES)))
        outs = []
        for (p_loc, p_ctx, inv_col), sl in zip(probs, pairs):
            pv = (_dot_tn(p_loc, vwin[:, sl]) + _dot_tn(p_ctx, vx[:, sl])) * inv_col
            outs.append(jnp.where(first_half, pv[0:GRID_W], pv[GRID_W:2 * GRID_W]))
        o_ref[qrows, :] = jnp.concatenate(outs, axis=1).astype(o_ref.dtype)
        return carry

    lax.fori_loop(0, NA_QROWS, row_body, 0)


def _na_bias_table(rpb):
    heads = rpb.shape[0]
    qc = np.arange(GRID_W)[:, None]
    kc = np.arange(GRID_W)[None, :]
    cs = np.clip(qc - WIN_COLS // 2, 0, GRID_W - WIN_COLS)
    valid = (kc >= cs) & (kc < cs + WIN_COLS)
    pad = GRID_W - WIN_COLS
    flipped = jnp.pad(jnp.flip(rpb * LOG2_E, axis=-1), ((0, 0), (0, 0), (pad, pad)))
    tab = jnp.stack([flipped[..., GRID_W - 1 - k:2 * GRID_W - 1 - k] for k in range(GRID_W)], axis=-2)
    tab = jnp.where(jnp.asarray(valid.T)[None, None], tab, NEG_INF)
    tab = jnp.stack([tab[:, s:s + WIN_ROWS] for s in range(WIN_ROWS)], axis=0)
    tab = tab.reshape(WIN_ROWS, heads // 2, 2, WIN_ROWS, GRID_W, GRID_W)
    tab = tab.transpose(0, 1, 3, 4, 2, 5)
    return tab.reshape(WIN_ROWS, heads // 2, WIN_ROWS * GRID_W, 2 * GRID_W)


def _na_latent(cfg, p, k_ctx, v_ctx, bias_tab, into):
    nw, heads = cfg.na_width, cfg.na_heads
    blk = NA_QROWS * GRID_W
    grid_rows = cfg.dec_seq // GRID_W
    nblk = grid_rows // NA_QROWS
    qb, kb = cfg.col_qn // nw, cfg.col_kv // nw
    rb0 = cfg.n_ctx // blk
    assert rb0 * blk == cfg.n_ctx and nblk * NA_QROWS == grid_rows

    def rows(shift, c):
        return pl.BlockSpec((blk, nw), lambda b, m: (rb0 + b * nblk + jnp.clip(m + shift, 0, nblk - 1), c))

    ctx_spec = pl.BlockSpec((None, cfg.past_len, nw), lambda b, m: (b, 0, 0))
    return pl.pallas_call(
        functools.partial(_na_kernel, heads=heads, grid_rows=grid_rows),
        out_shape=jax.ShapeDtypeStruct((cfg.n_tok, nw), BF16),
        grid=(cfg.dec_batch, nblk),
        in_specs=[rows(0, qb), rows(-1, kb), rows(0, kb), rows(1, kb),
                  rows(-1, kb + 1), rows(0, kb + 1), rows(1, kb + 1), ctx_spec, ctx_spec,
                  pl.BlockSpec((WIN_ROWS, heads // 2, WIN_ROWS * GRID_W, 2 * GRID_W), lambda b, m: (0, 0, 0, 0)),
                  pl.BlockSpec(memory_space=pl.ANY)],
        out_specs=pl.BlockSpec((blk, nw), lambda b, m: (rb0 + b * nblk + m, 0)),
        scratch_shapes=[pltpu.VMEM((3 * blk, nw), BF16), pltpu.VMEM((3 * blk, nw), BF16)],
        input_output_aliases={10: 0},
        compiler_params=_cparams(("arbitrary", "arbitrary")),
        name="neighbourhood_attention",
    )(p, p, p, p, p, p, p, k_ctx, v_ctx, bias_tab, into)


OUT_ROWS = 256


def _outproj_kernel(ym_ref, yc_ref, yn_ref, x_ref, w_ref, ga_ref, gpost_ref, gpre_ref, sc_ref, sh_ref,
                    xo_ref, ho_ref, y_scr, *, m_width, cv_width):
    a, b = m_width, m_width + cv_width
    for blk in range(x_ref.shape[0] // OUT_ROWS):
        blk_rows = slice(blk * OUT_ROWS, (blk + 1) * OUT_ROWS)
        y_scr[blk_rows, :] = (_dot(ym_ref[blk_rows, :], w_ref[0:a, :]) + _dot(yc_ref[blk_rows, :], w_ref[a:b, :])
                              + _dot(yn_ref[blk_rows, :], w_ref[b:, :]))
        for r in range(OUT_ROWS // ROW_CHUNK):
            rows = slice(blk * OUT_ROWS + r * ROW_CHUNK, blk * OUT_ROWS + (r + 1) * ROW_CHUNK)
            x = x_ref[rows, :] + ga_ref[...] * (_rms(y_scr[rows, :]) * gpost_ref[...])
            xo_ref[rows, :] = x
            ho_ref[rows, :] = (_rms(x) * gpre_ref[...] * (1.0 + sc_ref[...]) + sh_ref[...]).astype(BF16)


def _outproj(cfg, ym, yc, yn, x, w_out, modl, g_post, g_pre):
    d, tm = cfg.d_model, cfg.tm_out
    rowblk = lambda w: pl.BlockSpec((tm, w), lambda i: (i, 0))
    vec = lambda: pl.BlockSpec((1, d), lambda i: (0, 0))
    return pl.pallas_call(
        functools.partial(_outproj_kernel, m_width=cfg.m_width, cv_width=cfg.cv_width),
        out_shape=(jax.ShapeDtypeStruct((cfg.n_tok, d), F32), jax.ShapeDtypeStruct((cfg.n_tok, d), BF16)),
        grid=(cfg.n_tok // tm,),
        in_specs=[rowblk(cfg.m_width), rowblk(cfg.cv_width), rowblk(cfg.na_width), rowblk(d),
                  pl.BlockSpec((d, d), lambda i: (0, 0)),
                  _mod_spec(cfg, 2, tm), vec(), vec(), _mod_spec(cfg, 4, tm), _mod_spec(cfg, 3, tm)],
        out_specs=(rowblk(d), rowblk(d)),
        scratch_shapes=[pltpu.VMEM((tm, d), F32)],
        compiler_params=_cparams(("arbitrary",)),
        name="out_projection",
    )(ym, yc, yn, x, w_out, modl, g_post.reshape(1, d), g_pre.reshape(1, d), modl, modl)


FFN_HIDDEN = 1024


def _ffn_kernel(h_ref, x_ref, w1_ref, w2_ref, ga_ref, g_ref, o_ref):
    kk = pl.program_id(1)
    acc_ref = o_ref

    @pl.when(kk == 0)
    def _():
        acc_ref[...] = jnp.zeros_like(acc_ref)

    last = pl.num_programs(1) - 1
    tm = x_ref.shape[0]

    def accumulate(rows):
        step = min(FFN_HIDDEN, w1_ref.shape[1])
        for hb in range(0, w1_ref.shape[1], step):
            hidden = slice(hb, hb + step)
            a = jnp.maximum(_dot(h_ref[rows, :], w1_ref[:, hidden]), 0.0)
            acc_ref[rows, :] += _dot((a * a).astype(BF16), w2_ref[hidden, :])

    @pl.when(kk < last)
    def _():
        accumulate(slice(0, tm))

    @pl.when(kk == last)
    def _():
        blk_rows = min(OUT_ROWS, tm)
        for blk in range(tm // blk_rows):
            accumulate(slice(blk * blk_rows, (blk + 1) * blk_rows))
            for r in range(blk * blk_rows, (blk + 1) * blk_rows, ROW_CHUNK):
                rows = slice(r, r + ROW_CHUNK)
                o_ref[rows, :] = x_ref[rows, :] + ga_ref[...] * (_rms(acc_ref[rows, :]) * g_ref[...])


def _ffn(cfg, h, x, w1, w2, modl, g_post, row0=0, n_rows=None):
    d, tm, th = cfg.d_model, cfg.tm_ff, cfg.th_ff
    n_rows = cfg.n_tok if n_rows is None else n_rows
    tile0 = row0 // tm
    assert tile0 * tm == row0 and n_rows % tm == 0
    return pl.pallas_call(
        _ffn_kernel,
        out_shape=jax.ShapeDtypeStruct((n_rows, d), F32),
        grid=(n_rows // tm, cfg.d_ff // th),
        in_specs=[pl.BlockSpec((tm, d), lambda i, k: (tile0 + i, 0)),
                  pl.BlockSpec((tm, d), lambda i, k: (tile0 + i, 0)),
                  pl.BlockSpec((d, th), lambda i, k: (0, k)),
                  pl.BlockSpec((th, d), lambda i, k: (k, 0)),
                  _mod_spec(cfg, 5, tm, tile0),
                  pl.BlockSpec((1, d), lambda i, k: (0, 0))],
        out_specs=pl.BlockSpec((tm, d), lambda i, k: (i, 0)),
        compiler_params=_cparams(("arbitrary", "arbitrary"), VMEM_LIMIT_MLP),
        name="relu2_mlp",
    )(h, x, w1, w2, modl, g_post.reshape(1, d))


def _prep_w_in(cfg, w_in_l):
    mw, cv, nw, ng = cfg.m_width, cfg.cv_width, cfg.na_width, cfg.n_gates
    g0 = 4 * mw
    c0 = g0 + ng
    n0 = c0 + 2 * cv
    w16 = w_in_l.astype(BF16)
    gates = w16[:, g0:c0]
    parts = [w16[:, :g0], w16[:, c0:n0], w16[:, n0 + nw:n0 + 3 * nw], w16[:, n0:n0 + nw], gates]
    pad = cfg.in_cols - (cfg.col_gates + ng)
    w_p = jnp.concatenate(parts + [jnp.zeros((cfg.d_model, pad), BF16)], axis=1)
    return w_p, gates.T


def _forward(cfg, x_prompt, x_sample, c, cache_k, cache_v, state_C, state_n, state_m, c_ctx, w_ada, b_ada,
             g_pre_mix, g_post_mix, g_pre_mlp, g_post_mlp, w_in, b_gate, g_mlstm, w_dw, b_dw, ln_g, ln_b,
             w_pw, rpb, w_out, w_ff1, w_ff2):
    d = cfg.d_model
    x = jnp.concatenate([x_prompt.reshape(cfg.n_ctx, d), x_sample.reshape(cfg.n_lat, d)], axis=0)

    n_cond = 1 + cfg.dec_batch
    cond_rows = -(-n_cond // 8) * 8
    cond = jnp.concatenate([c_ctx[None, :], c, jnp.zeros((cond_rows - n_cond, d), F32)], axis=0)
    mod = _modulation(cond, w_ada, b_ada, tn=min(1024, 6 * d)).reshape(cfg.depth, cond_rows, 6, 1, d)

    rope_tabs = _rope_tables(cfg)
    nw = cfg.na_width
    ks, vs, cs, ns, ms = [], [], [], [], []
    for l in range(cfg.depth):
        w_in_p, w_gt = _prep_w_in(cfg, w_in[l])
        p, gt, kv_new = _inproj(cfg, x, mod[l], g_pre_mix[l], w_in_p, w_gt, rope_tabs)

        bias_rows = jnp.broadcast_to(b_gate[l].reshape(cfg.n_gates, 1, 1), (cfg.n_gates, 1, M_HEAD_DIM))
        ym_c, c_new, n_new, m_new = _mlstm(cfg, p, gt, bias_rows, g_mlstm[l], n_seq=cfg.batch, seq_len=cfg.seq,
                                           row0=0, n_sub=cfg.ctx_sub, unroll=1, name="mlstm_context",
                                           emit_state=True)
        (ym,) = _mlstm(cfg, p, gt, bias_rows, g_mlstm[l], n_seq=cfg.dec_batch, seq_len=cfg.dec_seq,
                       row0=cfg.n_ctx, n_sub=1, unroll=cfg.lat_unroll, name="mlstm_latent",
                       state0=(state_C[:, l], state_n[:, l], state_m[:, l]), into=ym_c)
        yc = _conv_module(cfg, p, w_dw[l], b_dw[l], ln_g[l], ln_b[l], w_pw[l].astype(BF16))
        yn = _na_latent(cfg, p, cache_k[:, l].reshape(cfg.dec_batch, cfg.past_len, nw),
                        cache_v[:, l].reshape(cfg.dec_batch, cfg.past_len, nw),
                        _na_bias_table(rpb[l]), into=_ctx_attention(cfg, p))

        x, h2 = _outproj(cfg, ym, yc, yn, x, w_out[l].astype(BF16), mod[l], g_post_mix[l], g_pre_mlp[l])
        ffn_args = (w_ff1[l].astype(BF16), w_ff2[l].astype(BF16), mod[l], g_post_mlp[l])
        if l + 1 < cfg.depth:
            x = _ffn(cfg, h2, x, *ffn_args)
        else:
            y_prompt = _ffn(cfg, h2, x, *ffn_args, row0=0, n_rows=cfg.n_ctx)
            y_sample = _ffn(cfg, h2, x, *ffn_args, row0=cfg.n_ctx, n_rows=cfg.n_lat)

        ks.append(kv_new[:, :nw].reshape(cfg.batch, cfg.seq, cfg.na_heads, NA_HEAD_DIM))
        vs.append(kv_new[:, nw:].reshape(cfg.batch, cfg.seq, cfg.na_heads, NA_HEAD_DIM))
        cs.append(c_new)
        ns.append(n_new[:, :, :, 0, :])
        ms.append(m_new[:, :, :, 0, 0])

    y_prompt = y_prompt.reshape(cfg.batch, cfg.seq, d)
    y_sample = y_sample.reshape(cfg.dec_batch, cfg.dec_seq, d)
    return (y_prompt, y_sample, jnp.stack(ks, axis=1), jnp.stack(vs, axis=1),
            jnp.stack(cs, axis=1), jnp.stack(ns, axis=1), jnp.stack(ms, axis=1))


def _make_cfg(x_prompt, x_sample, cache_k, w_in, w_ff1, w_pw, **tiles):
    batch, seq, d = x_prompt.shape
    dec_batch, dec_seq, _ = x_sample.shape
    return Cfg(d_model=d, batch=batch, seq=seq, depth=w_in.shape[0], dec_batch=dec_batch, dec_seq=dec_seq,
               past_len=cache_k.shape[2], m_heads=d // (2 * M_HEAD_DIM), cv_width=w_pw.shape[-1],
               na_heads=cache_k.shape[3], d_ff=w_ff1.shape[-1], **tiles)


def kernel(x_prompt, x_sample, c, cache_k, cache_v, state_C, state_n, state_m, c_ctx, w_ada, b_ada, g_pre_mix,
           g_post_mix, g_pre_mlp, g_post_mlp, w_in, b_gate, g_mlstm, w_dw, b_dw, ln_g, ln_b, w_pw, rpb, w_out,
           w_ff1, w_ff2):
    cfg = _make_cfg(x_prompt, x_sample, cache_k, w_in, w_ff1, w_pw,
                    tm_in=1024, tm_out=512, tm_ff=512, tn_in=1024, th_ff=2048, ctx_sub=4, lat_unroll=4,
                    conv_tt=256)
    return _forward(cfg, x_prompt, x_sample, c, cache_k, cache_v, state_C, state_n, state_m, c_ctx, w_ada, b_ada,
                    g_pre_mix, g_post_mix, g_pre_mlp, g_post_mlp, w_in, b_gate, g_mlstm, w_dw, b_dw, ln_g, ln_b,
                    w_pw, rpb, w_out, w_ff1, w_ff2)
```

```python
import functools
from typing import NamedTuple

import numpy as np
import jax
import jax.numpy as jnp
from jax import lax
from jax.experimental import pallas as pl
from jax.experimental.pallas import tpu as pltpu

F32 = jnp.float32
BF16 = jnp.bfloat16

EPS = 1e-6
NEG_INF = -1e30
ROPE_BASE = 10000.0

LANES = 128
SUBLANES = 8
M_HEAD_DIM = 128
NA_HEAD_DIM = 64
GRID_W = 64
WIN_ROWS = 8
WIN_COLS = 16
CONV_TAPS = 31
CONV_HALO = 16
VMEM_LIMIT = 52 * 1024 * 1024
VMEM_LIMIT_MLP = 60 * 1024 * 1024


class Cfg(NamedTuple):
    d_model: int
    batch: int
    seq: int
    depth: int
    dec_batch: int
    dec_seq: int
    past_len: int
    m_heads: int
    cv_width: int
    na_heads: int
    d_ff: int
    tm_in: int
    tm_out: int
    tm_ff: int
    tn_in: int
    th_ff: int
    ctx_sub: int
    lat_unroll: int
    conv_tt: int

    @property
    def m_width(self):
        return self.m_heads * M_HEAD_DIM

    @property
    def na_width(self):
        return self.na_heads * NA_HEAD_DIM

    @property
    def n_gates(self):
        return 4 * self.m_heads

    @property
    def n_ctx(self):
        return self.batch * self.seq

    @property
    def n_lat(self):
        return self.dec_batch * self.dec_seq

    @property
    def n_tok(self):
        return self.n_ctx + self.n_lat

    @property
    def col_ca(self):
        return 4 * self.m_width

    @property
    def col_kv(self):
        return self.col_ca + 2 * self.cv_width

    @property
    def col_qn(self):
        return self.col_kv + 2 * self.na_width

    @property
    def col_gates(self):
        return self.col_qn + self.na_width

    @property
    def in_cols(self):
        raw = self.col_gates + LANES
        return -(-raw // self.tn_in) * self.tn_in


def _cparams(sem, vmem_limit=VMEM_LIMIT):
    return pltpu.CompilerParams(dimension_semantics=sem, vmem_limit_bytes=vmem_limit)


def _rms(x):
    return x * lax.rsqrt(jnp.mean(x * x, axis=-1, keepdims=True) + EPS)


def _dot(a, b):
    return jnp.dot(a, b, preferred_element_type=F32)


def _dot_nt(a, b):
    return lax.dot_general(a, b, (((1,), (1,)), ((), ())), preferred_element_type=F32)


def _dot_tn(a, b):
    return lax.dot_general(a, b, (((0,), (0,)), ((), ())), preferred_element_type=F32)


def _mod_kernel(c_ref, w_ref, b_ref, o_ref):
    c = c_ref[...]
    s = (c * jax.nn.sigmoid(c)).astype(BF16)
    o_ref[...] = _dot(s, w_ref[...].astype(BF16)) + b_ref[...]


def _modulation(cond, w_ada, b_ada, tn):
    depth, d, n6 = w_ada.shape
    rows = cond.shape[0]
    return pl.pallas_call(
        _mod_kernel,
        out_shape=jax.ShapeDtypeStruct((depth, rows, n6), F32),
        grid=(depth, n6 // tn),
        in_specs=[
            pl.BlockSpec((rows, d), lambda l, j: (0, 0)),
            pl.BlockSpec((None, d, tn), lambda l, j: (l, 0, j)),
            pl.BlockSpec((None, 1, tn), lambda l, j: (l, 0, j)),
        ],
        out_specs=pl.BlockSpec((None, rows, tn), lambda l, j: (l, 0, j)),
        compiler_params=_cparams(("arbitrary", "arbitrary")),
        name="adaln_modulation",
    )(cond, w_ada, b_ada.reshape(depth, 1, n6))


def _mod_row_map(cfg, which, tm, tile0=0):
    assert cfg.n_ctx % tm == 0 and cfg.dec_seq % tm == 0
    nct = cfg.n_ctx // tm
    tpb = cfg.dec_seq // tm

    def index_map(i, *_):
        t = tile0 + i
        row = jnp.where(t < nct, 0, 1 + (t - nct) // tpb)
        return (row, which, 0, 0)

    return index_map


def _mod_spec(cfg, which, tm, tile0=0):
    return pl.BlockSpec((None, None, 1, cfg.d_model), _mod_row_map(cfg, which, tm, tile0))


ROW_CHUNK = 128


def _rope(x, cos, sin_signed, even_block):
    swapped = jnp.where(even_block, pltpu.roll(x, 96, 1), pltpu.roll(x, 32, 1))
    return x * cos + swapped * sin_signed


IN_ROWS = 256


def _inproj_kernel(x_ref, sh_ref, sc_ref, g_ref, w_ref, wgt_ref, cos_ref, sin_ref, o_ref, gt_ref, kv_ref,
                   h_scr, acc_scr, *, q_tiles, k_tiles, kv_tile, n_ctx_tiles):
    i = pl.program_id(0)
    j = pl.program_id(1)
    tm, tn = o_ref.shape

    def norm_rows(rows):
        h = _rms(x_ref[rows, :]) * g_ref[...] * (1.0 + sc_ref[...]) + sh_ref[...]
        h_scr[rows, :] = h.astype(BF16)

    def matmul_rope(rows):
        acc_scr[rows, :] = _dot(h_scr[rows, :], w_ref[...])
        scale = jnp.where(j >= q_tiles, M_HEAD_DIM ** -0.5, 1.0).astype(F32)
        lane_blk = lax.broadcasted_iota(jnp.int32, (ROW_CHUNK, M_HEAD_DIM), 1) // 32
        even_block = (lane_blk % 2) == 0
        for r in range(rows.start, rows.stop, ROW_CHUNK):
            chunk = slice(r, r + ROW_CHUNK)
            cos, sin = cos_ref[chunk, :], sin_ref[chunk, :]
            for hh in range(tn // M_HEAD_DIM):
                cols = slice(hh * M_HEAD_DIM, (hh + 1) * M_HEAD_DIM)
                o_ref[chunk, cols] = (_rope(acc_scr[chunk, cols], cos, sin, even_block) * scale).astype(o_ref.dtype)

    @pl.when(j == 0)
    def _():
        blk_rows = min(2 * IN_ROWS, tm)
        for blk in range(tm // blk_rows):
            for r in range(blk * blk_rows, (blk + 1) * blk_rows, ROW_CHUNK):
                norm_rows(slice(r, r + ROW_CHUNK))
            matmul_rope(slice(blk * blk_rows, (blk + 1) * blk_rows))
        gt_ref[...] = _dot_nt(wgt_ref[...], h_scr[...])

    @pl.when((j > 0) & (j < q_tiles + k_tiles))
    def _():
        blk_rows = min(IN_ROWS, tm)
        for blk in range(tm // blk_rows):
            matmul_rope(slice(blk * blk_rows, (blk + 1) * blk_rows))

    @pl.when((j >= q_tiles + k_tiles) & (j != kv_tile))
    def _():
        o_ref[...] = _dot(h_scr[...], w_ref[...]).astype(o_ref.dtype)

    @pl.when(j == kv_tile)
    def _():
        acc_scr[...] = _dot(h_scr[...], w_ref[...])
        o_ref[...] = acc_scr[...].astype(o_ref.dtype)

        @pl.when(i < n_ctx_tiles)
        def _():
            kv_ref[...] = acc_scr[...]


def _inproj(cfg, x, modl, g, w_in_p, w_gt, rope_tabs):
    d, tm, tn = cfg.d_model, cfg.tm_in, cfg.tn_in
    ng = cfg.n_gates
    nct = cfg.n_ctx // tm
    assert cfg.m_width % tn == 0 and cfg.col_kv % tn == 0 and 2 * cfg.na_width == tn and nct * tm == cfg.n_ctx
    kern = functools.partial(_inproj_kernel, q_tiles=cfg.m_width // tn, k_tiles=cfg.m_width // tn,
                             kv_tile=cfg.col_kv // tn, n_ctx_tiles=nct)
    return pl.pallas_call(
        kern,
        out_shape=(jax.ShapeDtypeStruct((cfg.n_tok, cfg.in_cols), BF16),
                   jax.ShapeDtypeStruct((ng, cfg.n_tok), F32),
                   jax.ShapeDtypeStruct((cfg.n_ctx, tn), F32)),
        grid=(cfg.n_tok // tm, cfg.in_cols // tn),
        in_specs=[
            pl.BlockSpec((tm, d), lambda i, j: (i, 0)),
            _mod_spec(cfg, 0, tm),
            _mod_spec(cfg, 1, tm),
            pl.BlockSpec((1, d), lambda i, j: (0, 0)),
            pl.BlockSpec((d, tn), lambda i, j: (0, j)),
            pl.BlockSpec((ng, d), lambda i, j: (0, 0)),
            pl.BlockSpec((tm, M_HEAD_DIM), lambda i, j: (i, 0)),
            pl.BlockSpec((tm, M_HEAD_DIM), lambda i, j: (i, 0)),
        ],
        out_specs=(pl.BlockSpec((tm, tn), lambda i, j: (i, j)),
                   pl.BlockSpec((ng, tm), lambda i, j: (0, i)),
                   pl.BlockSpec((tm, tn), lambda i, j: (jnp.minimum(i, nct - 1), 0))),
        scratch_shapes=[pltpu.VMEM((tm, d), BF16), pltpu.VMEM((tm, tn), F32)],
        compiler_params=_cparams(("arbitrary", "arbitrary")),
        name="in_projection",
    )(x, modl, modl, g.reshape(1, d), w_in_p, w_gt, *rope_tabs)


ROW_A, ROW_M, ROW_E, ROW_WK, ROW_MPREV, ROW_DECAY = range(6)
LOG2_E = 1.4426950408889634
FINISH_ROWS = 512


def _log_sigmoid(x):
    return jnp.minimum(x, 0.0) - jnp.log1p(jnp.exp(-jnp.abs(x)))


def _cumsum_lanes(x, cum_mat):
    hi = x.astype(BF16)
    r1 = x - hi.astype(F32)
    mid = r1.astype(BF16)
    lo = (r1 - mid.astype(F32)).astype(BF16)
    return _dot(hi, cum_mat) + _dot(mid, cum_mat) + _dot(lo, cum_mat)


def _cummax_lanes_both(xs):
    fwd, bwd = xs
    n = fwd.shape[1]
    lane = lax.broadcasted_iota(jnp.int32, fwd.shape, 1)
    k = 1
    while k < n:
        fwd = jnp.maximum(fwd, jnp.where(lane >= k, pltpu.roll(fwd, k, 1), -jnp.inf))
        bwd = jnp.maximum(bwd, jnp.where(lane < n - k, pltpu.roll(bwd, n - k, 1), -jnp.inf))
        k *= 2
    return [fwd, bwd]


def _col_bcast(row, n):
    return jnp.transpose(jnp.broadcast_to(row, (n, row.shape[1])))


def _mlstm_kernel(*refs, seq_len, n_sub, heads, unroll, init_state, emit_state, aliased):
    it = iter(refs)
    q_ref, k_ref, v_ref, om_ref, gt_ref, bias_ref, gm_ref = (next(it) for _ in range(7))
    c0_ref = n0_ref = m0_ref = None
    if init_state:
        c0_ref, n0_ref, m0_ref = next(it), next(it), next(it)
    if aliased:
        next(it)
    o_ref = next(it)
    co_ref = no_ref = mo_ref = None
    if emit_state:
        co_ref, no_ref, mo_ref = next(it), next(it), next(it)
    hs_f, hs_b, cext, rows_scr = next(it), next(it), next(it), next(it)

    L = dk = M_HEAD_DIM
    head = pl.program_id(1)
    nc = seq_len // L

    tt = lax.broadcasted_iota(jnp.int32, (L, L), 0)
    ss = lax.broadcasted_iota(jnp.int32, (L, L), 1)
    cum_mats = ((tt <= ss).astype(BF16), (tt >= ss).astype(BF16))
    causal = (ss <= tt, ss >= tt)

    dirs = (0, 1)
    row_i = [d * 2 * heads + head for d in dirs]
    xi = [gt_ref[r] + bias_ref[r] for r in row_i]
    lf = [_log_sigmoid(gt_ref[r + heads] + bias_ref[r + heads]) for r in row_i]
    b = [_cumsum_lanes(lf[d], cum_mats[d]) for d in dirs]
    tot = [jnp.sum(lf[d], axis=1, keepdims=True) for d in dirs]
    a = [xi[d] - b[d] for d in dirs]
    g = [tot[d] - b[d] + xi[d] for d in dirs]
    gmax = [jnp.max(g[d], axis=1, keepdims=True) for d in dirs]
    arun = _cummax_lanes_both(a)
    a_l2 = [a[d] * LOG2_E for d in dirs]
    for s in range(n_sub):
        m = [m0_ref[s, d][:, 0:1] if init_state else jnp.zeros((1, 1), F32) for d in dirs]
        for step in range(nc):
            for d in dirs:
                r = s * nc + (step if d == 0 else nc - 1 - step)
                m_new = jnp.maximum(tot[d][r:r + 1] + m[d], gmax[d][r:r + 1])
                m_row = jnp.maximum(arun[d][r:r + 1], m[d])
                rows_scr[d, r, ROW_A:ROW_A + 1, :] = a_l2[d][r:r + 1]
                rows_scr[d, r, ROW_M:ROW_M + 1, :] = m_row * LOG2_E
                rows_scr[d, r, ROW_E:ROW_E + 1, :] = -(b[d][r:r + 1] + m_row) * LOG2_E
                rows_scr[d, r, ROW_WK:ROW_WK + 1, :] = jnp.exp(g[d][r:r + 1] - m_new)
                rows_scr[d, r, ROW_MPREV:ROW_MPREV + 1, :] = jnp.broadcast_to(m[d] * LOG2_E, (1, L))
                rows_scr[d, r, ROW_DECAY:ROW_DECAY + 1, :] = jnp.broadcast_to(
                    jnp.exp(tot[d][r:r + 1] + m[d] - m_new), (1, L))
                m[d] = m_new
        for d in dirs:
            if emit_state:
                mo_ref[s, d] = jnp.broadcast_to(m[d], (1, LANES))
            if init_state:
                cext[s, d, :, 0:dk] = c0_ref[s, d]
                cext[s, d, :, dk:2 * dk] = _col_bcast(n0_ref[s, d], dk)
            else:
                cext[s, d] = jnp.zeros((dk, 2 * dk), F32)

    ones_ext = jnp.ones((L, dk), BF16)

    def prepare(s, d, c):
        rows = pl.ds(pl.multiple_of(s * seq_len + c * L, L), L)
        tile = rows_scr[d, s * nc + c]
        q = q_ref[rows, :].astype(BF16)
        k = k_ref[rows, :]
        v_ext = jnp.concatenate([v_ref[rows, :].astype(BF16), ones_ext], axis=1)
        m_b = _col_bcast(tile[ROW_M:ROW_M + 1], dk)
        floor = jnp.exp2(_col_bcast(tile[ROW_E:ROW_E + 1], dk))
        dmat = jnp.exp2(jnp.where(causal[d], tile[ROW_A:ROW_A + 1] - m_b, -jnp.inf))
        sc = (_dot_nt(q, k.astype(BF16)) * dmat).astype(BF16)
        w_int = jnp.exp2(tile[ROW_MPREV:ROW_MPREV + 1] - m_b)
        lhs = jnp.concatenate([sc, (w_int * q.astype(F32)).astype(BF16)], axis=1)
        kt_w = (jnp.transpose(k.astype(F32)) * tile[ROW_WK:ROW_WK + 1]).astype(BF16)
        return rows, lhs, v_ext, floor, kt_w, tile[ROW_DECAY:ROW_DECAY + 1, 0:1]

    def apply(s, d, prepared):
        rows, lhs, v_ext, floor, kt_w, decay = prepared
        c_prev = cext[s, d]
        nd = _dot(lhs, jnp.concatenate([v_ext, c_prev.astype(BF16)], axis=0))
        den = jnp.maximum(jnp.abs(nd[:, dk:2 * dk]), floor)
        (hs_f if d == 0 else hs_b)[rows, :] = nd[:, 0:dk] / den
        cext[s, d] = decay * c_prev + _dot(kt_w, v_ext)

    def chunk_body(i, carry):
        todo = []
        for u in range(unroll):
            c = i * unroll + u
            for s in range(n_sub):
                todo.append((s, 0, prepare(s, 0, c)))
                todo.append((s, 1, prepare(s, 1, nc - 1 - c)))
        for s, d, prepared in todo:
            apply(s, d, prepared)
        return carry

    lax.fori_loop(0, nc // unroll, chunk_body, 0)

    def finish(r, carry):
        rows = pl.ds(pl.multiple_of(r * FINISH_ROWS, FINISH_ROWS), FINISH_ROWS)
        y = _rms(hs_f[rows, :] + hs_b[rows, :]) * gm_ref[...] * jax.nn.sigmoid(om_ref[rows, :].astype(F32))
        o_ref[rows, :] = y.astype(o_ref.dtype)
        return carry

    lax.fori_loop(0, n_sub * seq_len // FINISH_ROWS, finish, 0)

    if emit_state:
        for s in range(n_sub):
            for d in range(2):
                co_ref[s, d] = cext[s, d, :, 0:dk]
                no_ref[s, d] = jnp.transpose(cext[s, d, :, dk:2 * dk])[0:1, :]


def _mlstm(cfg, p, gt, bias_rows, g_mlstm, *, n_seq, seq_len, row0, n_sub, unroll, name, state0=None,
           emit_state=False, into=None):
    heads, dk = cfg.m_heads, M_HEAD_DIM
    L = dk
    nc = seq_len // L
    blk_rows = n_sub * seq_len
    sb = row0 // blk_rows
    assert sb * blk_rows == row0 and nc * L == seq_len and n_seq % n_sub == 0 and nc % unroll == 0
    assert (n_sub * nc) % 8 == 0
    n_steps = n_seq // n_sub

    def col(block0):
        return pl.BlockSpec((blk_rows, dk), lambda b, h: (sb + b, block0 + h))

    gt4 = gt[:, row0:row0 + n_seq * seq_len].reshape(cfg.n_gates, n_steps, n_sub * nc, L)
    in_specs = [col(0), col(heads), col(2 * heads), col(3 * heads),
                pl.BlockSpec((cfg.n_gates, None, n_sub * nc, L), lambda b, h: (0, b, 0, 0)),
                pl.BlockSpec((cfg.n_gates, 1, L), lambda b, h: (0, 0, 0)),
                pl.BlockSpec((None, 1, dk), lambda b, h: (h, 0, 0))]
    args = [p, p, p, p, gt4, bias_rows, g_mlstm.reshape(heads, 1, dk)]
    state_spec = lambda r, c: pl.BlockSpec((n_sub, 2, None, r, c), lambda b, h: (b, 0, h, 0, 0))
    if state0 is not None:
        c0, n0, m0 = state0
        in_specs += [state_spec(dk, dk), state_spec(1, dk), state_spec(1, LANES)]
        args += [c0, n0.reshape(n_seq, 2, heads, 1, dk),
                 jnp.broadcast_to(m0[..., None, None], (n_seq, 2, heads, 1, LANES))]
    aliases = {}
    if into is not None:
        aliases = {len(args): 0}
        in_specs.append(pl.BlockSpec(memory_space=pl.ANY))
        args.append(into)
    out_shape = [jax.ShapeDtypeStruct((cfg.n_tok, cfg.m_width), BF16)]
    out_specs = [pl.BlockSpec((blk_rows, dk), lambda b, h: (sb + b, h))]
    if emit_state:
        out_shape += [jax.ShapeDtypeStruct((n_seq, 2, heads, dk, dk), F32),
                      jax.ShapeDtypeStruct((n_seq, 2, heads, 1, dk), F32),
                      jax.ShapeDtypeStruct((n_seq, 2, heads, 1, LANES), F32)]
        out_specs += [state_spec(dk, dk), state_spec(1, dk), state_spec(1, LANES)]
    kern = functools.partial(_mlstm_kernel, seq_len=seq_len, n_sub=n_sub, heads=heads, unroll=unroll,
                             init_state=state0 is not None, emit_state=emit_state, aliased=into is not None)
    return pl.pallas_call(
        kern,
        out_shape=tuple(out_shape),
        grid=(n_steps, heads),
        in_specs=in_specs,
        out_specs=tuple(out_specs),
        input_output_aliases=aliases,
        scratch_shapes=[pltpu.VMEM((blk_rows, dk), F32), pltpu.VMEM((blk_rows, dk), F32),
                        pltpu.VMEM((n_sub, 2, dk, 2 * dk), F32), pltpu.VMEM((2, n_sub * nc, 8, L), F32)],
        compiler_params=_cparams(("arbitrary", "arbitrary")),
        name=name,
    )(*args)


def _rope_tables(cfg):
    half = M_HEAD_DIM // 2
    nf = half // 2
    t = jnp.arange(cfg.dec_seq)
    inv = ROPE_BASE ** (-jnp.arange(nf, dtype=F32) / nf)
    ang_r = (t // GRID_W).astype(F32)[:, None] * inv[None, :]
    ang_c = (t % GRID_W).astype(F32)[:, None] * inv[None, :]
    cos = jnp.concatenate([jnp.cos(ang_r)] * 2 + [jnp.cos(ang_c)] * 2, axis=-1)
    sin = jnp.concatenate([-jnp.sin(ang_r), jnp.sin(ang_r), -jnp.sin(ang_c), jnp.sin(ang_c)], axis=-1)
    cos = jnp.concatenate([jnp.ones((cfg.n_ctx, M_HEAD_DIM), F32), jnp.tile(cos, (cfg.dec_batch, 1))], axis=0)
    sin = jnp.concatenate([jnp.zeros((cfg.n_ctx, M_HEAD_DIM), F32), jnp.tile(sin, (cfg.dec_batch, 1))], axis=0)
    return cos, sin


CONV_ROWS = 32


def _conv_kernel(a_ref, g_ref, ap_ref, gp_ref, an_ref, gn_ref, wdw_ref, bdw_ref, lng_ref, lnb_ref, wpw_ref,
                 o_ref, u_scr, *, n_ctx_tiles, ctx_tiles_per_seq, lat_tiles_per_seq):
    i = pl.program_id(0)
    tt = a_ref.shape[0]
    halo = CONV_HALO
    pos = jnp.where(i < n_ctx_tiles, i % ctx_tiles_per_seq, (i - n_ctx_tiles) % lat_tiles_per_seq)
    per_seq = jnp.where(i < n_ctx_tiles, ctx_tiles_per_seq, lat_tiles_per_seq)
    keep_prev = (pos > 0).astype(F32)
    keep_next = (pos < per_seq - 1).astype(F32)

    glu = lambda a, g: a[...].astype(F32) * jax.nn.sigmoid(g[...].astype(F32))
    u_scr[0, 0:halo, :] = glu(ap_ref, gp_ref) * keep_prev
    u_scr[0, halo:halo + tt, :] = glu(a_ref, g_ref)
    u_scr[0, halo + tt:2 * halo + tt, :] = glu(an_ref, gn_ref) * keep_next
    n_rows = tt + 2 * halo
    window = u_scr[0]
    for r in range(1, SUBLANES):
        u_scr[r] = pltpu.roll(window, n_rows - r, 0)

    for blk in range(tt // CONV_ROWS):
        base = blk * CONV_ROWS
        acc = jnp.broadcast_to(bdw_ref[...], (CONV_ROWS, a_ref.shape[1]))
        for tap in range(CONV_TAPS):
            start = base + halo - CONV_TAPS // 2 + tap
            r, aligned = start % SUBLANES, start - start % SUBLANES
            w_tap = jnp.concatenate([wdw_ref[tap]] * (CONV_ROWS // SUBLANES), axis=0)
            acc = acc + u_scr[r, aligned:aligned + CONV_ROWS, :] * w_tap
        mu = jnp.mean(acc, axis=-1, keepdims=True)
        cen = acc - mu
        var = jnp.mean(cen * cen, axis=-1, keepdims=True)
        y = cen * lax.rsqrt(var + EPS) * lng_ref[...] + lnb_ref[...]
        y = y * jax.nn.sigmoid(y)
        o_ref[base:base + CONV_ROWS, :] = _dot(y.astype(BF16), wpw_ref[...]).astype(o_ref.dtype)


def _conv_module(cfg, p, w_dw, b_dw, ln_g, ln_b, w_pw):
    tt, cv, halo = cfg.conv_tt, cfg.cv_width, CONV_HALO
    n_tiles = cfg.n_tok // tt
    cb = cfg.col_ca // cv
    hb = tt // halo
    last_halo = cfg.n_tok // halo - 1
    main = lambda c: pl.BlockSpec((tt, cv), lambda i: (i, c))
    prev = lambda c: pl.BlockSpec((halo, cv), lambda i: (jnp.maximum(i * hb - 1, 0), c))
    nxt = lambda c: pl.BlockSpec((halo, cv), lambda i: (jnp.minimum((i + 1) * hb, last_halo), c))
    vec = lambda: pl.BlockSpec((1, cv), lambda i: (0, 0))
    kern = functools.partial(_conv_kernel, n_ctx_tiles=cfg.n_ctx // tt, ctx_tiles_per_seq=cfg.seq // tt,
                             lat_tiles_per_seq=cfg.dec_seq // tt)
    return pl.pallas_call(
        kern,
        out_shape=jax.ShapeDtypeStruct((cfg.n_tok, cv), BF16),
        grid=(n_tiles,),
        in_specs=[main(cb), main(cb + 1), prev(cb), prev(cb + 1), nxt(cb), nxt(cb + 1),
                  pl.BlockSpec((CONV_TAPS, SUBLANES, cv), lambda i: (0, 0, 0)), vec(), vec(), vec(),
                  pl.BlockSpec((cv, cv), lambda i: (0, 0))],
        out_specs=pl.BlockSpec((tt, cv), lambda i: (i, 0)),
        scratch_shapes=[pltpu.VMEM((SUBLANES, tt + 2 * halo, cv), F32)],
        compiler_params=_cparams(("arbitrary",)),
        name="conv_module",
    )(p, p, p, p, p, p, jnp.broadcast_to(w_dw[:, None, :], (CONV_TAPS, SUBLANES, cv)),
      b_dw.reshape(1, cv), ln_g.reshape(1, cv), ln_b.reshape(1, cv), w_pw)


def _softmax_pv(scores, values):
    mx = functools.reduce(jnp.maximum, [jnp.max(s, axis=1, keepdims=True) for s in scores])
    ps = [jnp.exp(s - mx) for s in scores]
    den = functools.reduce(jnp.add, [jnp.sum(p, axis=1, keepdims=True) for p in ps])
    num = functools.reduce(jnp.add, [_dot(p.astype(BF16), v) for p, v in zip(ps, values)])
    return num / den


def _head_pair(q2, scores_fn, values):
    first_half = lax.broadcasted_iota(jnp.int32, q2.shape, 1) < NA_HEAD_DIM
    zero = jnp.zeros_like(q2)
    out_even = _softmax_pv(scores_fn(jnp.where(first_half, q2, zero), 0), values)
    out_odd = _softmax_pv(scores_fn(jnp.where(first_half, zero, q2), 1), values)
    return jnp.where(first_half, out_even, out_odd)


def _ctx_attn_kernel(q_ref, k_ref, v_ref, o_ref, *, heads):
    dh = NA_HEAD_DIM
    q = (q_ref[...] * (dh ** -0.5)).astype(BF16)
    k = k_ref[...].astype(BF16)
    v = v_ref[...].astype(BF16)
    outs = []
    for hp in range(heads // 2):
        sl = slice(hp * LANES, (hp + 1) * LANES)
        outs.append(_head_pair(q[:, sl], lambda qh, half: [_dot_nt(qh, k[:, sl])], [v[:, sl]]))
    o_ref[...] = jnp.concatenate(outs, axis=1).astype(o_ref.dtype)


def _ctx_attention(cfg, p):
    s, nw = cfg.seq, cfg.na_width
    qb, kb = cfg.col_qn // nw, cfg.col_kv // nw
    spec = lambda c: pl.BlockSpec((s, nw), lambda b: (b, c))
    return pl.pallas_call(
        functools.partial(_ctx_attn_kernel, heads=cfg.na_heads),
        out_shape=jax.ShapeDtypeStruct((cfg.n_tok, nw), BF16),
        grid=(cfg.batch,),
        in_specs=[spec(qb), spec(kb), spec(kb + 1)],
        out_specs=pl.BlockSpec((s, nw), lambda b: (b, 0)),
        compiler_params=_cparams(("arbitrary",)),
        name="context_attention",
    )(p, p, p)


NA_QROWS = 8


def _na_kernel(q_ref, kp_ref, kc_ref, kn_ref, vp_ref, vc_ref, vn_ref, kx_ref, vx_ref, bt_ref, into_ref, o_ref,
               kw_scr, vw_scr, *, heads, grid_rows):
    del into_ref
    dh = NA_HEAD_DIM
    blk = NA_QROWS * GRID_W
    win = WIN_ROWS * GRID_W
    m = pl.program_id(1)
    for slot, (kr, vr) in enumerate(((kp_ref, vp_ref), (kc_ref, vc_ref), (kn_ref, vn_ref))):
        kw_scr[slot * blk:(slot + 1) * blk, :] = kr[...].astype(BF16)
        vw_scr[slot * blk:(slot + 1) * blk, :] = vr[...].astype(BF16)
    kx = kx_ref[...].astype(BF16)
    vx = vx_ref[...].astype(BF16)

    def row_body(i, carry):
        r = m * NA_QROWS + i
        rs = jnp.clip(r - WIN_ROWS // 2, 0, grid_rows - WIN_ROWS)
        start = rs - r + (WIN_ROWS - 1)
        off = pl.multiple_of((rs - NA_QROWS * (m - 1)) * GRID_W, GRID_W)
        qrows = pl.ds(pl.multiple_of(i * GRID_W, GRID_W), GRID_W)
        q = (q_ref[qrows, :].astype(F32) * (dh ** -0.5 * LOG2_E)).astype(BF16)
        kwin = kw_scr[pl.ds(off, win), :]
        vwin = vw_scr[pl.ds(off, win), :]
        first_half = lax.broadcasted_iota(jnp.int32, (GRID_W, LANES), 1) < dh
        pairs = [slice(hp * LANES, (hp + 1) * LANES) for hp in range(heads // 2)]
        scores = []
        for hp, sl in enumerate(pairs):
            q2 = q[:, sl]
            zero = jnp.zeros_like(q2)
            qm = jnp.concatenate([jnp.where(first_half, q2, zero), jnp.where(first_half, zero, q2)], axis=0)
            scores.append((_dot_nt(kwin[:, sl], qm) + bt_ref[start, hp],
                           _dot_nt(kx[:, sl], qm)))
        probs = []
        for s_loc, s_ctx in scores:
            mx = jnp.maximum(jnp.max(s_loc, axis=0, keepdims=True), jnp.max(s_ctx, axis=0, keepdims=True))
            p_loc = jnp.exp2(s_loc - mx)
            p_ctx = jnp.exp2(s_ctx - mx)
            inv = 1.0 / (jnp.sum(p_loc, axis=0, keepdims=True) + jnp.sum(p_ctx, axis=0, keepdims=True))
            probs.append((p_loc.astype(BF16), p_ctx.astype(BF16), _col_bcast(inv, LANES)))
        outs = []
        for (p_loc, p_ctx, inv_col), sl in zip(probs, pairs):
            pv = (_dot_tn(p_loc, vwin[:, sl]) + _dot_tn(p_ctx, vx[:, sl])) * inv_col
            outs.append(jnp.where(first_half, pv[0:GRID_W], pv[GRID_W:2 * GRID_W]))
        o_ref[qrows, :] = jnp.concatenate(outs, axis=1).astype(o_ref.dtype)
        return carry

    lax.fori_loop(0, NA_QROWS, row_body, 0)


def _na_bias_table(rpb):
    heads = rpb.shape[0]
    qc = np.arange(GRID_W)[:, None]
    kc = np.arange(GRID_W)[None, :]
    cs = np.clip(qc - WIN_COLS // 2, 0, GRID_W - WIN_COLS)
    valid = (kc >= cs) & (kc < cs + WIN_COLS)
    pad = GRID_W - WIN_COLS
    flipped = jnp.pad(jnp.flip(rpb * LOG2_E, axis=-1), ((0, 0), (0, 0), (pad, pad)))
    tab = jnp.stack([flipped[..., GRID_W - 1 - k:2 * GRID_W - 1 - k] for k in range(GRID_W)], axis=-2)
    tab = jnp.where(jnp.asarray(valid.T)[None, None], tab, NEG_INF)
    tab = jnp.stack([tab[:, s:s + WIN_ROWS] for s in range(WIN_ROWS)], axis=0)
    tab = tab.reshape(WIN_ROWS, heads // 2, 2, WIN_ROWS, GRID_W, GRID_W)
    tab = tab.transpose(0, 1, 3, 4, 2, 5)
    return tab.reshape(WIN_ROWS, heads // 2, WIN_ROWS * GRID_W, 2 * GRID_W)


def _na_latent(cfg, p, k_ctx, v_ctx, bias_tab, into):
    nw, heads = cfg.na_width, cfg.na_heads
    blk = NA_QROWS * GRID_W
    grid_rows = cfg.dec_seq // GRID_W
    nblk = grid_rows // NA_QROWS
    qb, kb = cfg.col_qn // nw, cfg.col_kv // nw
    rb0 = cfg.n_ctx // blk
    assert rb0 * blk == cfg.n_ctx and nblk * NA_QROWS == grid_rows

    def rows(shift, c):
        return pl.BlockSpec((blk, nw), lambda b, m: (rb0 + b * nblk + jnp.clip(m + shift, 0, nblk - 1), c))

    ctx_spec = pl.BlockSpec((None, cfg.past_len, nw), lambda b, m: (b, 0, 0))
    return pl.pallas_call(
        functools.partial(_na_kernel, heads=heads, grid_rows=grid_rows),
        out_shape=jax.ShapeDtypeStruct((cfg.n_tok, nw), BF16),
        grid=(cfg.dec_batch, nblk),
        in_specs=[rows(0, qb), rows(-1, kb), rows(0, kb), rows(1, kb),
                  rows(-1, kb + 1), rows(0, kb + 1), rows(1, kb + 1), ctx_spec, ctx_spec,
                  pl.BlockSpec((WIN_ROWS, heads // 2, WIN_ROWS * GRID_W, 2 * GRID_W), lambda b, m: (0, 0, 0, 0)),
                  pl.BlockSpec(memory_space=pl.ANY)],
        out_specs=pl.BlockSpec((blk, nw), lambda b, m: (rb0 + b * nblk + m, 0)),
        scratch_shapes=[pltpu.VMEM((3 * blk, nw), BF16), pltpu.VMEM((3 * blk, nw), BF16)],
        input_output_aliases={10: 0},
        compiler_params=_cparams(("arbitrary", "arbitrary")),
        name="neighbourhood_attention",
    )(p, p, p, p, p, p, p, k_ctx, v_ctx, bias_tab, into)


OUT_ROWS = 256


def _outproj_kernel(ym_ref, yc_ref, yn_ref, x_ref, w_ref, ga_ref, gpost_ref, gpre_ref, sc_ref, sh_ref,
                    xo_ref, ho_ref, y_scr, *, m_width, cv_width):
    a, b = m_width, m_width + cv_width
    for blk in range(x_ref.shape[0] // OUT_ROWS):
        blk_rows = slice(blk * OUT_ROWS, (blk + 1) * OUT_ROWS)
        y_scr[blk_rows, :] = (_dot(ym_ref[blk_rows, :], w_ref[0:a, :]) + _dot(yc_ref[blk_rows, :], w_ref[a:b, :])
                              + _dot(yn_ref[blk_rows, :], w_ref[b:, :]))
        for r in range(OUT_ROWS // ROW_CHUNK):
            rows = slice(blk * OUT_ROWS + r * ROW_CHUNK, blk * OUT_ROWS + (r + 1) * ROW_CHUNK)
            x = x_ref[rows, :] + ga_ref[...] * (_rms(y_scr[rows, :]) * gpost_ref[...])
            xo_ref[rows, :] = x
            ho_ref[rows, :] = (_rms(x) * gpre_ref[...] * (1.0 + sc_ref[...]) + sh_ref[...]).astype(BF16)


def _outproj(cfg, ym, yc, yn, x, w_out, modl, g_post, g_pre):
    d, tm = cfg.d_model, cfg.tm_out
    rowblk = lambda w: pl.BlockSpec((tm, w), lambda i: (i, 0))
    vec = lambda: pl.BlockSpec((1, d), lambda i: (0, 0))
    return pl.pallas_call(
        functools.partial(_outproj_kernel, m_width=cfg.m_width, cv_width=cfg.cv_width),
        out_shape=(jax.ShapeDtypeStruct((cfg.n_tok, d), F32), jax.ShapeDtypeStruct((cfg.n_tok, d), BF16)),
        grid=(cfg.n_tok // tm,),
        in_specs=[rowblk(cfg.m_width), rowblk(cfg.cv_width), rowblk(cfg.na_width), rowblk(d),
                  pl.BlockSpec((d, d), lambda i: (0, 0)),
                  _mod_spec(cfg, 2, tm), vec(), vec(), _mod_spec(cfg, 4, tm), _mod_spec(cfg, 3, tm)],
        out_specs=(rowblk(d), rowblk(d)),
        scratch_shapes=[pltpu.VMEM((tm, d), F32)],
        compiler_params=_cparams(("arbitrary",)),
        name="out_projection",
    )(ym, yc, yn, x, w_out, modl, g_post.reshape(1, d), g_pre.reshape(1, d), modl, modl)


FFN_HIDDEN = 1024


def _ffn_kernel(h_ref, x_ref, w1_ref, w2_ref, ga_ref, g_ref, o_ref):
    kk = pl.program_id(1)
    acc_ref = o_ref

    @pl.when(kk == 0)
    def _():
        acc_ref[...] = jnp.zeros_like(acc_ref)

    last = pl.num_programs(1) - 1
    tm = x_ref.shape[0]

    def accumulate(rows):
        step = min(FFN_HIDDEN, w1_ref.shape[1])
        for hb in range(0, w1_ref.shape[1], step):
            hidden = slice(hb, hb + step)
            a = jnp.maximum(_dot(h_ref[rows, :], w1_ref[:, hidden]), 0.0)
            acc_ref[rows, :] += _dot((a * a).astype(BF16), w2_ref[hidden, :])

    @pl.when(kk < last)
    def _():
        accumulate(slice(0, tm))

    @pl.when(kk == last)
    def _():
        blk_rows = min(OUT_ROWS, tm)
        for blk in range(tm // blk_rows):
            accumulate(slice(blk * blk_rows, (blk + 1) * blk_rows))
            for r in range(blk * blk_rows, (blk + 1) * blk_rows, ROW_CHUNK):
                rows = slice(r, r + ROW_CHUNK)
                o_ref[rows, :] = x_ref[rows, :] + ga_ref[...] * (_rms(acc_ref[rows, :]) * g_ref[...])


def _ffn(cfg, h, x, w1, w2, modl, g_post, row0=0, n_rows=None):
    d, tm, th = cfg.d_model, cfg.tm_ff, cfg.th_ff
    n_rows = cfg.n_tok if n_rows is None else n_rows
    tile0 = row0 // tm
    assert tile0 * tm == row0 and n_rows % tm == 0
    return pl.pallas_call(
        _ffn_kernel,
        out_shape=jax.ShapeDtypeStruct((n_rows, d), F32),
        grid=(n_rows // tm, cfg.d_ff // th),
        in_specs=[pl.BlockSpec((tm, d), lambda i, k: (tile0 + i, 0)),
                  pl.BlockSpec((tm, d), lambda i, k: (tile0 + i, 0)),
                  pl.BlockSpec((d, th), lambda i, k: (0, k)),
                  pl.BlockSpec((th, d), lambda i, k: (k, 0)),
                  _mod_spec(cfg, 5, tm, tile0),
                  pl.BlockSpec((1, d), lambda i, k: (0, 0))],
        out_specs=pl.BlockSpec((tm, d), lambda i, k: (i, 0)),
        compiler_params=_cparams(("arbitrary", "arbitrary"), VMEM_LIMIT_MLP),
        name="relu2_mlp",
    )(h, x, w1, w2, modl, g_post.reshape(1, d))


def _prep_w_in(cfg, w_in_l):
    mw, cv, nw, ng = cfg.m_width, cfg.cv_width, cfg.na_width, cfg.n_gates
    g0 = 4 * mw
    c0 = g0 + ng
    n0 = c0 + 2 * cv
    gates = w_in_l[:, g0:c0]
    parts = [w_in_l[:, :g0], w_in_l[:, c0:n0], w_in_l[:, n0 + nw:n0 + 3 * nw], w_in_l[:, n0:n0 + nw], gates]
    pad = cfg.in_cols - (cfg.col_gates + ng)
    w_p = jnp.concatenate(parts + [jnp.zeros((cfg.d_model, pad), w_in_l.dtype)], axis=1)
    return w_p.astype(BF16), gates.T.astype(BF16)


def _forward(cfg, x_prompt, x_sample, c, cache_k, cache_v, state_C, state_n, state_m, c_ctx, w_ada, b_ada,
             g_pre_mix, g_post_mix, g_pre_mlp, g_post_mlp, w_in, b_gate, g_mlstm, w_dw, b_dw, ln_g, ln_b,
             w_pw, rpb, w_out, w_ff1, w_ff2):
    d = cfg.d_model
    x = jnp.concatenate([x_prompt.reshape(cfg.n_ctx, d), x_sample.reshape(cfg.n_lat, d)], axis=0)

    n_cond = 1 + cfg.dec_batch
    cond_rows = -(-n_cond // 8) * 8
    cond = jnp.concatenate([c_ctx[None, :], c, jnp.zeros((cond_rows - n_cond, d), F32)], axis=0)
    mod = _modulation(cond, w_ada, b_ada, tn=min(1024, 6 * d)).reshape(cfg.depth, cond_rows, 6, 1, d)

    rope_tabs = _rope_tables(cfg)
    nw = cfg.na_width
    ks, vs, cs, ns, ms = [], [], [], [], []
    for l in range(cfg.depth):
        w_in_p, w_gt = _prep_w_in(cfg, w_in[l])
        p, gt, kv_new = _inproj(cfg, x, mod[l], g_pre_mix[l], w_in_p, w_gt, rope_tabs)

        bias_rows = jnp.broadcast_to(b_gate[l].reshape(cfg.n_gates, 1, 1), (cfg.n_gates, 1, M_HEAD_DIM))
        ym_c, c_new, n_new, m_new = _mlstm(cfg, p, gt, bias_rows, g_mlstm[l], n_seq=cfg.batch, seq_len=cfg.seq,
                                           row0=0, n_sub=cfg.ctx_sub, unroll=1, name="mlstm_context",
                                           emit_state=True)
        (ym,) = _mlstm(cfg, p, gt, bias_rows, g_mlstm[l], n_seq=cfg.dec_batch, seq_len=cfg.dec_seq,
                       row0=cfg.n_ctx, n_sub=1, unroll=cfg.lat_unroll, name="mlstm_latent",
                       state0=(state_C[:, l], state_n[:, l], state_m[:, l]), into=ym_c)
        yc = _conv_module(cfg, p, w_dw[l], b_dw[l], ln_g[l], ln_b[l], w_pw[l].astype(BF16))
        yn = _na_latent(cfg, p, cache_k[:, l].reshape(cfg.dec_batch, cfg.past_len, nw),
                        cache_v[:, l].reshape(cfg.dec_batch, cfg.past_len, nw),
                        _na_bias_table(rpb[l]), into=_ctx_attention(cfg, p))

        x, h2 = _outproj(cfg, ym, yc, yn, x, w_out[l].astype(BF16), mod[l], g_post_mix[l], g_pre_mlp[l])
        ffn_args = (w_ff1[l].astype(BF16), w_ff2[l].astype(BF16), mod[l], g_post_mlp[l])
        if l + 1 < cfg.depth:
            x = _ffn(cfg, h2, x, *ffn_args)
        else:
            y_prompt = _ffn(cfg, h2, x, *ffn_args, row0=0, n_rows=cfg.n_ctx)
            y_sample = _ffn(cfg, h2, x, *ffn_args, row0=cfg.n_ctx, n_rows=cfg.n_lat)

        ks.append(kv_new.reshape(cfg.batch, cfg.seq, 2, cfg.na_heads, NA_HEAD_DIM))
        cs.append(c_new)
        ns.append(n_new[:, :, :, 0, :])
        ms.append(m_new[:, :, :, 0, 0])

    y_prompt = y_prompt.reshape(cfg.batch, cfg.seq, d)
    y_sample = y_sample.reshape(cfg.dec_batch, cfg.dec_seq, d)
    kv_all = jnp.stack(ks, axis=1)
    return (y_prompt, y_sample, kv_all[:, :, :, 0], kv_all[:, :, :, 1],
            jnp.stack(cs, axis=1), jnp.stack(ns, axis=1), jnp.stack(ms, axis=1))


def _make_cfg(x_prompt, x_sample, cache_k, w_in, w_ff1, w_pw, **tiles):
    batch, seq, d = x_prompt.shape
    dec_batch, dec_seq, _ = x_sample.shape
    return Cfg(d_model=d, batch=batch, seq=seq, depth=w_in.shape[0], dec_batch=dec_batch, dec_seq=dec_seq,
               past_len=cache_k.shape[2], m_heads=d // (2 * M_HEAD_DIM), cv_width=w_pw.shape[-1],
               na_heads=cache_k.shape[3], d_ff=w_ff1.shape[-1], **tiles)


def kernel(x_prompt, x_sample, c, cache_k, cache_v, state_C, state_n, state_m, c_ctx, w_ada, b_ada, g_pre_mix,
           g_post_mix, g_pre_mlp, g_post_mlp, w_in, b_gate, g_mlstm, w_dw, b_dw, ln_g, ln_b, w_pw, rpb, w_out,
           w_ff1, w_ff2):
    cfg = _make_cfg(x_prompt, x_sample, cache_k, w_in, w_ff1, w_pw,
                    tm_in=1024, tm_out=512, tm_ff=512, tn_in=1024, th_ff=2048, ctx_sub=4, lat_unroll=4,
                    conv_tt=256)
    return _forward(cfg, x_prompt, x_sample, c, cache_k, cache_v, state_C, state_n, state_m, c_ctx, w_ada, b_ada,
                    g_pre_mix, g_post_mix, g_pre_mlp, g_post_mlp, w_in, b_gate, g_mlstm, w_dw, b_dw, ln_g, ln_b,
                    w_pw, rpb, w_out, w_ff1, w_ff2)
```
